```python
import math
import jax
import jax.numpy as jnp
from jax import lax
import numpy as np

D_MODEL = 1024
BATCH = 32
SEQ = 256
DEPTH = 4
DEC_BATCH = 4
DEC_SEQ = 1024
PAST_LEN = 256

GRID_W = 64
N_MIXERS = 3
HEAD_DIM = 64
Q_BLOCK = 128
NORM_EPS = 1e-6
ROPE_BASE = 10000.0
NA_HEADS = 16
NA_ROWS = 8
NA_COLS = 16
SWA_Q_HEADS = 16
SWA_KV_HEADS = 4
SWA_GROUP = SWA_Q_HEADS // SWA_KV_HEADS
SWA_WINDOW = 128
SWA_BLOCK = 128
SSD_INNER = 2 * D_MODEL
SSD_HEADDIM = 64
SSD_HEADS = SSD_INNER // SSD_HEADDIM
SSD_GROUPS = 4
SSD_STATE = 128
SSD_CONV = 5
SSD_CHUNK = 128
SSD_CONV_DIM = SSD_INNER + 2 * SSD_GROUPS * SSD_STATE
SSD_IN_DIM = SSD_INNER + SSD_CONV_DIM + 2 * SSD_HEADS
FFN_DIM = 2816
N_EXPERTS = 8
TOP_K = 2
EXPERT_DIM = 3584
MOE_BLOCK = 128
N_NA = (DEPTH + 2) // 3
N_SWA = (DEPTH + 1) // 3
N_SSD = DEPTH // 3
N_DENSE = (DEPTH + 1) // 2
N_MOE = DEPTH // 2

kernel_name = 'hybrid_prefix_diffusion_trunk'


def rms_norm(x, gain):
    xf = x.astype(jnp.float32)
    y = xf * lax.rsqrt(jnp.mean(xf * xf, axis=-1, keepdims=True) + NORM_EPS)
    return (y * gain.astype(jnp.float32)).astype(x.dtype)


def adaln(cond, w, b):
    m = (jax.nn.silu(cond) @ w + b)[..., None, :]
    return jnp.split(m, 6, axis=-1)


def modulate(h, shift, scale):
    return h * (1.0 + scale) + shift


def softmax_with_sink(logits, sink):
    m = jnp.maximum(jnp.max(logits, axis=-1, keepdims=True), sink)
    e = jnp.exp(logits - m)
    return e / (jnp.sum(e, axis=-1, keepdims=True) + jnp.exp(sink - m))


def axial_rope_2d(x):
    L, hd = x.shape[1], x.shape[-1]
    quarter = hd // 4
    t = jnp.arange(L)
    pos = jnp.stack([t // GRID_W, t % GRID_W], axis=-1).astype(jnp.float32)
    inv = ROPE_BASE ** (-jnp.arange(quarter, dtype=jnp.float32) / quarter)
    ang = pos[:, :, None] * inv
    shp = (L,) + (1,) * (x.ndim - 3) + (2, quarter)
    cos, sin = jnp.cos(ang).reshape(shp), jnp.sin(ang).reshape(shp)
    xr = x.reshape(x.shape[:-1] + (2, 2, quarter)).astype(jnp.float32)
    x1, x2 = xr[..., 0, :], xr[..., 1, :]
    out = jnp.stack([x1 * cos - x2 * sin, x2 * cos + x1 * sin], axis=-2)
    return out.reshape(x.shape).astype(x.dtype)


def attend_blocks(q, k, v, sink=None):
    b, s, kh, g, hd = q.shape
    scale = hd ** -0.5
    qb = jnp.moveaxis(q.reshape(b, s // Q_BLOCK, Q_BLOCK, kh, g, hd), 1, 0)

    def block(q_n):
        logits = jnp.einsum('bqkgd,bpkd->bkgqp', q_n, k).astype(jnp.float32) * scale
        p = jax.nn.softmax(logits, axis=-1) if sink is None else softmax_with_sink(logits, sink)
        return jnp.einsum('bkgqp,bpkd->bqkgd', p.astype(v.dtype), v)

    o = lax.map(block, qb)
    return jnp.moveaxis(o, 0, 1).reshape(b, s, kh * g * hd)


def na_project(h, w_qkv, q_norm, k_norm):
    b, l, _ = h.shape
    qkv = (h @ w_qkv).reshape(b, l, 3, NA_HEADS, HEAD_DIM)
    return rms_norm(qkv[:, :, 0], q_norm), rms_norm(qkv[:, :, 1], k_norm), qkv[:, :, 2]


def na_context(h, w_qkv, w_o, q_norm, k_norm):
    q, k, v = na_project(h, w_qkv, q_norm, k_norm)
    o = attend_blocks(q[:, :, :, None], k, v)
    return o @ w_o, k, v


def na_latent(h, ctx_k, ctx_v, w_qkv, w_o, q_norm, k_norm, rpb):
    b, l, _ = h.shape
    rows = l // GRID_W
    kr = min(NA_ROWS, rows)
    q, k, v = na_project(h, w_qkv, q_norm, k_norm)
    to_grid = lambda t: t.reshape(b, rows, GRID_W, NA_HEADS, HEAD_DIM)
    q, k, v = to_grid(q), to_grid(k), to_grid(v)
    col = jnp.arange(GRID_W)
    col_start = jnp.clip(col - NA_COLS // 2, 0, GRID_W - NA_COLS)
    col_ok = (col[None, :] >= col_start[:, None]) & (col[None, :] < col_start[:, None] + NA_COLS)
    dc = jnp.clip(col[None, :] - col[:, None], 1 - NA_COLS, NA_COLS - 1) + NA_COLS - 1
    scale = HEAD_DIM ** -0.5
    n_nb = kr * GRID_W

    def row_block(r):
        start = jnp.clip(r - kr // 2, 0, rows - kr)
        q_r = lax.dynamic_index_in_dim(q, r, axis=1, keepdims=False)
        k_r = lax.dynamic_slice_in_dim(k, start, kr, axis=1)
        v_r = lax.dynamic_slice_in_dim(v, start, kr, axis=1)
        dr = start - r + jnp.arange(kr) + NA_ROWS - 1
        bias = rpb[:, dr[None, :, None], dc[:, None, :]].astype(jnp.float32)
        s_nb = jnp.einsum('bqhd,bjkhd->bhqjk', q_r, k_r).astype(jnp.float32) * scale + bias
        s_nb = jnp.where(col_ok[:, None, :], s_nb, -jnp.inf)
        s_ctx = jnp.einsum('bqhd,bphd->bhqp', q_r, ctx_k).astype(jnp.float32) * scale
        logits = jnp.concatenate([s_nb.reshape(b, NA_HEADS, GRID_W, n_nb), s_ctx], axis=-1)
        p = jax.nn.softmax(logits, axis=-1).astype(v.dtype)
        p_nb = p[..., :n_nb].reshape(b, NA_HEADS, GRID_W, kr, GRID_W)
        return (jnp.einsum('bhqjk,bjkhd->bqhd', p_nb, v_r)
                + jnp.einsum('bhqp,bphd->bqhd', p[..., n_nb:], ctx_v))

    o = lax.map(row_block, jnp.arange(rows))
    o = jnp.moveaxis(o, 0, 1).reshape(b, l, NA_HEADS * HEAD_DIM)
    return o @ w_o


def swa_project(h, w_qkv, q_norm, k_norm):
    b, l, _ = h.shape
    nq, nk = SWA_Q_HEADS * HEAD_DIM, SWA_KV_HEADS * HEAD_DIM
    qkv = h @ w_qkv
    q = rms_norm(qkv[..., :nq].reshape(b, l, SWA_KV_HEADS, SWA_GROUP, HEAD_DIM), q_norm)
    k = rms_norm(qkv[..., nq:nq + nk].reshape(b, l, SWA_KV_HEADS, HEAD_DIM), k_norm)
    v = qkv[..., nq + nk:].reshape(b, l, SWA_KV_HEADS, HEAD_DIM)
    return q, k, v


def swa_context(h, w_qkv, w_o, q_norm, k_norm, sink):
    q, k, v = swa_project(h, w_qkv, q_norm, k_norm)
    sink_b = sink.reshape(SWA_KV_HEADS, SWA_GROUP)[None, :, :, None, None].astype(jnp.float32)
    o = attend_blocks(q, k, v, sink_b)
    return o @ w_o, k, v


def swa_latent(h, ctx_k, ctx_v, w_qkv, w_o, q_norm, k_norm, sink):
    b, l, _ = h.shape
    q, k, v = swa_project(h, w_qkv, q_norm, k_norm)
    q, k = axial_rope_2d(q), axial_rope_2d(k)
    nb = l // SWA_BLOCK
    band = SWA_BLOCK + 2 * SWA_WINDOW
    padw = ((0, 0), (SWA_WINDOW, SWA_WINDOW), (0, 0), (0, 0))
    kp, vp = jnp.pad(k, padw), jnp.pad(v, padw)
    sink_b = sink.reshape(SWA_KV_HEADS, SWA_GROUP)[None, :, :, None, None].astype(jnp.float32)
    scale = HEAD_DIM ** -0.5

    def block(n):
        q0 = n * SWA_BLOCK
        q_n = lax.dynamic_slice_in_dim(q, q0, SWA_BLOCK, axis=1)
        k_n = lax.dynamic_slice_in_dim(kp, q0, band, axis=1)
        v_n = lax.dynamic_slice_in_dim(vp, q0, band, axis=1)
        qpos = q0 + jnp.arange(SWA_BLOCK)
        kpos = q0 - SWA_WINDOW + jnp.arange(band)
        ok = ((jnp.abs(qpos[:, None] - kpos[None, :]) <= SWA_WINDOW)
              & (kpos[None, :] >= 0) & (kpos[None, :] < l))
        s_band = jnp.einsum('bqkgd,bjkd->bkgqj', q_n, k_n).astype(jnp.float32) * scale
        s_band = jnp.where(ok, s_band, -jnp.inf)
        s_ctx = jnp.einsum('bqkgd,bpkd->bkgqp', q_n, ctx_k).astype(jnp.float32) * scale
        p = softmax_with_sink(jnp.concatenate([s_band, s_ctx], axis=-1), sink_b).astype(v.dtype)
        return (jnp.einsum('bkgqj,bjkd->bqkgd', p[..., :band], v_n)
                + jnp.einsum('bkgqp,bpkd->bqkgd', p[..., band:], ctx_v))

    o = lax.map(block, jnp.arange(nb))
    o = jnp.moveaxis(o, 0, 1).reshape(b, l, SWA_Q_HEADS * HEAD_DIM)
    return o @ w_o


def centred_depthwise_conv(x, w, bias):
    k = w.shape[0]
    y = lax.conv_general_dilated(x, w[:, None, :], window_strides=(1,), padding=[(k // 2, k // 2)],
                                 dimension_numbers=('NWC', 'WIO', 'NWC'),
                                 feature_group_count=x.shape[-1])
    return y + bias


def ssd_scan(x, dt, a, bm, cm, h0):
    b, l, nh, p = x.shape
    g, n = bm.shape[2], bm.shape[3]
    e, nc, q = nh // g, l // SSD_CHUNK, SSD_CHUNK
    xc = x.astype(jnp.float32).reshape(b, nc, q, g, e, p)
    dtc = dt.reshape(b, nc, q, g, e)
    bc = bm.astype(jnp.float32).reshape(b, nc, q, g, n)
    cc = cm.astype(jnp.float32).reshape(b, nc, q, g, n)
    xdt = xc * dtc[..., None]
    cs = jnp.cumsum(dtc * a.reshape(g, e), axis=2)
    lower = jnp.tril(jnp.ones((q, q), dtype=bool))
    seg = cs[:, :, :, None] - cs[:, :, None]
    decay = jnp.exp(jnp.where(lower[:, :, None, None], seg, -jnp.inf))
    cb = jnp.einsum('bcign,bcjgn->bcijg', cc, bc)
    y_diag = jnp.einsum('bcijge,bcjgep->bcigep', cb[..., None] * decay, xdt)
    to_end = jnp.exp(cs[:, :, -1:] - cs)
    chunk_states = jnp.einsum('bcjgn,bcjgep->bcgepn', bc, xdt * to_end[..., None])
    chunk_decay = jnp.exp(cs[:, :, -1])

    def step(h, inp):
        dec, st = inp
        return dec[..., None, None] * h + st, h

    h_final, h_in = lax.scan(step, h0.astype(jnp.float32).reshape(b, g, e, p, n),
                             (jnp.moveaxis(chunk_decay, 1, 0), jnp.moveaxis(chunk_states, 1, 0)))
    h_in = jnp.moveaxis(h_in, 0, 1)
    y_off = jnp.einsum('bcign,bcgepn->bcigep', cc, h_in) * jnp.exp(cs)[..., None]
    y = (y_diag + y_off).reshape(b, l, nh, p).astype(x.dtype)
    return y, h_final.reshape(b, nh, p, n)


def ssd_mixer(h, init_f, init_b, w_in, conv_w, conv_b, dt_bias, a_log, d_skip, norm_w, w_out):
    b, l, _ = h.shape
    proj = h @ w_in
    z = proj[..., :SSD_INNER]
    xbc = jax.nn.silu(centred_depthwise_conv(proj[..., SSD_INNER:SSD_INNER + SSD_CONV_DIM], conv_w, conv_b))
    dt = proj[..., SSD_INNER + SSD_CONV_DIM:].reshape(b, l, 2, SSD_HEADS)
    gn = SSD_GROUPS * SSD_STATE
    x = xbc[..., :SSD_INNER].reshape(b, l, SSD_HEADS, SSD_HEADDIM)
    bm = xbc[..., SSD_INNER:SSD_INNER + gn].reshape(b, l, SSD_GROUPS, SSD_STATE)
    cm = xbc[..., SSD_INNER + gn:].reshape(b, l, SSD_GROUPS, SSD_STATE)
    dt = jax.nn.softplus(dt.astype(jnp.float32) + dt_bias.astype(jnp.float32))
    a = -jnp.exp(a_log.astype(jnp.float32))
    flip = lambda t: jnp.flip(t, axis=1)
    y_f, s_f = ssd_scan(x, dt[:, :, 0], a[0], bm, cm, init_f)
    y_b, s_b = ssd_scan(flip(x), flip(dt[:, :, 1]), a[1], flip(bm), flip(cm), init_b)
    y = y_f + flip(y_b) + d_skip[:, None] * x
    y = y.reshape(b, l, SSD_INNER) * jax.nn.silu(z)
    y = rms_norm(y.reshape(b, l, SSD_GROUPS, SSD_INNER // SSD_GROUPS),
                 norm_w.reshape(SSD_GROUPS, SSD_INNER // SSD_GROUPS)).reshape(b, l, SSD_INNER)
    return y @ w_out, s_f, s_b


def swiglu(h, w_in, w_out):
    g, u = jnp.split(h @ w_in, 2, axis=-1)
    return (jax.nn.silu(g) * u) @ w_out


def moe_ffn(h, router, w_in, w_out):
    b, l, d = h.shape
    t = h.reshape(b * l, d)
    n_tok = b * l
    n_slot = n_tok * TOP_K
    logits = (t @ router).astype(jnp.float32)
    top_v, top_e = lax.top_k(logits, TOP_K)
    top_w = jax.nn.softmax(top_v, axis=-1)
    e_flat = top_e.reshape(-1)
    tok_flat = jnp.arange(n_slot, dtype=jnp.int32) // TOP_K
    order = jnp.argsort(e_flat)
    e_s, t_s, w_s = e_flat[order], tok_flat[order], top_w.reshape(-1)[order]
    counts = jnp.bincount(e_flat, length=N_EXPERTS)
    start = jnp.cumsum(counts) - counts
    padded = (counts + MOE_BLOCK - 1) // MOE_BLOCK * MOE_BLOCK
    pad_end = jnp.cumsum(padded)
    pad_start = pad_end - padded
    dest = pad_start[e_s] + jnp.arange(n_slot, dtype=jnp.int32) - start[e_s]
    n_pad = -(-n_slot // MOE_BLOCK) * MOE_BLOCK + N_EXPERTS * MOE_BLOCK
    slot_tok = jnp.full((n_pad,), n_tok, dtype=jnp.int32).at[dest].set(t_s)
    slot_w = jnp.zeros((n_pad,), jnp.float32).at[dest].set(w_s)
    n_blk = n_pad // MOE_BLOCK
    blk_start = jnp.arange(n_blk) * MOE_BLOCK
    blk_expert = jnp.minimum(jnp.sum(pad_end[None, :] <= blk_start[:, None], axis=1), N_EXPERTS - 1)
    t_ext = jnp.concatenate([t, jnp.zeros((1, d), t.dtype)], axis=0)
    xb = t_ext[slot_tok].reshape(n_blk, MOE_BLOCK, d)

    def expert_block(args):
        x_blk, e = args
        return swiglu(x_blk, w_in[e], w_out[e])

    yb = lax.map(expert_block, (xb, blk_expert)).reshape(n_pad, d)
    y = jax.ops.segment_sum(yb * slot_w[:, None].astype(yb.dtype), slot_tok, num_segments=n_tok + 1)[:n_tok]
    return y.astype(t.dtype).reshape(b, l, d)


def setup_inputs(seed: int = 0) -> dict:
    key = jax.random.key(seed)
    ks = iter(jax.random.split(key, 48))

    def nrm(shape, scale):
        return jax.random.normal(next(ks), shape, jnp.float32) * scale

    def gain(shape):
        return 1.0 + nrm(shape, 0.02)

    D = D_MODEL
    dt0 = jnp.exp(jax.random.uniform(next(ks), (N_SSD, 2, SSD_HEADS), jnp.float32,
                                     math.log(1e-3), math.log(1e-1)))
    a0 = jax.random.uniform(next(ks), (N_SSD, 2, SSD_HEADS), jnp.float32, 1.0, 16.0)
    return {
        'x_prompt': nrm((BATCH, SEQ, D), 1.0),
        'x_sample': nrm((DEC_BATCH, DEC_SEQ, D), 1.0),
        'cache_na_k': nrm((DEC_BATCH, N_NA, PAST_LEN, NA_HEADS, HEAD_DIM), 1.0),
        'cache_na_v': nrm((DEC_BATCH, N_NA, PAST_LEN, NA_HEADS, HEAD_DIM), 1.0),
        'cache_swa_k': nrm((DEC_BATCH, N_SWA, PAST_LEN, SWA_KV_HEADS, HEAD_DIM), 1.0),
        'cache_swa_v': nrm((DEC_BATCH, N_SWA, PAST_LEN, SWA_KV_HEADS, HEAD_DIM), 1.0),
        'state_ssd_fwd': nrm((DEC_BATCH, N_SSD, SSD_HEADS, SSD_HEADDIM, SSD_STATE), 0.1),
        'state_ssd_bwd': nrm((DEC_BATCH, N_SSD, SSD_HEADS, SSD_HEADDIM, SSD_STATE), 0.1),
        'c': nrm((DEC_BATCH, D), 1.0),
        'c_ctx': nrm((D,), 1.0),
        'ada_w': nrm((DEPTH, D, 6 * D), 0.5 * D ** -0.5),
        'ada_b': nrm((DEPTH, 6 * D), 0.02),
        'norm_mix': gain((DEPTH, D)),
        'norm_ffn': gain((DEPTH, D)),
        'na_w_qkv': nrm((N_NA, D, 3 * NA_HEADS * HEAD_DIM), D ** -0.5),
        'na_w_o': nrm((N_NA, NA_HEADS * HEAD_DIM, D), (NA_HEADS * HEAD_DIM) ** -0.5),
        'na_q_norm': gain((N_NA, HEAD_DIM)),
        'na_k_norm': gain((N_NA, HEAD_DIM)),
        'na_rpb': nrm((N_NA, NA_HEADS, 2 * NA_ROWS - 1, 2 * NA_COLS - 1), 0.1),
        'swa_w_qkv': nrm((N_SWA, D, (SWA_Q_HEADS + 2 * SWA_KV_HEADS) * HEAD_DIM), D ** -0.5),
        'swa_w_o': nrm((N_SWA, SWA_Q_HEADS * HEAD_DIM, D), (SWA_Q_HEADS * HEAD_DIM) ** -0.5),
        'swa_q_norm': gain((N_SWA, HEAD_DIM)),
        'swa_k_norm': gain((N_SWA, HEAD_DIM)),
        'swa_sink': nrm((N_SWA, SWA_Q_HEADS), 1.0),
        'ssd_w_in': nrm((N_SSD, D, SSD_IN_DIM), D ** -0.5),
        'ssd_conv_w': nrm((N_SSD, SSD_CONV, SSD_CONV_DIM), SSD_CONV ** -0.5),
        'ssd_conv_b': nrm((N_SSD, SSD_CONV_DIM), 0.02),
        'ssd_dt_bias': dt0 + jnp.log(-jnp.expm1(-dt0)),
        'ssd_a_log': jnp.log(a0),
        'ssd_d': 1.0 + nrm((N_SSD, SSD_HEADS), 0.1),
        'ssd_norm': gain((N_SSD, SSD_INNER)),
        'ssd_w_out': nrm((N_SSD, SSD_INNER, D), SSD_INNER ** -0.5),
        'ffn_w_in': nrm((N_DENSE, D, 2 * FFN_DIM), D ** -0.5),
        'ffn_w_out': nrm((N_DENSE, FFN_DIM, D), FFN_DIM ** -0.5),
        'moe_router': nrm((N_MOE, D, N_EXPERTS), D ** -0.5),
        'moe_w_in': nrm((N_MOE, N_EXPERTS, D, 2 * EXPERT_DIM), D ** -0.5),
        'moe_w_out': nrm((N_MOE, N_EXPERTS, EXPERT_DIM, D), EXPERT_DIM ** -0.5),
    }


def reference(x_prompt, x_sample, cache_na_k, cache_na_v, cache_swa_k, cache_swa_v,
              state_ssd_fwd, state_ssd_bwd, c, c_ctx, ada_w, ada_b, norm_mix, norm_ffn,
              na_w_qkv, na_w_o, na_q_norm, na_k_norm, na_rpb,
              swa_w_qkv, swa_w_o, swa_q_norm, swa_k_norm, swa_sink,
              ssd_w_in, ssd_conv_w, ssd_conv_b, ssd_dt_bias, ssd_a_log, ssd_d, ssd_norm, ssd_w_out,
              ffn_w_in, ffn_w_out, moe_router, moe_w_in, moe_w_out):
    yp, ys = x_prompt, x_sample
    new_na_k, new_na_v, new_swa_k, new_swa_v, new_fwd, new_bwd = [], [], [], [], [], []
    for i in range(DEPTH):
        kind, j = i % N_MIXERS, i // N_MIXERS
        mp = adaln(c_ctx, ada_w[i], ada_b[i])
        ms = adaln(c, ada_w[i], ada_b[i])
        hp = modulate(rms_norm(yp, norm_mix[i]), mp[0], mp[1])
        hs = modulate(rms_norm(ys, norm_mix[i]), ms[0], ms[1])
        if kind == 0:
            op, kc, vc = na_context(hp, na_w_qkv[j], na_w_o[j], na_q_norm[j], na_k_norm[j])
            os_ = na_latent(hs, cache_na_k[:, j], cache_na_v[:, j], na_w_qkv[j], na_w_o[j],
                            na_q_norm[j], na_k_norm[j], na_rpb[j])
            new_na_k.append(kc)
            new_na_v.append(vc)
        elif kind == 1:
            op, kc, vc = swa_context(hp, swa_w_qkv[j], swa_w_o[j], swa_q_norm[j], swa_k_norm[j], swa_sink[j])
            os_ = swa_latent(hs, cache_swa_k[:, j], cache_swa_v[:, j], swa_w_qkv[j], swa_w_o[j],
                             swa_q_norm[j], swa_k_norm[j], swa_sink[j])
            new_swa_k.append(kc)
            new_swa_v.append(vc)
        else:
            zero_state = jnp.zeros((yp.shape[0], SSD_HEADS, SSD_HEADDIM, SSD_STATE), jnp.float32)
            op, sf, sb = ssd_mixer(hp, zero_state, zero_state, ssd_w_in[j], ssd_conv_w[j], ssd_conv_b[j],
                                   ssd_dt_bias[j], ssd_a_log[j], ssd_d[j], ssd_norm[j], ssd_w_out[j])
            os_, _, _ = ssd_mixer(hs, state_ssd_fwd[:, j], state_ssd_bwd[:, j], ssd_w_in[j], ssd_conv_w[j],
                                  ssd_conv_b[j], ssd_dt_bias[j], ssd_a_log[j], ssd_d[j], ssd_norm[j], ssd_w_out[j])
            new_fwd.append(sf)
            new_bwd.append(sb)
        yp = yp + mp[2] * op
        ys = ys + ms[2] * os_
        hp = modulate(rms_norm(yp, norm_ffn[i]), mp[3], mp[4])
        hs = modulate(rms_norm(ys, norm_ffn[i]), ms[3], ms[4])
        if i % 2 == 0:
            fp = swiglu(hp, ffn_w_in[i // 2], ffn_w_out[i // 2])
            fs = swiglu(hs, ffn_w_in[i // 2], ffn_w_out[i // 2])
        else:
            fp = moe_ffn(hp, moe_router[i // 2], moe_w_in[i // 2], moe_w_out[i // 2])
            fs = moe_ffn(hs, moe_router[i // 2], moe_w_in[i // 2], moe_w_out[i // 2])
        yp = yp + mp[5] * fp
        ys = ys + ms[5] * fs
    return (yp, ys, jnp.stack(new_na_k, axis=1), jnp.stack(new_na_v, axis=1),
            jnp.stack(new_swa_k, axis=1), jnp.stack(new_swa_v, axis=1),
            jnp.stack(new_fwd, axis=1), jnp.stack(new_bwd, axis=1))
```

```python
import functools

import jax
import jax.numpy as jnp
from jax import lax
from jax.experimental import pallas as pl
from jax.experimental.pallas import tpu as pltpu

F32 = jnp.float32
BF16 = jnp.bfloat16
HIGHEST = lax.Precision.HIGHEST

D_MODEL = 1024
BATCH = 32
SEQ = 256
DEPTH = 4
DEC_BATCH = 4
DEC_SEQ = 1024
PAST_LEN = 256
GRID_W = 64
N_MIXERS = 3
HEAD_DIM = 64
NORM_EPS = 1e-6
ROPE_BASE = 10000.0
NA_HEADS = 16
NA_ROWS = 8
NA_COLS = 16
SWA_Q_HEADS = 16
SWA_KV_HEADS = 4
SWA_GROUP = SWA_Q_HEADS // SWA_KV_HEADS
SWA_WINDOW = 128
SWA_BLOCK = 128
SSD_INNER = 2 * D_MODEL
SSD_HEADDIM = 64
SSD_HEADS = SSD_INNER // SSD_HEADDIM
SSD_GROUPS = 4
SSD_STATE = 128
SSD_CONV = 5
SSD_CHUNK = 128
SSD_CONV_DIM = SSD_INNER + 2 * SSD_GROUPS * SSD_STATE
FFN_DIM = 2816
N_EXPERTS = 8
TOP_K = 2
EXPERT_DIM = 3584

N_PROMPT = BATCH * SEQ
N_SAMPLE = DEC_BATCH * DEC_SEQ
N_TOK = N_PROMPT + N_SAMPLE
N_GROUPS = 8
LANES = 128
GRID_ROWS = DEC_SEQ // GRID_W
HEADS_PER_GROUP = SSD_HEADS // SSD_GROUPS

VMEM_LIMIT = 56 * 1024 * 1024
TM_LIN = 1024
TN_LIN = 512
TM_FFN = 512
TF_DENSE = 1408
TF_MOE = 512
TM_MOE = 512
N_SLOT = N_TOK * TOP_K
N_PAD = N_SLOT + N_EXPERTS * TM_MOE
TG = 512
TM_SSD_OUT = 256


def _params(sem):
    return pltpu.CompilerParams(dimension_semantics=sem, vmem_limit_bytes=VMEM_LIMIT)


def _group_of_row(start):
    return jnp.where(start < N_PROMPT, 0, 1 + (start - N_PROMPT) // DEC_SEQ)


def _mod_spec(layer, which, tm, width, ncol_arg=True):
    if ncol_arg:
        return pl.BlockSpec((None, None, None, 1, width),
                            lambda i, j: (layer, which, _group_of_row(i * tm), 0, j))
    return pl.BlockSpec((None, None, None, 1, width),
                        lambda i: (layer, which, _group_of_row(i * tm), 0, 0))


def _silu(x):
    return x / (1.0 + jnp.exp(-x))


def _norm_mod(x, gain, shift, scale):
    y = x * lax.rsqrt(jnp.mean(x * x, axis=-1, keepdims=True) + NORM_EPS)
    return (y * gain) * (1.0 + scale) + shift


def _adaln_kernel(c_ref, w_ref, b_ref, o_ref):
    s = _silu(c_ref[...])
    o_ref[...] = jnp.dot(s, w_ref[...], precision=HIGHEST, preferred_element_type=F32) + b_ref[...]


def _adaln(cond, ada_w, ada_b):
    tn = 1024
    out = pl.pallas_call(
        _adaln_kernel,
        grid=(DEPTH, 6 * D_MODEL // tn),
        in_specs=[
            pl.BlockSpec((N_GROUPS, D_MODEL), lambda l, j: (0, 0)),
            pl.BlockSpec((None, D_MODEL, tn), lambda l, j: (l, 0, j)),
            pl.BlockSpec((None, 1, tn), lambda l, j: (l, 0, j)),
        ],
        out_specs=pl.BlockSpec((None, N_GROUPS, tn), lambda l, j: (l, 0, j)),
        out_shape=jax.ShapeDtypeStruct((DEPTH, N_GROUPS, 6 * D_MODEL), F32),
        compiler_params=_params(("arbitrary", "arbitrary")),
        name="adaln",
    )(cond, ada_w, ada_b.reshape(DEPTH, 1, 6 * D_MODEL))
    out = out.reshape(DEPTH, N_GROUPS, 6, D_MODEL)
    return jnp.transpose(out, (0, 2, 1, 3)).reshape(DEPTH, 6, N_GROUPS, 1, D_MODEL)


def _nm_matmul_kernel(x_ref, g_ref, sh_ref, sc_ref, w_ref, o_ref, h_ref):
    @pl.when(pl.program_id(1) == 0)
    def _():
        h_ref[...] = _norm_mod(x_ref[...], g_ref[...], sh_ref[...], sc_ref[...]).astype(BF16)

    o_ref[...] = jnp.dot(h_ref[...], w_ref[...].astype(BF16), preferred_element_type=F32)


def _nm_matmul(x, gain, mod, layer, w, n_out, tn=TN_LIN, name="nm_matmul"):
    tm = TM_LIN
    return pl.pallas_call(
        _nm_matmul_kernel,
        grid=(N_TOK // tm, n_out // tn),
        in_specs=[
            pl.BlockSpec((tm, D_MODEL), lambda i, j: (i, 0)),
            pl.BlockSpec((1, D_MODEL), lambda i, j: (0, 0)),
            pl.BlockSpec((None, None, None, 1, D_MODEL),
                         lambda i, j: (layer, 0, _group_of_row(i * tm), 0, 0)),
            pl.BlockSpec((None, None, None, 1, D_MODEL),
                         lambda i, j: (layer, 1, _group_of_row(i * tm), 0, 0)),
            pl.BlockSpec((D_MODEL, tn), lambda i, j: (0, j)),
        ],
        out_specs=pl.BlockSpec((tm, tn), lambda i, j: (i, j)),
        out_shape=jax.ShapeDtypeStruct((N_TOK, n_out), F32),
        scratch_shapes=[pltpu.VMEM((tm, D_MODEL), BF16)],
        compiler_params=_params(("arbitrary", "arbitrary")),
        name=name,
    )(x, gain, mod, mod, w)


def _linear_res_kernel(x_ref, w_ref, r_ref, g_ref, o_ref):
    acc = jnp.dot(x_ref[...], w_ref[...].astype(BF16), preferred_element_type=F32)
    o_ref[...] = r_ref[...] + g_ref[...] * acc


def _linear_res(x_bf, w, res, mod, layer, which):
    tm, tn = TM_LIN, TN_LIN
    k = x_bf.shape[1]
    return pl.pallas_call(
        _linear_res_kernel,
        grid=(N_TOK // tm, D_MODEL // tn),
        in_specs=[
            pl.BlockSpec((tm, k), lambda i, j: (i, 0)),
            pl.BlockSpec((k, tn), lambda i, j: (0, j)),
            pl.BlockSpec((tm, tn), lambda i, j: (i, j)),
            _mod_spec(layer, which, tm, tn),
        ],
        out_specs=pl.BlockSpec((tm, tn), lambda i, j: (i, j)),
        out_shape=jax.ShapeDtypeStruct((N_TOK, D_MODEL), F32),
        compiler_params=_params(("arbitrary", "arbitrary")),
        name="linear_res",
    )(x_bf, w, res, mod)


def _group_sumsq(x):
    r = lax.broadcasted_iota(jnp.int32, (LANES, LANES), 0) // HEAD_DIM
    c = lax.broadcasted_iota(jnp.int32, (LANES, LANES), 1) // HEAD_DIM
    ones = jnp.where(r == c, 1.0, 0.0).astype(BF16)
    outs = []
    for t in range(x.shape[1] // LANES):
        x2 = x[:, t * LANES:(t + 1) * LANES]
        x2 = x2 * x2
        hi = x2.astype(BF16)
        lo = (x2 - hi.astype(F32)).astype(BF16)
        outs.append(jnp.dot(hi, ones, preferred_element_type=F32)
                    + jnp.dot(lo, ones, preferred_element_type=F32))
    return jnp.concatenate(outs, axis=1) if len(outs) > 1 else outs[0]


def _head_norm(x, gain):
    return x * lax.rsqrt(_group_sumsq(x) * (1.0 / HEAD_DIM) + NORM_EPS) * gain


def _rope(x, cos, sin):
    w = x.shape[1]
    lane = lax.broadcasted_iota(jnp.int32, x.shape, 1)
    partner = jnp.where((lane % 32) < 16, pltpu.roll(x, w - 16, axis=1), pltpu.roll(x, 16, axis=1))
    return x * cos + partner * sin


def _prep_kernel(*refs, rope):
    if rope:
        q_ref, k_ref, v_ref, qg_ref, kg_ref, cos_ref, sin_ref, qo_ref, ko_ref, vo_ref = refs
    else:
        q_ref, k_ref, v_ref, qg_ref, kg_ref, qo_ref, ko_ref, vo_ref = refs
    q = _head_norm(q_ref[...], qg_ref[...])
    k = _head_norm(k_ref[...], kg_ref[...])
    if rope:
        q = _rope(q, cos_ref[...], sin_ref[...])
        kw = k.shape[1]
        k = _rope(k, cos_ref[:, :kw], sin_ref[:, :kw])
    qo_ref[...] = (q * (HEAD_DIM ** -0.5)).astype(qo_ref.dtype)
    ko_ref[...] = k.astype(ko_ref.dtype)
    vo_ref[...] = v_ref[...].astype(vo_ref.dtype)


def _prep(qkv, q_gain, k_gain, kw, row0, n_rows, kv_dtype, rope_tabs=None):
    tm = 512
    qw = NA_HEADS * HEAD_DIM
    r0 = row0 // tm
    kb, vb = qw // kw, qw // kw + 1
    in_specs = [
        pl.BlockSpec((tm, qw), lambda i: (r0 + i, 0)),
        pl.BlockSpec((tm, kw), lambda i: (r0 + i, kb)),
        pl.BlockSpec((tm, kw), lambda i: (r0 + i, vb)),
        pl.BlockSpec((1, qw), lambda i: (0, 0)),
        pl.BlockSpec((1, kw), lambda i: (0, 0)),
    ]
    args = [qkv, qkv, qkv, jnp.tile(q_gain, qw // HEAD_DIM)[None], jnp.tile(k_gain, kw // HEAD_DIM)[None]]
    if rope_tabs is not None:
        nb = DEC_SEQ // tm
        in_specs += [pl.BlockSpec((tm, qw), lambda i: (i % nb, 0))] * 2
        args += list(rope_tabs)
    return pl.pallas_call(
        functools.partial(_prep_kernel, rope=rope_tabs is not None),
        grid=(n_rows // tm,),
        in_specs=in_specs,
        out_specs=[
            pl.BlockSpec((tm, qw), lambda i: (i, 0)),
            pl.BlockSpec((tm, kw), lambda i: (i, 0)),
            pl.BlockSpec((tm, kw), lambda i: (i, 0)),
        ],
        out_shape=[
            jax.ShapeDtypeStruct((n_rows, qw), BF16),
            jax.ShapeDtypeStruct((n_rows, kw), kv_dtype),
            jax.ShapeDtypeStruct((n_rows, kw), kv_dtype),
        ],
        compiler_params=_params(("arbitrary",)),
        name="qk_prep",
    )(*args)


def _rope_tables():
    quarter = HEAD_DIM // 4
    t = jnp.arange(DEC_SEQ)
    pos = jnp.stack([t // GRID_W, t % GRID_W], axis=-1).astype(F32)
    inv = ROPE_BASE ** (-jnp.arange(quarter, dtype=F32) / quarter)
    ang = pos[:, :, None] * inv
    cos, sin = jnp.cos(ang), jnp.sin(ang)
    cos64 = jnp.concatenate([cos[:, 0], cos[:, 0], cos[:, 1], cos[:, 1]], axis=1)
    sin64 = jnp.concatenate([-sin[:, 0], sin[:, 0], -sin[:, 1], sin[:, 1]], axis=1)
    reps = NA_HEADS
    return jnp.tile(cos64, (1, reps)), jnp.tile(sin64, (1, reps))


def _nt_dot(a, b):
    return lax.dot_general(a, b, (((1,), (1,)), ((), ())), preferred_element_type=F32)


def _ctx_attn_kernel(*refs, group, use_sink):
    if use_sink:
        q_ref, k_ref, v_ref, sink_ref, o_ref = refs
    else:
        q_ref, k_ref, v_ref, o_ref = refs
    n_q = q_ref.shape[1] // HEAD_DIM
    for h in range(n_q):
        kh = h // group
        q = q_ref[:, h * HEAD_DIM:(h + 1) * HEAD_DIM]
        k = k_ref[:, kh * HEAD_DIM:(kh + 1) * HEAD_DIM].astype(BF16)
        v = v_ref[:, kh * HEAD_DIM:(kh + 1) * HEAD_DIM].astype(BF16)
        s = _nt_dot(q, k)
        m = jnp.max(s, axis=-1, keepdims=True)
        if use_sink:
            m = jnp.maximum(m, sink_ref[h])
        e = jnp.exp(s - m)
        l = jnp.sum(e, axis=-1, keepdims=True)
        if use_sink:
            l = l + jnp.exp(sink_ref[h] - m)
        o = jnp.dot(e.astype(BF16), v, preferred_element_type=F32)
        o_ref[:, h * HEAD_DIM:(h + 1) * HEAD_DIM] = (o / l).astype(o_ref.dtype)


def _ctx_attn(q, k, v, sink=None):
    kw = k.shape[1]
    group = q.shape[1] // kw
    in_specs = [
        pl.BlockSpec((SEQ, q.shape[1]), lambda b: (b, 0)),
        pl.BlockSpec((SEQ, kw), lambda b: (b, 0)),
        pl.BlockSpec((SEQ, kw), lambda b: (b, 0)),
    ]
    args = [q, k, v]
    if sink is not None:
        in_specs.append(pl.BlockSpec(memory_space=pltpu.SMEM))
        args.append(sink)
    return pl.pallas_call(
        functools.partial(_ctx_attn_kernel, group=group, use_sink=sink is not None),
        grid=(BATCH,),
        in_specs=in_specs,
        out_specs=pl.BlockSpec((SEQ, q.shape[1]), lambda b: (b, 0)),
        out_shape=jax.ShapeDtypeStruct((N_PROMPT, q.shape[1]), BF16),
        compiler_params=_params(("arbitrary",)),
        name="ctx_attn",
    )(*args)


def _na_latent_kernel(q_ref, k_ref, v_ref, ck_ref, cv_ref, bias_ref, o_ref):
    r = pl.program_id(1)
    kr = NA_ROWS
    start = jnp.clip(r - kr // 2, 0, GRID_ROWS - kr)
    row0 = pl.multiple_of(start * GRID_W, GRID_W)
    for h in range(NA_HEADS):
        sl = slice(h * HEAD_DIM, (h + 1) * HEAD_DIM)
        q = q_ref[:, sl]
        k = k_ref[pl.ds(row0, kr * GRID_W), sl]
        v = v_ref[pl.ds(row0, kr * GRID_W), sl]
        s_nb = _nt_dot(q, k) + bias_ref[h]
        s_cx = _nt_dot(q, ck_ref[:, sl])
        m = jnp.maximum(jnp.max(s_nb, axis=-1, keepdims=True), jnp.max(s_cx, axis=-1, keepdims=True))
        e_nb = jnp.exp(s_nb - m)
        e_cx = jnp.exp(s_cx - m)
        l = jnp.sum(e_nb, axis=-1, keepdims=True) + jnp.sum(e_cx, axis=-1, keepdims=True)
        o = (jnp.dot(e_nb.astype(BF16), v, preferred_element_type=F32)
             + jnp.dot(e_cx.astype(BF16), cv_ref[:, sl], preferred_element_type=F32))
        o_ref[:, sl] = (o / l).astype(o_ref.dtype)


def _na_bias_table(rpb):
    col = jnp.arange(GRID_W)
    col_start = jnp.clip(col - NA_COLS // 2, 0, GRID_W - NA_COLS)
    col_ok = (col[None, :] >= col_start[:, None]) & (col[None, :] < col_start[:, None] + NA_COLS)
    dc = jnp.clip(col[None, :] - col[:, None], 1 - NA_COLS, NA_COLS - 1) + NA_COLS - 1
    t = jnp.where(col_ok[None, None], rpb[:, :, dc].astype(F32), -jnp.inf)
    d = jnp.arange(NA_ROWS)[:, None] + jnp.arange(NA_ROWS)[None, :]
    tb = t[:, d]
    tb = jnp.transpose(tb, (0, 1, 3, 2, 4))
    return tb.reshape(NA_HEADS, NA_ROWS, GRID_W, NA_ROWS * GRID_W)


def _na_latent(q, k, v, ck, cv, bias):
    w = NA_HEADS * HEAD_DIM

    def bias_map(b, r):
        start = jnp.clip(r - NA_ROWS // 2, 0, GRID_ROWS - NA_ROWS)
        return (0, start - r + NA_ROWS - 1, 0, 0)

    return pl.pallas_call(
        _na_latent_kernel,
        grid=(DEC_BATCH, GRID_ROWS),
        in_specs=[
            pl.BlockSpec((GRID_W, w), lambda b, r: (b * GRID_ROWS + r, 0)),
            pl.BlockSpec((DEC_SEQ, w), lambda b, r: (b, 0)),
            pl.BlockSpec((DEC_SEQ, w), lambda b, r: (b, 0)),
            pl.BlockSpec((None, PAST_LEN, w), lambda b, r: (b, 0, 0)),
            pl.BlockSpec((None, PAST_LEN, w), lambda b, r: (b, 0, 0)),
            pl.BlockSpec((NA_HEADS, None, GRID_W, NA_ROWS * GRID_W), bias_map),
        ],
        out_specs=pl.BlockSpec((GRID_W, w), lambda b, r: (b * GRID_ROWS + r, 0)),
        out_shape=jax.ShapeDtypeStruct((N_SAMPLE, w), BF16),
        compiler_params=_params(("arbitrary", "arbitrary")),
        name="na_latent",
    )(q, k, v, ck, cv, bias)


SWA_SPAN = SWA_BLOCK + 2 * SWA_WINDOW


def _swa_latent_kernel(q_ref, k_ref, v_ref, ck_ref, cv_ref, sink_ref, o_ref):
    n = pl.program_id(1)
    k0 = pl.multiple_of(jnp.clip(n - 1, 0, DEC_SEQ // SWA_BLOCK - SWA_SPAN // SWA_BLOCK) * SWA_BLOCK,
                        SWA_BLOCK)
    qpos = n * SWA_BLOCK + lax.broadcasted_iota(jnp.int32, (SWA_BLOCK, SWA_SPAN), 0)
    kpos = k0 + lax.broadcasted_iota(jnp.int32, (SWA_BLOCK, SWA_SPAN), 1)
    ok = jnp.abs(qpos - kpos) <= SWA_WINDOW
    for kh in range(SWA_KV_HEADS):
        ksl = slice(kh * HEAD_DIM, (kh + 1) * HEAD_DIM)
        k = k_ref[pl.ds(k0, SWA_SPAN), ksl]
        v = v_ref[pl.ds(k0, SWA_SPAN), ksl]
        ck = ck_ref[:, ksl]
        cv = cv_ref[:, ksl]
        for g in range(SWA_GROUP):
            h = kh * SWA_GROUP + g
            sl = slice(h * HEAD_DIM, (h + 1) * HEAD_DIM)
            q = q_ref[:, sl]
            sink = sink_ref[h]
            s_b = jnp.where(ok, _nt_dot(q, k), -jnp.inf)
            s_c = _nt_dot(q, ck)
            m = jnp.maximum(jnp.maximum(jnp.max(s_b, axis=-1, keepdims=True),
                                        jnp.max(s_c, axis=-1, keepdims=True)), sink)
            e_b = jnp.exp(s_b - m)
            e_c = jnp.exp(s_c - m)
            l = (jnp.sum(e_b, axis=-1, keepdims=True) + jnp.sum(e_c, axis=-1, keepdims=True)
                 + jnp.exp(sink - m))
            o = (jnp.dot(e_b.astype(BF16), v, preferred_element_type=F32)
                 + jnp.dot(e_c.astype(BF16), cv, preferred_element_type=F32))
            o_ref[:, sl] = (o / l).astype(o_ref.dtype)


def _swa_latent(q, k, v, ck, cv, sink):
    qw = SWA_Q_HEADS * HEAD_DIM
    kw = SWA_KV_HEADS * HEAD_DIM
    nb = DEC_SEQ // SWA_BLOCK
    return pl.pallas_call(
        _swa_latent_kernel,
        grid=(DEC_BATCH, nb),
        in_specs=[
            pl.BlockSpec((SWA_BLOCK, qw), lambda b, n: (b * nb + n, 0)),
            pl.BlockSpec((DEC_SEQ, kw), lambda b, n: (b, 0)),
            pl.BlockSpec((DEC_SEQ, kw), lambda b, n: (b, 0)),
            pl.BlockSpec((None, PAST_LEN, kw), lambda b, n: (b, 0, 0)),
            pl.BlockSpec((None, PAST_LEN, kw), lambda b, n: (b, 0, 0)),
            pl.BlockSpec(memory_space=pltpu.SMEM),
        ],
        out_specs=pl.BlockSpec((SWA_BLOCK, qw), lambda b, n: (b * nb + n, 0)),
        out_shape=jax.ShapeDtypeStruct((N_SAMPLE, qw), BF16),
        compiler_params=_params(("arbitrary", "arbitrary")),
        name="swa_latent",
    )(q, k, v, ck, cv, sink)


def _swiglu_accumulate(h_ref, wg_ref, wu_ref, wo_ref, acc_ref, j):
    h = h_ref[...]
    g = jnp.dot(h, wg_ref[...].astype(BF16), preferred_element_type=F32)
    u = jnp.dot(h, wu_ref[...].astype(BF16), preferred_element_type=F32)
    a = (_silu(g) * u).astype(BF16)
    part = jnp.dot(a, wo_ref[...].astype(BF16), preferred_element_type=F32)

    @pl.when(j == 0)
    def _():
        acc_ref[...] = part

    @pl.when(j > 0)
    def _():
        acc_ref[...] += part


def _dense_ffn_kernel(x_ref, g_ref, sh_ref, sc_ref, wg_ref, wu_ref, wo_ref, gate_ref, o_ref,
                      h_ref, acc_ref):
    j = pl.program_id(1)

    @pl.when(j == 0)
    def _():
        h_ref[...] = _norm_mod(x_ref[...], g_ref[...], sh_ref[...], sc_ref[...]).astype(BF16)

    _swiglu_accumulate(h_ref, wg_ref, wu_ref, wo_ref, acc_ref, j)

    @pl.when(j == pl.num_programs(1) - 1)
    def _():
        o_ref[...] = x_ref[...] + gate_ref[...] * acc_ref[...]


def _dense_ffn(x, gain, mod, layer, w_in, w_out):
    tm, tf = TM_FFN, TF_DENSE
    nf = FFN_DIM // tf

    def mspec(which):
        return pl.BlockSpec((None, None, None, 1, D_MODEL),
                            lambda i, j: (layer, which, _group_of_row(i * tm), 0, 0))

    return pl.pallas_call(
        _dense_ffn_kernel,
        grid=(N_TOK // tm, nf),
        in_specs=[
            pl.BlockSpec((tm, D_MODEL), lambda i, j: (i, 0)),
            pl.BlockSpec((1, D_MODEL), lambda i, j: (0, 0)),
            mspec(3), mspec(4),
            pl.BlockSpec((D_MODEL, tf), lambda i, j: (0, j)),
            pl.BlockSpec((D_MODEL, tf), lambda i, j: (0, nf + j)),
            pl.BlockSpec((tf, D_MODEL), lambda i, j: (j, 0)),
            mspec(5),
        ],
        out_specs=pl.BlockSpec((tm, D_MODEL), lambda i, j: (i, 0)),
        out_shape=jax.ShapeDtypeStruct((N_TOK, D_MODEL), F32),
        scratch_shapes=[pltpu.VMEM((tm, D_MODEL), BF16), pltpu.VMEM((tm, D_MODEL), F32)],
        compiler_params=_params(("arbitrary", "arbitrary")),
        name="dense_ffn",
    )(x, gain, mod, mod, w_in, w_in, w_out, mod)


def _moe_ffn_kernel(te_ref, tv_ref, x_ref, wg_ref, wu_ref, wo_ref, o_ref, h_ref, acc_ref):
    i, j = pl.program_id(0), pl.program_id(1)
    valid = tv_ref[i] == 1
    last = j == pl.num_programs(1) - 1

    @pl.when(valid)
    def _():
        @pl.when(j == 0)
        def _():
            h_ref[...] = x_ref[...].astype(BF16)

        _swiglu_accumulate(h_ref, wg_ref, wu_ref, wo_ref, acc_ref, j)

        @pl.when(last)
        def _():
            o_ref[...] = acc_ref[...]

    @pl.when(jnp.logical_not(valid) & last)
    def _():
        o_ref[...] = jnp.zeros_like(o_ref)


def _moe_ffn(xg, tile_expert, tile_valid, w_in, w_out):
    tm, tf = TM_MOE, TF_MOE
    nf = EXPERT_DIM // tf

    def jj(i, j, tv):
        return jnp.where(tv[i] == 1, j, nf - 1)

    grid_spec = pltpu.PrefetchScalarGridSpec(
        num_scalar_prefetch=2,
        grid=(N_PAD // tm, nf),
        in_specs=[
            pl.BlockSpec((tm, D_MODEL), lambda i, j, te, tv: (i, 0)),
            pl.BlockSpec((None, D_MODEL, tf), lambda i, j, te, tv: (te[i], 0, jj(i, j, tv))),
            pl.BlockSpec((None, D_MODEL, tf), lambda i, j, te, tv: (te[i], 0, nf + jj(i, j, tv))),
            pl.BlockSpec((None, tf, D_MODEL), lambda i, j, te, tv: (te[i], jj(i, j, tv), 0)),
        ],
        out_specs=pl.BlockSpec((tm, D_MODEL), lambda i, j, te, tv: (i, 0)),
        scratch_shapes=[pltpu.VMEM((tm, D_MODEL), BF16), pltpu.VMEM((tm, D_MODEL), F32)],
    )
    return pl.pallas_call(
        _moe_ffn_kernel,
        grid_spec=grid_spec,
        out_shape=jax.ShapeDtypeStruct((N_PAD, D_MODEL), F32),
        compiler_params=_params(("arbitrary", "arbitrary")),
        name="moe_ffn",
    )(tile_expert, tile_valid, xg, w_in, w_in, w_out)


def _router_kernel(x_ref, g_ref, sh_ref, sc_ref, wr_ref, h_ref, e_ref, w_ref):
    h = _norm_mod(x_ref[...], g_ref[...], sh_ref[...], sc_ref[...])
    h_ref[...] = h
    logits = jnp.dot(h, wr_ref[...], precision=HIGHEST, preferred_element_type=F32)
    lane = lax.broadcasted_iota(jnp.int32, logits.shape, 1)
    logits = jnp.where(lane < N_EXPERTS, logits, -jnp.inf)
    m1 = jnp.max(logits, axis=-1, keepdims=True)
    i1 = jnp.min(jnp.where(logits == m1, lane, LANES), axis=-1, keepdims=True)
    rest = jnp.where(lane == i1, -jnp.inf, logits)
    m2 = jnp.max(rest, axis=-1, keepdims=True)
    i2 = jnp.min(jnp.where(rest == m2, lane, LANES), axis=-1, keepdims=True)
    e2 = jnp.exp(m2 - m1)
    den = 1.0 + e2
    e_ref[...] = jnp.where(lane == 0, i1, jnp.where(lane == 1, i2, 0))
    w_ref[...] = jnp.where(lane == 0, 1.0 / den, jnp.where(lane == 1, e2 / den, 0.0))


def _router(x, gain, mod, layer, w_router):
    tm = 512
    wr = jnp.pad(w_router, ((0, 0), (0, LANES - N_EXPERTS)))

    def mspec(which):
        return pl.BlockSpec((None, None, None, 1, D_MODEL),
                            lambda i: (layer, which, _group_of_row(i * tm), 0, 0))

    return pl.pallas_call(
        _router_kernel,
        grid=(N_TOK // tm,),
        in_specs=[
            pl.BlockSpec((tm, D_MODEL), lambda i: (i, 0)),
            pl.BlockSpec((1, D_MODEL), lambda i: (0, 0)),
            mspec(3), mspec(4),
            pl.BlockSpec((D_MODEL, LANES), lambda i: (0, 0)),
        ],
        out_specs=[
            pl.BlockSpec((tm, D_MODEL), lambda i: (i, 0)),
            pl.BlockSpec((tm, LANES), lambda i: (i, 0)),
            pl.BlockSpec((tm, LANES), lambda i: (i, 0)),
        ],
        out_shape=[
            jax.ShapeDtypeStruct((N_TOK, D_MODEL), F32),
            jax.ShapeDtypeStruct((N_TOK, LANES), jnp.int32),
            jax.ShapeDtypeStruct((N_TOK, LANES), F32),
        ],
        compiler_params=_params(("arbitrary",)),
        name="router",
    )(x, gain, mod, mod, wr)


def _row_copy(src_hbm, row, dst_ref, r, sem):
    return pltpu.make_async_copy(src_hbm.at[pl.ds(row, 1), :], dst_ref.at[pl.ds(r, 1), :], sem)


def _gather_kernel(idx_ref, x_hbm, o_ref, sem):
    base = pl.program_id(0) * TG

    def issue(r, carry):
        _row_copy(x_hbm, idx_ref[base + r], o_ref, r, sem).start()
        return carry

    lax.fori_loop(0, TG, issue, 0)

    def drain(r, carry):
        _row_copy(x_hbm, 0, o_ref, r, sem).wait()
        return carry

    lax.fori_loop(0, TG, drain, 0)


def _gather_rows(x, idx):
    n = idx.shape[0]
    grid_spec = pltpu.PrefetchScalarGridSpec(
        num_scalar_prefetch=1,
        grid=(n // TG,),
        in_specs=[pl.BlockSpec(memory_space=pl.ANY)],
        out_specs=pl.BlockSpec((TG, x.shape[1]), lambda i, idx: (i, 0)),
        scratch_shapes=[pltpu.SemaphoreType.DMA(())],
    )
    return pl.pallas_call(
        _gather_kernel,
        grid_spec=grid_spec,
        out_shape=jax.ShapeDtypeStruct((n, x.shape[1]), x.dtype),
        compiler_params=_params(("arbitrary",)),
        name="gather_rows",
    )(idx, x)


def _combine_kernel(pos_ref, y_hbm, res_ref, gate_ref, w_ref, o_ref, a_ref, b_ref, sem):
    base = pl.program_id(0) * TG

    def issue(r, carry):
        _row_copy(y_hbm, pos_ref[2 * (base + r)], a_ref, r, sem.at[0]).start()
        _row_copy(y_hbm, pos_ref[2 * (base + r) + 1], b_ref, r, sem.at[1]).start()
        return carry

    lax.fori_loop(0, TG, issue, 0)

    def drain(r, carry):
        _row_copy(y_hbm, 0, a_ref, r, sem.at[0]).wait()
        _row_copy(y_hbm, 0, b_ref, r, sem.at[1]).wait()
        return carry

    lax.fori_loop(0, TG, drain, 0)
    w = w_ref[...]
    mix = w[:, 0:1] * a_ref[...] + w[:, 1:2] * b_ref[...]
    o_ref[...] = res_ref[...] + gate_ref[...] * mix


def _combine(yb, pos, res, mod, layer, top_w):
    grid_spec = pltpu.PrefetchScalarGridSpec(
        num_scalar_prefetch=1,
        grid=(N_TOK // TG,),
        in_specs=[
            pl.BlockSpec(memory_space=pl.ANY),
            pl.BlockSpec((TG, D_MODEL), lambda i, p: (i, 0)),
            pl.BlockSpec((None, None, None, 1, D_MODEL),
                         lambda i, p: (layer, 5, _group_of_row(i * TG), 0, 0)),
            pl.BlockSpec((TG, LANES), lambda i, p: (i, 0)),
        ],
        out_specs=pl.BlockSpec((TG, D_MODEL), lambda i, p: (i, 0)),
        scratch_shapes=[pltpu.VMEM((TG, D_MODEL), F32), pltpu.VMEM((TG, D_MODEL), F32),
                        pltpu.SemaphoreType.DMA((2,))],
    )
    return pl.pallas_call(
        _combine_kernel,
        grid_spec=grid_spec,
        out_shape=jax.ShapeDtypeStruct((N_TOK, D_MODEL), F32),
        compiler_params=_params(("arbitrary",)),
        name="moe_combine",
    )(pos, yb, res, mod, top_w)


def _moe_layer(x, gain, mod, layer, w_router, w_in, w_out):
    h, top_e, top_w = _router(x, gain, mod, layer, w_router)
    e_flat = top_e[:, :TOP_K].reshape(-1)
    onehot = (e_flat[:, None] == jnp.arange(N_EXPERTS)[None, :]).astype(jnp.int32)
    csum = jnp.cumsum(onehot, axis=0)
    rank = jnp.sum((csum - onehot) * onehot, axis=1)
    counts = csum[-1]
    padded = (counts + TM_MOE - 1) // TM_MOE * TM_MOE
    pad_end = jnp.cumsum(padded)
    pad_start = pad_end - padded
    dest = (jnp.sum(onehot * pad_start[None, :], axis=1) + rank).astype(jnp.int32)
    slot_tok = jnp.zeros((N_PAD,), jnp.int32).at[dest].set(jnp.arange(N_SLOT, dtype=jnp.int32) // TOP_K)
    tile_start = jnp.arange(N_PAD // TM_MOE, dtype=jnp.int32) * TM_MOE
    n_before = jnp.sum(pad_end[None, :] <= tile_start[:, None], axis=1)
    tile_valid = (tile_start < pad_end[-1]).astype(jnp.int32)
    last_expert = jnp.sum(pad_end < pad_end[-1])
    tile_expert = jnp.where(tile_valid == 1, jnp.minimum(n_before, N_EXPERTS - 1), last_expert)
    xg = _gather_rows(h, slot_tok)
    yb = _moe_ffn(xg, tile_expert.astype(jnp.int32), tile_valid, w_in, w_out)
    return _combine(yb, dest, x, mod, layer, top_w)


def _conv_kernel(x_ref, w_ref, b_ref, o_ref):
    x = x_ref[...]
    n = x.shape[0]
    t = lax.broadcasted_iota(jnp.int32, x.shape, 0)
    acc = b_ref[...] + w_ref[SSD_CONV // 2:SSD_CONV // 2 + 1, :] * x
    for k in range(SSD_CONV):
        s = k - SSD_CONV // 2
        if s == 0:
            continue
        xs = pltpu.roll(x, (-s) % n, axis=0)
        ok = (t + s >= 0) & (t + s < n)
        acc = acc + w_ref[k:k + 1, :] * jnp.where(ok, xs, 0.0)
    o_ref[...] = _silu(acc)


def _ssd_conv(zx, conv_w, conv_b, row0, n_seq, seq_len):
    tn = 512
    c0 = SSD_INNER // tn
    r0 = row0 // seq_len
    return pl.pallas_call(
        _conv_kernel,
        grid=(n_seq, SSD_CONV_DIM // tn),
        in_specs=[
            pl.BlockSpec((seq_len, tn), lambda b, j: (r0 + b, c0 + j)),
            pl.BlockSpec((SSD_CONV, tn), lambda b, j: (0, j)),
            pl.BlockSpec((1, tn), lambda b, j: (0, j)),
        ],
        out_specs=pl.BlockSpec((seq_len, tn), lambda b, j: (b, j)),
        out_shape=jax.ShapeDtypeStruct((n_seq * seq_len, SSD_CONV_DIM), F32),
        compiler_params=_params(("arbitrary", "arbitrary")),
        name="ssd_conv",
    )(zx, conv_w, conv_b[None])


def _softplus(x):
    return jnp.maximum(x, 0.0) + jnp.log(1.0 + jnp.exp(-jnp.abs(x)))


def _ssd_scan_kernel(*refs, reverse, has_h0, out_state):
    refs = list(refs)
    xbc_ref, dt_ref, dtb_ref, alog_ref = refs[:4]
    pos = 4
    h0_ref = None
    if has_h0:
        h0_ref = refs[pos]
        pos += 1
    y_ref = refs[pos]
    pos += 1
    sf_ref = None
    if out_state:
        sf_ref = refs[pos]
        pos += 1
    s_ref = refs[pos]
    c = pl.program_id(1)
    q = SSD_CHUNK

    @pl.when(c == 0)
    def _():
        if has_h0:
            s_ref[...] = h0_ref[...]
        else:
            s_ref[...] = jnp.zeros_like(s_ref)

    dt = _softplus(dt_ref[...] + dtb_ref[...])
    da = dt * (-jnp.exp(alog_ref[...]))
    ri = lax.broadcasted_iota(jnp.int32, (q, q), 0)
    ci = lax.broadcasted_iota(jnp.int32, (q, q), 1)
    reach = (ri <= ci) if reverse else (ri >= ci)
    cs = jnp.dot(jnp.where(reach, 1.0, 0.0), da, precision=HIGHEST, preferred_element_type=F32)
    cs_t = cs.T
    cs_end = cs[0:1, :] if reverse else cs[q - 1:q, :]
    to_end = jnp.exp(cs_end - cs)
    ecs = jnp.exp(cs)
    dec = jnp.exp(cs_end)
    w_t = (dt * to_end).T
    x_t = xbc_ref[:, 0:SSD_INNER].T
    col0 = SSD_HEADS if reverse else 0
    gn = SSD_GROUPS * SSD_STATE
    for g in range(SSD_GROUPS):
        bg = xbc_ref[:, SSD_INNER + g * SSD_STATE:SSD_INNER + (g + 1) * SSD_STATE].astype(BF16)
        cg = xbc_ref[:, SSD_INNER + gn + g * SSD_STATE:SSD_INNER + gn + (g + 1) * SSD_STATE].astype(BF16)
        cb = _nt_dot(cg, bg)
        for e in range(HEADS_PER_GROUP):
            h = g * HEADS_PER_GROUP + e
            col = col0 + h
            hs = slice(h * SSD_HEADDIM, (h + 1) * SSD_HEADDIM)
            xdt = xbc_ref[:, hs] * dt[:, col:col + 1]
            seg = cs[:, col:col + 1] - cs_t[col:col + 1, :]
            decay = jnp.exp(jnp.where(reach, seg, -jnp.inf))
            y_diag = jnp.dot((cb * decay).astype(BF16), xdt.astype(BF16), preferred_element_type=F32)
            state = s_ref[hs, :]
            y_off = _nt_dot(cg, state.astype(BF16)) * ecs[:, col:col + 1]
            y_ref[:, hs] = y_diag + y_off
            xw_t = (x_t[hs, :] * w_t[col:col + 1, :]).astype(BF16)
            s_ref[hs, :] = dec[0:1, col:col + 1] * state + jnp.dot(xw_t, bg, preferred_element_type=F32)

    if out_state:
        @pl.when(c == pl.num_programs(1) - 1)
        def _():
            sf_ref[...] = s_ref[...]


def _ssd_scan(xbc, dt_all, dt_bias, a_log, row0, n_seq, seq_len, reverse, h0=None, out_state=False):
    nc = seq_len // SSD_CHUNK
    c0 = row0 // SSD_CHUNK

    def chunk(c):
        return nc - 1 - c if reverse else c

    in_specs = [
        pl.BlockSpec((SSD_CHUNK, SSD_CONV_DIM), lambda b, c: (b * nc + chunk(c), 0)),
        pl.BlockSpec((SSD_CHUNK, LANES), lambda b, c: (c0 + b * nc + chunk(c), 0)),
        pl.BlockSpec((1, LANES), lambda b, c: (0, 0)),
        pl.BlockSpec((1, LANES), lambda b, c: (0, 0)),
    ]
    args = [xbc, dt_all, dt_bias, a_log]
    if h0 is not None:
        in_specs.append(pl.BlockSpec((None, SSD_INNER, SSD_STATE), lambda b, c: (b, 0, 0)))
        args.append(h0)
    out_specs = [pl.BlockSpec((SSD_CHUNK, SSD_INNER), lambda b, c: (b * nc + chunk(c), 0))]
    out_shape = [jax.ShapeDtypeStruct((n_seq * seq_len, SSD_INNER), F32)]
    if out_state:
        out_specs.append(pl.BlockSpec((None, SSD_INNER, SSD_STATE), lambda b, c: (b, 0, 0)))
        out_shape.append(jax.ShapeDtypeStruct((n_seq, SSD_INNER, SSD_STATE), F32))
    return pl.pallas_call(
        functools.partial(_ssd_scan_kernel, reverse=reverse, has_h0=h0 is not None, out_state=out_state),
        grid=(n_seq, nc),
        in_specs=in_specs,
        out_specs=out_specs,
        out_shape=out_shape,
        scratch_shapes=[pltpu.VMEM((SSD_INNER, SSD_STATE), F32)],
        compiler_params=_params(("arbitrary", "arbitrary")),
        name="ssd_scan",
    )(*args)


def _ssd_out_kernel(yf_ref, yb_ref, x_ref, z_ref, d_ref, nw_ref, w_ref, r_ref, g_ref, o_ref):
    y = yf_ref[...] + yb_ref[...] + d_ref[...] * x_ref[...]
    y = y * _silu(z_ref[...])
    gw = SSD_INNER // SSD_GROUPS
    parts = []
    for g in range(SSD_GROUPS):
        yg = y[:, g * gw:(g + 1) * gw]
        yg = yg * lax.rsqrt(jnp.mean(yg * yg, axis=-1, keepdims=True) + NORM_EPS)
        parts.append((yg * nw_ref[:, g * gw:(g + 1) * gw]).astype(BF16))
    yn = jnp.concatenate(parts, axis=1)
    acc = jnp.dot(yn, w_ref[...].astype(BF16), preferred_element_type=F32)
    o_ref[...] = r_ref[...] + g_ref[...] * acc


def _ssd_out(yf, yb, xbc, zx, d_exp, norm_w, w_out, res, mod, layer, row0, n_rows):
    tm = TM_SSD_OUT
    r0 = row0 // tm
    return pl.pallas_call(
        _ssd_out_kernel,
        grid=(n_rows // tm,),
        in_specs=[
            pl.BlockSpec((tm, SSD_INNER), lambda i: (i, 0)),
            pl.BlockSpec((tm, SSD_INNER), lambda i: (i, 0)),
            pl.BlockSpec((tm, SSD_INNER), lambda i: (i, 0)),
            pl.BlockSpec((tm, SSD_INNER), lambda i: (r0 + i, 0)),
            pl.BlockSpec((1, SSD_INNER), lambda i: (0, 0)),
            pl.BlockSpec((1, SSD_INNER), lambda i: (0, 0)),
            pl.BlockSpec((SSD_INNER, D_MODEL), lambda i: (0, 0)),
            pl.BlockSpec((tm, D_MODEL), lambda i: (r0 + i, 0)),
            pl.BlockSpec((None, None, None, 1, D_MODEL),
                         lambda i: (layer, 2, _group_of_row(row0 + i * tm), 0, 0)),
        ],
        out_specs=pl.BlockSpec((tm, D_MODEL), lambda i: (i, 0)),
        out_shape=jax.ShapeDtypeStruct((n_rows, D_MODEL), F32),
        compiler_params=_params(("arbitrary",)),
        name="ssd_out",
    )(yf, yb, xbc, zx, d_exp, norm_w, w_out, res, mod)


def _ssd_layer(x, gain, mod, layer, state_f, state_b, w_in, conv_w, conv_b, dt_bias, a_log, d_skip,
               norm_w, w_out):
    n_zx = SSD_INNER + SSD_CONV_DIM
    zx = _nm_matmul(x, gain, mod, layer, w_in, n_zx, name="ssd_in_proj")
    w_dt = jnp.pad(w_in[:, n_zx:], ((0, 0), (0, LANES - 2 * SSD_HEADS)))
    dt_all = _nm_matmul(x, gain, mod, layer, w_dt, LANES, tn=LANES, name="ssd_dt_proj")
    pad = LANES - 2 * SSD_HEADS
    dtb = jnp.pad(dt_bias.reshape(1, -1), ((0, 0), (0, pad)))
    alog = jnp.pad(a_log.reshape(1, -1), ((0, 0), (0, pad)))
    d_exp = jnp.repeat(d_skip, SSD_HEADDIM)[None]
    outs, states = [], []
    for row0, n_seq, seq_len, h0s in ((0, BATCH, SEQ, None), (N_PROMPT, DEC_BATCH, DEC_SEQ, (state_f, state_b))):
        xbc = _ssd_conv(zx, conv_w, conv_b, row0, n_seq, seq_len)
        ys = []
        for reverse in (False, True):
            h0 = None if h0s is None else h0s[int(reverse)].reshape(n_seq, SSD_INNER, SSD_STATE)
            res = _ssd_scan(xbc, dt_all, dtb, alog, row0, n_seq, seq_len, reverse, h0=h0,
                            out_state=h0s is None)
            ys.append(res[0])
            if h0s is None:
                states.append(res[1])
        outs.append(_ssd_out(ys[0], ys[1], xbc, zx, d_exp, norm_w[None], w_out, x, mod, layer,
                             row0, n_seq * seq_len))
    shape = (BATCH, SSD_HEADS, SSD_HEADDIM, SSD_STATE)
    return jnp.concatenate(outs, axis=0), states[0].reshape(shape), states[1].reshape(shape)


def _na_layer(x, gain, mod, layer, cache_k, cache_v, w_qkv, w_o, q_norm, k_norm, rpb):
    w = NA_HEADS * HEAD_DIM
    qkv = _nm_matmul(x, gain, mod, layer, w_qkv, 3 * w, name="na_qkv")
    qp, kp, vp = _prep(qkv, q_norm, k_norm, w, 0, N_PROMPT, F32)
    qs, ks, vs = _prep(qkv, q_norm, k_norm, w, N_PROMPT, N_SAMPLE, BF16)
    op = _ctx_attn(qp, kp, vp)
    ck = cache_k.reshape(DEC_BATCH, PAST_LEN, w).astype(BF16)
    cv = cache_v.reshape(DEC_BATCH, PAST_LEN, w).astype(BF16)
    os_ = _na_latent(qs, ks, vs, ck, cv, _na_bias_table(rpb))
    o = jnp.concatenate([op, os_], axis=0)
    shape = (BATCH, SEQ, NA_HEADS, HEAD_DIM)
    return _linear_res(o, w_o, x, mod, layer, 2), kp.reshape(shape), vp.reshape(shape)


def _swa_layer(x, gain, mod, layer, cache_k, cache_v, w_qkv, w_o, q_norm, k_norm, sink):
    qw, kw = SWA_Q_HEADS * HEAD_DIM, SWA_KV_HEADS * HEAD_DIM
    qkv = _nm_matmul(x, gain, mod, layer, w_qkv, qw + 2 * kw, name="swa_qkv")
    qp, kp, vp = _prep(qkv, q_norm, k_norm, kw, 0, N_PROMPT, F32)
    qs, ks, vs = _prep(qkv, q_norm, k_norm, kw, N_PROMPT, N_SAMPLE, BF16, rope_tabs=_rope_tables())
    sink = sink.astype(F32)
    op = _ctx_attn(qp, kp, vp, sink)
    ck = cache_k.reshape(DEC_BATCH, PAST_LEN, kw).astype(BF16)
    cv = cache_v.reshape(DEC_BATCH, PAST_LEN, kw).astype(BF16)
    os_ = _swa_latent(qs, ks, vs, ck, cv, sink)
    o = jnp.concatenate([op, os_], axis=0)
    shape = (BATCH, SEQ, SWA_KV_HEADS, HEAD_DIM)
    return _linear_res(o, w_o, x, mod, layer, 2), kp.reshape(shape), vp.reshape(shape)


def kernel(x_prompt, x_sample, cache_na_k, cache_na_v, cache_swa_k, cache_swa_v, state_ssd_fwd, state_ssd_bwd, c, c_ctx, ada_w, ada_b, norm_mix, norm_ffn, na_w_qkv, na_w_o, na_q_norm, na_k_norm, na_rpb, swa_w_qkv, swa_w_o, swa_q_norm, swa_k_norm, swa_sink, ssd_w_in, ssd_conv_w, ssd_conv_b, ssd_dt_bias, ssd_a_log, ssd_d, ssd_norm, ssd_w_out, ffn_w_in, ffn_w_out, moe_router, moe_w_in, moe_w_out):
    x = jnp.concatenate([x_prompt.reshape(N_PROMPT, D_MODEL), x_sample.reshape(N_SAMPLE, D_MODEL)], axis=0)
    cond = jnp.concatenate([c_ctx[None], c, jnp.zeros((N_GROUPS - 1 - DEC_BATCH, D_MODEL), F32)], axis=0)
    mod = _adaln(cond, ada_w, ada_b)
    na_k, na_v, swa_k, swa_v, ssd_f, ssd_b = [], [], [], [], [], []
    for i in range(DEPTH):
        kind, j = i % N_MIXERS, i // N_MIXERS
        g_mix, g_ffn = norm_mix[i][None], norm_ffn[i][None]
        if kind == 0:
            x, kc, vc = _na_layer(x, g_mix, mod, i, cache_na_k[:, j], cache_na_v[:, j], na_w_qkv[j],
                                  na_w_o[j], na_q_norm[j], na_k_norm[j], na_rpb[j])
            na_k.append(kc)
            na_v.append(vc)
        elif kind == 1:
            x, kc, vc = _swa_layer(x, g_mix, mod, i, cache_swa_k[:, j], cache_swa_v[:, j], swa_w_qkv[j],
                                   swa_w_o[j], swa_q_norm[j], swa_k_norm[j], swa_sink[j])
            swa_k.append(kc)
            swa_v.append(vc)
        else:
            x, sf, sb = _ssd_layer(x, g_mix, mod, i, state_ssd_fwd[:, j], state_ssd_bwd[:, j], ssd_w_in[j],
                                   ssd_conv_w[j], ssd_conv_b[j], ssd_dt_bias[j], ssd_a_log[j], ssd_d[j],
                                   ssd_norm[j], ssd_w_out[j])
            ssd_f.append(sf)
            ssd_b.append(sb)
        if i % 2 == 0:
            x = _dense_ffn(x, g_ffn, mod, i, ffn_w_in[i // 2].astype(BF16), ffn_w_out[i // 2].astype(BF16))
        else:
            x = _moe_layer(x, g_ffn, mod, i, moe_router[i // 2], moe_w_in[i // 2], moe_w_out[i // 2])
    yp = x[:N_PROMPT].reshape(BATCH, SEQ, D_MODEL)
    ys = x[N_PROMPT:].reshape(DEC_BATCH, DEC_SEQ, D_MODEL)
    return (yp, ys, jnp.stack(na_k, axis=1), jnp.stack(na_v, axis=1), jnp.stack(swa_k, axis=1),
            jnp.stack(swa_v, axis=1), jnp.stack(ssd_f, axis=1), jnp.stack(ssd_b, axis=1))
```

```python
import functools

import jax
import jax.numpy as jnp
from jax import lax
from jax.experimental import pallas as pl
from jax.experimental.pallas import tpu as pltpu

F32 = jnp.float32
BF16 = jnp.bfloat16
HIGHEST = lax.Precision.HIGHEST

D_MODEL = 1024
BATCH = 32
SEQ = 256
DEPTH = 4
DEC_BATCH = 4
DEC_SEQ = 1024
PAST_LEN = 256
GRID_W = 64
N_MIXERS = 3
HEAD_DIM = 64
NORM_EPS = 1e-6
ROPE_BASE = 10000.0
NA_HEADS = 16
NA_ROWS = 8
NA_COLS = 16
SWA_Q_HEADS = 16
SWA_KV_HEADS = 4
SWA_GROUP = SWA_Q_HEADS // SWA_KV_HEADS
SWA_WINDOW = 128
SWA_BLOCK = 128
SSD_INNER = 2 * D_MODEL
SSD_HEADDIM = 64
SSD_HEADS = SSD_INNER // SSD_HEADDIM
SSD_GROUPS = 4
SSD_STATE = 128
SSD_CONV = 5
SSD_CHUNK = 128
SSD_CONV_DIM = SSD_INNER + 2 * SSD_GROUPS * SSD_STATE
FFN_DIM = 2816
N_EXPERTS = 8
TOP_K = 2
EXPERT_DIM = 3584

N_PROMPT = BATCH * SEQ
N_SAMPLE = DEC_BATCH * DEC_SEQ
N_TOK = N_PROMPT + N_SAMPLE
N_GROUPS = 8
LANES = 128
SUBLANES = 8
GRID_ROWS = DEC_SEQ // GRID_W
HEADS_PER_GROUP = SSD_HEADS // SSD_GROUPS
ROW_TILES = D_MODEL // LANES

VMEM_LIMIT = 56 * 1024 * 1024
TM_LIN = 1024
TN_LIN = 512
TM_FFN = 512
TF_DENSE = 1408
TF_MOE = 512
TM_MOE = 1024
SUB_MOE = 512
N_SUB = TM_MOE // SUB_MOE
N_SLOT = N_TOK * TOP_K
N_PAD = N_SLOT + N_EXPERTS * TM_MOE
N_MOE_TILES = N_PAD // TM_MOE
TG = 512
TM_ROUTER = 512
TM_SSD_OUT = 256
assert ROW_TILES == SUBLANES


def _params(sem):
    return pltpu.CompilerParams(dimension_semantics=sem, vmem_limit_bytes=VMEM_LIMIT)


def _group_of_row(start):
    return jnp.where(start < N_PROMPT, 0, 1 + (start - N_PROMPT) // DEC_SEQ)


def _silu(x):
    return x / (1.0 + jnp.exp(-x))


def _norm_mod(x, gain, shift, scale):
    y = x * lax.rsqrt(jnp.mean(x * x, axis=-1, keepdims=True) + NORM_EPS)
    return (y * gain) * (1.0 + scale) + shift


def _adaln_kernel(c_ref, w_ref, b_ref, o_ref):
    s = _silu(c_ref[...])
    o_ref[...] = jnp.dot(s, w_ref[...], precision=HIGHEST, preferred_element_type=F32) + b_ref[...]


def _adaln(cond, ada_w, ada_b):
    tn = 1024
    out = pl.pallas_call(
        _adaln_kernel,
        grid=(DEPTH, 6 * D_MODEL // tn),
        in_specs=[
            pl.BlockSpec((N_GROUPS, D_MODEL), lambda l, j: (0, 0)),
            pl.BlockSpec((None, D_MODEL, tn), lambda l, j: (l, 0, j)),
            pl.BlockSpec((None, 1, tn), lambda l, j: (l, 0, j)),
        ],
        out_specs=pl.BlockSpec((None, N_GROUPS, tn), lambda l, j: (l, 0, j)),
        out_shape=jax.ShapeDtypeStruct((DEPTH, N_GROUPS, 6 * D_MODEL), F32),
        compiler_params=_params(("arbitrary", "arbitrary")),
        name="adaln",
    )(cond, ada_w, ada_b.reshape(DEPTH, 1, 6 * D_MODEL))
    out = out.reshape(DEPTH, N_GROUPS, 6, D_MODEL)
    return jnp.transpose(out, (0, 2, 1, 3)).reshape(DEPTH, 6, N_GROUPS, 1, D_MODEL)


def _nm_matmul_kernel(x_ref, g_ref, sh_ref, sc_ref, w_ref, o_ref, h_ref):
    @pl.when(pl.program_id(1) == 0)
    def _():
        h_ref[...] = _norm_mod(x_ref[...], g_ref[...], sh_ref[...], sc_ref[...]).astype(BF16)

    o_ref[...] = jnp.dot(h_ref[...], w_ref[...].astype(BF16), preferred_element_type=F32)


def _nm_matmul(x, gain, mod, layer, w, widx, n_out, tn=TN_LIN, name="nm_matmul"):
    tm = TM_LIN
    return pl.pallas_call(
        _nm_matmul_kernel,
        grid=(N_TOK // tm, n_out // tn),
        in_specs=[
            pl.BlockSpec((tm, D_MODEL), lambda i, j: (i, 0)),
            pl.BlockSpec((1, D_MODEL), lambda i, j: (0, 0)),
            pl.BlockSpec((None, None, None, 1, D_MODEL),
                         lambda i, j: (layer, 0, _group_of_row(i * tm), 0, 0)),
            pl.BlockSpec((None, None, None, 1, D_MODEL),
                         lambda i, j: (layer, 1, _group_of_row(i * tm), 0, 0)),
            pl.BlockSpec((None, D_MODEL, tn), lambda i, j: (widx, 0, j)),
        ],
        out_specs=pl.BlockSpec((tm, tn), lambda i, j: (i, j)),
        out_shape=jax.ShapeDtypeStruct((N_TOK, n_out), F32),
        scratch_shapes=[pltpu.VMEM((tm, D_MODEL), BF16)],
        compiler_params=_params(("arbitrary", "arbitrary")),
        name=name,
    )(x, gain, mod, mod, w)


def _linear_res_kernel(x_ref, w_ref, r_ref, g_ref, o_ref):
    acc = jnp.dot(x_ref[...], w_ref[...].astype(BF16), preferred_element_type=F32)
    o_ref[...] = r_ref[...] + g_ref[...] * acc


def _linear_res(x_bf, w, widx, res, mod, layer, which):
    tm, tn = TM_LIN, TN_LIN
    k = x_bf.shape[1]
    return pl.pallas_call(
        _linear_res_kernel,
        grid=(N_TOK // tm, D_MODEL // tn),
        in_specs=[
            pl.BlockSpec((tm, k), lambda i, j: (i, 0)),
            pl.BlockSpec((None, k, tn), lambda i, j: (widx, 0, j)),
            pl.BlockSpec((tm, tn), lambda i, j: (i, j)),
            pl.BlockSpec((None, None, None, 1, tn),
                         lambda i, j: (layer, which, _group_of_row(i * tm), 0, j)),
        ],
        out_specs=pl.BlockSpec((tm, tn), lambda i, j: (i, j)),
        out_shape=jax.ShapeDtypeStruct((N_TOK, D_MODEL), F32),
        compiler_params=_params(("arbitrary", "arbitrary")),
        name="linear_res",
    )(x_bf, w, res, mod)


def _group_sumsq(x):
    r = lax.broadcasted_iota(jnp.int32, (LANES, LANES), 0) // HEAD_DIM
    c = lax.broadcasted_iota(jnp.int32, (LANES, LANES), 1) // HEAD_DIM
    ones = jnp.where(r == c, 1.0, 0.0).astype(BF16)
    outs = []
    for t in range(x.shape[1] // LANES):
        x2 = x[:, t * LANES:(t + 1) * LANES]
        x2 = x2 * x2
        hi = x2.astype(BF16)
        lo = (x2 - hi.astype(F32)).astype(BF16)
        outs.append(jnp.dot(hi, ones, preferred_element_type=F32)
                    + jnp.dot(lo, ones, preferred_element_type=F32))
    return jnp.concatenate(outs, axis=1) if len(outs) > 1 else outs[0]


def _head_norm(x, gain):
    return x * lax.rsqrt(_group_sumsq(x) * (1.0 / HEAD_DIM) + NORM_EPS) * gain


def _rope(x, cos, sin):
    w = x.shape[1]
    lane = lax.broadcasted_iota(jnp.int32, x.shape, 1)
    partner = jnp.where((lane % 32) < 16, pltpu.roll(x, w - 16, axis=1), pltpu.roll(x, 16, axis=1))
    return x * cos + partner * sin


def _qkv_kernel(*refs, nq, rope):
    if rope:
        (x_ref, g_ref, sh_ref, sc_ref, w_ref, qg_ref, kg_ref, cos_ref, sin_ref,
         q_ref, k_ref, v_ref, h_ref) = refs
    else:
        x_ref, g_ref, sh_ref, sc_ref, w_ref, qg_ref, kg_ref, q_ref, k_ref, v_ref, h_ref = refs
    j = pl.program_id(1)

    @pl.when(j == 0)
    def _():
        h_ref[...] = _norm_mod(x_ref[...], g_ref[...], sh_ref[...], sc_ref[...]).astype(BF16)

    acc = jnp.dot(h_ref[...], w_ref[...].astype(BF16), preferred_element_type=F32)

    def normed(gain_ref):
        y = _head_norm(acc, gain_ref[...])
        return _rope(y, cos_ref[...], sin_ref[...]) if rope else y

    @pl.when(j < nq)
    def _():
        q_ref[...] = (normed(qg_ref) * (HEAD_DIM ** -0.5)).astype(q_ref.dtype)

    @pl.when(j == nq)
    def _():
        k_ref[...] = normed(kg_ref).astype(k_ref.dtype)

    @pl.when(j == nq + 1)
    def _():
        v_ref[...] = acc.astype(v_ref.dtype)


def _qkv_proj(x, gain, mod, layer, w, widx, q_gain, k_gain, kw, row0, n_rows, kv_dtype, rope_tabs=None):
    tm = TM_LIN
    qw = NA_HEADS * HEAD_DIM
    nq = qw // kw
    r0 = row0 // tm
    in_specs = [
        pl.BlockSpec((tm, D_MODEL), lambda i, j: (r0 + i, 0)),
        pl.BlockSpec((1, D_MODEL), lambda i, j: (0, 0)),
        pl.BlockSpec((None, None, None, 1, D_MODEL),
                     lambda i, j: (layer, 0, _group_of_row(row0 + i * tm), 0, 0)),
        pl.BlockSpec((None, None, None, 1, D_MODEL),
                     lambda i, j: (layer, 1, _group_of_row(row0 + i * tm), 0, 0)),
        pl.BlockSpec((None, D_MODEL, kw), lambda i, j: (widx, 0, j)),
        pl.BlockSpec((1, kw), lambda i, j: (0, 0)),
        pl.BlockSpec((1, kw), lambda i, j: (0, 0)),
    ]
    args = [x, gain, mod, mod, w, jnp.tile(q_gain, kw // HEAD_DIM)[None], jnp.tile(k_gain, kw // HEAD_DIM)[None]]
    if rope_tabs is not None:
        in_specs += [pl.BlockSpec((tm, kw), lambda i, j: (0, 0))] * 2
        args += [t[:, :kw] for t in rope_tabs]
    return pl.pallas_call(
        functools.partial(_qkv_kernel, nq=nq, rope=rope_tabs is not None),
        grid=(n_rows // tm, nq + 2),
        in_specs=in_specs,
        out_specs=[
            pl.BlockSpec((tm, kw), lambda i, j: (i, jnp.minimum(j, nq - 1))),
            pl.BlockSpec((tm, kw), lambda i, j: (i, 0)),
            pl.BlockSpec((tm, kw), lambda i, j: (i, 0)),
        ],
        out_shape=[
            jax.ShapeDtypeStruct((n_rows, qw), BF16),
            jax.ShapeDtypeStruct((n_rows, kw), kv_dtype),
            jax.ShapeDtypeStruct((n_rows, kw), kv_dtype),
        ],
        scratch_shapes=[pltpu.VMEM((tm, D_MODEL), BF16)],
        compiler_params=_params(("arbitrary", "arbitrary")),
        name="qkv_proj",
    )(*args)


def _rope_tables():
    quarter = HEAD_DIM // 4
    t = jnp.arange(DEC_SEQ)
    pos = jnp.stack([t // GRID_W, t % GRID_W], axis=-1).astype(F32)
    inv = ROPE_BASE ** (-jnp.arange(quarter, dtype=F32) / quarter)
    ang = pos[:, :, None] * inv
    cos, sin = jnp.cos(ang), jnp.sin(ang)
    cos64 = jnp.concatenate([cos[:, 0], cos[:, 0], cos[:, 1], cos[:, 1]], axis=1)
    sin64 = jnp.concatenate([-sin[:, 0], sin[:, 0], -sin[:, 1], sin[:, 1]], axis=1)
    return jnp.tile(cos64, (1, SWA_KV_HEADS)), jnp.tile(sin64, (1, SWA_KV_HEADS))


def _nt_dot(a, b):
    return lax.dot_general(a, b, (((1,), (1,)), ((), ())), preferred_element_type=F32)


def _ctx_attn_kernel(*refs, group, use_sink):
    if use_sink:
        q_ref, k_ref, v_ref, sink_ref, o_ref = refs
    else:
        q_ref, k_ref, v_ref, o_ref = refs
    n_q = q_ref.shape[1] // HEAD_DIM
    outs = []
    for h in range(n_q):
        kh = h // group
        q = q_ref[:, h * HEAD_DIM:(h + 1) * HEAD_DIM]
        k = k_ref[:, kh * HEAD_DIM:(kh + 1) * HEAD_DIM].astype(BF16)
        v = v_ref[:, kh * HEAD_DIM:(kh + 1) * HEAD_DIM].astype(BF16)
        s = _nt_dot(q, k)
        m = jnp.max(s, axis=-1, keepdims=True)
        if use_sink:
            m = jnp.maximum(m, sink_ref[h])
        e = jnp.exp(s - m)
        l = jnp.sum(e, axis=-1, keepdims=True)
        if use_sink:
            l = l + jnp.exp(sink_ref[h] - m)
        o = jnp.dot(e.astype(BF16), v, preferred_element_type=F32)
        outs.append(o / l)
    o_ref[...] = jnp.concatenate(outs, axis=1).astype(o_ref.dtype)


def _ctx_attn(q, k, v, sink=None):
    kw = k.shape[1]
    group = q.shape[1] // kw
    in_specs = [
        pl.BlockSpec((SEQ, q.shape[1]), lambda b: (b, 0)),
        pl.BlockSpec((SEQ, kw), lambda b: (b, 0)),
        pl.BlockSpec((SEQ, kw), lambda b: (b, 0)),
    ]
    args = [q, k, v]
    if sink is not None:
        in_specs.append(pl.BlockSpec(memory_space=pltpu.SMEM))
        args.append(sink)
    return pl.pallas_call(
        functools.partial(_ctx_attn_kernel, group=group, use_sink=sink is not None),
        grid=(BATCH,),
        in_specs=in_specs,
        out_specs=pl.BlockSpec((SEQ, q.shape[1]), lambda b: (b, 0)),
        out_shape=jax.ShapeDtypeStruct((N_PROMPT, q.shape[1]), BF16),
        compiler_params=_params(("arbitrary",)),
        name="ctx_attn",
    )(*args)


def _na_latent_kernel(q_ref, k_ref, v_ref, ck_ref, cv_ref, bias_ref, o_ref):
    r = pl.program_id(1)
    kr = NA_ROWS
    start = jnp.clip(r - kr // 2, 0, GRID_ROWS - kr)
    row0 = pl.multiple_of(start * GRID_W, GRID_W)
    outs = []
    for h in range(NA_HEADS):
        sl = slice(h * HEAD_DIM, (h + 1) * HEAD_DIM)
        q = q_ref[:, sl]
        k = k_ref[pl.ds(row0, kr * GRID_W), sl]
        v = v_ref[pl.ds(row0, kr * GRID_W), sl]
        s_nb = _nt_dot(q, k) + bias_ref[h]
        s_cx = _nt_dot(q, ck_ref[:, sl])
        m = jnp.maximum(jnp.max(s_nb, axis=-1, keepdims=True), jnp.max(s_cx, axis=-1, keepdims=True))
        e_nb = jnp.exp(s_nb - m)
        e_cx = jnp.exp(s_cx - m)
        l = jnp.sum(e_nb, axis=-1, keepdims=True) + jnp.sum(e_cx, axis=-1, keepdims=True)
        o = (jnp.dot(e_nb.astype(BF16), v, preferred_element_type=F32)
             + jnp.dot(e_cx.astype(BF16), cv_ref[:, sl], preferred_element_type=F32))
        outs.append(o / l)
    o_ref[...] = jnp.concatenate(outs, axis=1).astype(o_ref.dtype)


def _na_bias_table(rpb):
    col = jnp.arange(GRID_W)
    col_start = jnp.clip(col - NA_COLS // 2, 0, GRID_W - NA_COLS)
    col_ok = (col[None, :] >= col_start[:, None]) & (col[None, :] < col_start[:, None] + NA_COLS)
    dc = jnp.clip(col[None, :] - col[:, None], 1 - NA_COLS, NA_COLS - 1) + NA_COLS - 1
    pick = (dc[None] == jnp.arange(2 * NA_COLS - 1)[:, None, None]).astype(F32)
    t = jnp.einsum('hrc,cqk->hrqk', rpb.astype(F32), pick, precision=HIGHEST)
    t = jnp.where(col_ok[None, None], t, -jnp.inf)
    tb = jnp.stack([t[:, d0:d0 + NA_ROWS] for d0 in range(NA_ROWS)], axis=1)
    tb = jnp.transpose(tb, (0, 1, 3, 2, 4))
    return tb.reshape(NA_HEADS, NA_ROWS, GRID_W, NA_ROWS * GRID_W)


def _na_latent(q, k, v, ck, cv, bias):
    w = NA_HEADS * HEAD_DIM

    def bias_map(b, r):
        start = jnp.clip(r - NA_ROWS // 2, 0, GRID_ROWS - NA_ROWS)
        return (0, start - r + NA_ROWS - 1, 0, 0)

    return pl.pallas_call(
        _na_latent_kernel,
        grid=(DEC_BATCH, GRID_ROWS),
        in_specs=[
            pl.BlockSpec((GRID_W, w), lambda b, r: (b * GRID_ROWS + r, 0)),
            pl.BlockSpec((DEC_SEQ, w), lambda b, r: (b, 0)),
            pl.BlockSpec((DEC_SEQ, w), lambda b, r: (b, 0)),
            pl.BlockSpec((None, PAST_LEN, w), lambda b, r: (b, 0, 0)),
            pl.BlockSpec((None, PAST_LEN, w), lambda b, r: (b, 0, 0)),
            pl.BlockSpec((NA_HEADS, None, GRID_W, NA_ROWS * GRID_W), bias_map),
        ],
        out_specs=pl.BlockSpec((GRID_W, w), lambda b, r: (b * GRID_ROWS + r, 0)),
        out_shape=jax.ShapeDtypeStruct((N_SAMPLE, w), BF16),
        compiler_params=_params(("arbitrary", "arbitrary")),
        name="na_latent",
    )(q, k, v, ck, cv, bias)


SWA_SPAN = SWA_BLOCK + 2 * SWA_WINDOW


def _swa_latent_kernel(q_ref, k_ref, v_ref, ck_ref, cv_ref, sink_ref, o_ref):
    n = pl.program_id(1)
    k0 = pl.multiple_of(jnp.clip(n - 1, 0, DEC_SEQ // SWA_BLOCK - SWA_SPAN // SWA_BLOCK) * SWA_BLOCK,
                        SWA_BLOCK)
    qpos = n * SWA_BLOCK + lax.broadcasted_iota(jnp.int32, (SWA_BLOCK, SWA_SPAN), 0)
    kpos = k0 + lax.broadcasted_iota(jnp.int32, (SWA_BLOCK, SWA_SPAN), 1)
    ok = jnp.abs(qpos - kpos) <= SWA_WINDOW
    outs = []
    for kh in range(SWA_KV_HEADS):
        ksl = slice(kh * HEAD_DIM, (kh + 1) * HEAD_DIM)
        k = k_ref[pl.ds(k0, SWA_SPAN), ksl]
        v = v_ref[pl.ds(k0, SWA_SPAN), ksl]
        ck = ck_ref[:, ksl]
        cv = cv_ref[:, ksl]
        for g in range(SWA_GROUP):
            h = kh * SWA_GROUP + g
            q = q_ref[:, h * HEAD_DIM:(h + 1) * HEAD_DIM]
            sink = sink_ref[h]
            s_b = jnp.where(ok, _nt_dot(q, k), -jnp.inf)
            s_c = _nt_dot(q, ck)
            m = jnp.maximum(jnp.maximum(jnp.max(s_b, axis=-1, keepdims=True),
                                        jnp.max(s_c, axis=-1, keepdims=True)), sink)
            e_b = jnp.exp(s_b - m)
            e_c = jnp.exp(s_c - m)
            l = (jnp.sum(e_b, axis=-1, keepdims=True) + jnp.sum(e_c, axis=-1, keepdims=True)
                 + jnp.exp(sink - m))
            o = (jnp.dot(e_b.astype(BF16), v, preferred_element_type=F32)
                 + jnp.dot(e_c.astype(BF16), cv, preferred_element_type=F32))
            outs.append(o / l)
    o_ref[...] = jnp.concatenate(outs, axis=1).astype(o_ref.dtype)


def _swa_latent(q, k, v, ck, cv, sink):
    qw = SWA_Q_HEADS * HEAD_DIM
    kw = SWA_KV_HEADS * HEAD_DIM
    nb = DEC_SEQ // SWA_BLOCK
    return pl.pallas_call(
        _swa_latent_kernel,
        grid=(DEC_BATCH, nb),
        in_specs=[
            pl.BlockSpec((SWA_BLOCK, qw), lambda b, n: (b * nb + n, 0)),
            pl.BlockSpec((DEC_SEQ, kw), lambda b, n: (b, 0)),
            pl.BlockSpec((DEC_SEQ, kw), lambda b, n: (b, 0)),
            pl.BlockSpec((None, PAST_LEN, kw), lambda b, n: (b, 0, 0)),
            pl.BlockSpec((None, PAST_LEN, kw), lambda b, n: (b, 0, 0)),
            pl.BlockSpec(memory_space=pltpu.SMEM),
        ],
        out_specs=pl.BlockSpec((SWA_BLOCK, qw), lambda b, n: (b * nb + n, 0)),
        out_shape=jax.ShapeDtypeStruct((N_SAMPLE, qw), BF16),
        compiler_params=_params(("arbitrary", "arbitrary")),
        name="swa_latent",
    )(q, k, v, ck, cv, sink)


def _swiglu_part(h, wg, wu, wo):
    g = jnp.dot(h, wg, preferred_element_type=F32)
    u = jnp.dot(h, wu, preferred_element_type=F32)
    a = (_silu(g) * u).astype(BF16)
    return jnp.dot(a, wo, preferred_element_type=F32)


def _dense_ffn_kernel(x_ref, g_ref, sh_ref, sc_ref, wg_ref, wu_ref, wo_ref, gate_ref, o_ref,
                      h_ref, acc_ref):
    j = pl.program_id(1)

    @pl.when(j == 0)
    def _():
        h_ref[...] = _norm_mod(x_ref[...], g_ref[...], sh_ref[...], sc_ref[...]).astype(BF16)

    part = _swiglu_part(h_ref[...], wg_ref[...], wu_ref[...], wo_ref[...])

    @pl.when(j == 0)
    def _():
        acc_ref[...] = part

    @pl.when(j > 0)
    def _():
        acc_ref[...] += part

    @pl.when(j == pl.num_programs(1) - 1)
    def _():
        o_ref[...] = x_ref[...] + gate_ref[...] * acc_ref[...]


def _dense_ffn(x, gain, mod, layer, w_in, w_out, widx):
    tm, tf = TM_FFN, TF_DENSE
    nf = FFN_DIM // tf

    def mspec(which):
        return pl.BlockSpec((None, None, None, 1, D_MODEL),
                            lambda i, j: (layer, which, _group_of_row(i * tm), 0, 0))

    return pl.pallas_call(
        _dense_ffn_kernel,
        grid=(N_TOK // tm, nf),
        in_specs=[
            pl.BlockSpec((tm, D_MODEL), lambda i, j: (i, 0)),
            pl.BlockSpec((1, D_MODEL), lambda i, j: (0, 0)),
            mspec(3), mspec(4),
            pl.BlockSpec((None, D_MODEL, tf), lambda i, j: (widx, 0, j)),
            pl.BlockSpec((None, D_MODEL, tf), lambda i, j: (widx, 0, nf + j)),
            pl.BlockSpec((None, tf, D_MODEL), lambda i, j: (widx, j, 0)),
            mspec(5),
        ],
        out_specs=pl.BlockSpec((tm, D_MODEL), lambda i, j: (i, 0)),
        out_shape=jax.ShapeDtypeStruct((N_TOK, D_MODEL), F32),
        scratch_shapes=[pltpu.VMEM((tm, D_MODEL), BF16), pltpu.VMEM((tm, D_MODEL), F32)],
        compiler_params=_params(("arbitrary", "arbitrary")),
        name="dense_ffn",
    )(x, gain, mod, mod, w_in, w_in, w_out, mod)


def _tile_rows(ref, sub, c, n):
    return ref.at[pl.ds(sub * n * ROW_TILES + c, n, stride=ROW_TILES), :]


def _moe_ffn_kernel(te_ref, ns_ref, x_ref, wg_ref, wu_ref, wo_ref, o_ref,
                    h_ref, acc_ref, wgb_ref, wub_ref, wob_ref):
    i, j = pl.program_id(0), pl.program_id(1)
    nsub = ns_ref[i]
    last = j == pl.num_programs(1) - 1

    @pl.when(nsub > 0)
    def _():
        wgb_ref[...] = wg_ref[...].astype(BF16)
        wub_ref[...] = wu_ref[...].astype(BF16)
        wob_ref[...] = wo_ref[...].astype(BF16)

    for s in range(N_SUB):
        rows = pl.ds(s * SUB_MOE, SUB_MOE)

        @pl.when(s < nsub)
        def _():
            @pl.when(j == 0)
            def _():
                for c in range(ROW_TILES):
                    h_ref[rows, c * LANES:(c + 1) * LANES] = _tile_rows(x_ref, s, c, SUB_MOE)[...].astype(BF16)

            part = _swiglu_part(h_ref[rows, :], wgb_ref[...], wub_ref[...], wob_ref[...])

            @pl.when(j == 0)
            def _():
                acc_ref[rows, :] = part

            @pl.when(j > 0)
            def _():
                acc_ref[rows, :] += part

            @pl.when(last)
            def _():
                for c in range(ROW_TILES):
                    _tile_rows(o_ref, s, c, SUB_MOE)[...] = acc_ref[rows, c * LANES:(c + 1) * LANES]

        @pl.when((s >= nsub) & last)
        def _():
            o_ref[pl.ds(s * SUB_MOE * ROW_TILES, SUB_MOE * ROW_TILES), :] = jnp.zeros(
                (SUB_MOE * ROW_TILES, LANES), F32)


def _moe_ffn(xg, tile_expert, tile_nsub, w_in, w_out, widx):
    tm, tf = TM_MOE, TF_MOE
    nf = EXPERT_DIM // tf

    def jj(i, j, ns):
        return jnp.where(ns[i] > 0, j, nf - 1)

    grid_spec = pltpu.PrefetchScalarGridSpec(
        num_scalar_prefetch=2,
        grid=(N_MOE_TILES, nf),
        in_specs=[
            pl.BlockSpec((tm * ROW_TILES, LANES), lambda i, j, te, ns: (i, 0)),
            pl.BlockSpec((None, None, D_MODEL, tf), lambda i, j, te, ns: (widx, te[i], 0, jj(i, j, ns))),
            pl.BlockSpec((None, None, D_MODEL, tf), lambda i, j, te, ns: (widx, te[i], 0, nf + jj(i, j, ns))),
            pl.BlockSpec((None, None, tf, D_MODEL), lambda i, j, te, ns: (widx, te[i], jj(i, j, ns), 0)),
        ],
        out_specs=pl.BlockSpec((tm * ROW_TILES, LANES), lambda i, j, te, ns: (i, 0)),
        scratch_shapes=[pltpu.VMEM((tm, D_MODEL), BF16), pltpu.VMEM((tm, D_MODEL), F32),
                        pltpu.VMEM((D_MODEL, tf), BF16), pltpu.VMEM((D_MODEL, tf), BF16),
                        pltpu.VMEM((tf, D_MODEL), BF16)],
    )
    return pl.pallas_call(
        _moe_ffn_kernel,
        grid_spec=grid_spec,
        out_shape=jax.ShapeDtypeStruct((N_PAD * ROW_TILES, LANES), F32),
        compiler_params=_params(("arbitrary", "arbitrary")),
        name="moe_ffn",
    )(tile_expert, tile_nsub, xg, w_in, w_in, w_out)


def _router_kernel(x_ref, g_ref, sh_ref, sc_ref, wr_ref, h_ref, info_ref, w_ref, cnt_ref, base_ref):
    tm = x_ref.shape[0]

    @pl.when(pl.program_id(0) == 0)
    def _():
        base_ref[...] = jnp.zeros_like(base_ref)

    h = _norm_mod(x_ref[...], g_ref[...], sh_ref[...], sc_ref[...])
    for c in range(ROW_TILES):
        h_ref[pl.ds(c, tm, stride=ROW_TILES), :] = h[:, c * LANES:(c + 1) * LANES]
    logits = jnp.dot(h, wr_ref[...], precision=HIGHEST, preferred_element_type=F32)
    lane = lax.broadcasted_iota(jnp.int32, logits.shape, 1)
    logits = jnp.where(lane < N_EXPERTS, logits, -jnp.inf)
    m1 = jnp.max(logits, axis=-1, keepdims=True)
    i1 = jnp.min(jnp.where(logits == m1, lane, LANES), axis=-1, keepdims=True)
    rest = jnp.where(lane == i1, -jnp.inf, logits)
    m2 = jnp.max(rest, axis=-1, keepdims=True)
    i2 = jnp.min(jnp.where(rest == m2, lane, LANES), axis=-1, keepdims=True)
    e2 = jnp.exp(m2 - m1)
    den = 1.0 + e2
    w_ref[...] = jnp.where(lane == 0, 1.0 / den, jnp.where(lane == 1, e2 / den, 0.0))
    chosen = jnp.where((lane == i1) | (lane == i2), 1.0, 0.0)
    rt = lax.broadcasted_iota(jnp.int32, (tm, tm), 0)
    ct = lax.broadcasted_iota(jnp.int32, (tm, tm), 1)
    earlier = jnp.where(ct < rt, 1.0, 0.0).astype(BF16)
    before = jnp.dot(earlier, chosen.astype(BF16), preferred_element_type=F32) + base_ref[0:1, :]
    r1 = jnp.sum(jnp.where(lane == i1, before, 0.0), axis=-1, keepdims=True).astype(jnp.int32)
    r2 = jnp.sum(jnp.where(lane == i2, before, 0.0), axis=-1, keepdims=True).astype(jnp.int32)
    info_ref[...] = jnp.where(lane == 0, i1, jnp.where(lane == 1, i2, jnp.where(
        lane == 2, r1, jnp.where(lane == 3, r2, 0))))
    total = base_ref[0:1, :] + jnp.sum(chosen, axis=0, keepdims=True)
    base_ref[...] = jnp.broadcast_to(total, base_ref.shape)
    cnt_ref[...] = jnp.broadcast_to(total, cnt_ref.shape).astype(jnp.int32)


def _router(x, gain, mod, layer, w_router):
    tm = TM_ROUTER
    wr = jnp.pad(w_router, ((0, 0), (0, LANES - N_EXPERTS)))

    def mspec(which):
        return pl.BlockSpec((None, None, None, 1, D_MODEL),
                            lambda i: (layer, which, _group_of_row(i * tm), 0, 0))

    return pl.pallas_call(
        _router_kernel,
        grid=(N_TOK // tm,),
        in_specs=[
            pl.BlockSpec((tm, D_MODEL), lambda i: (i, 0)),
            pl.BlockSpec((1, D_MODEL), lambda i: (0, 0)),
            mspec(3), mspec(4),
            pl.BlockSpec((D_MODEL, LANES), lambda i: (0, 0)),
        ],
        out_specs=[
            pl.BlockSpec((tm * ROW_TILES, LANES), lambda i: (i, 0)),
            pl.BlockSpec((tm, LANES), lambda i: (i, 0)),
            pl.BlockSpec((tm, LANES), lambda i: (i, 0)),
            pl.BlockSpec((SUBLANES, LANES), lambda i: (0, 0)),
        ],
        out_shape=[
            jax.ShapeDtypeStruct((N_TOK * ROW_TILES, LANES), F32),
            jax.ShapeDtypeStruct((N_TOK, LANES), jnp.int32),
            jax.ShapeDtypeStruct((N_TOK, LANES), F32),
            jax.ShapeDtypeStruct((SUBLANES, LANES), jnp.int32),
        ],
        scratch_shapes=[pltpu.VMEM((SUBLANES, LANES), F32)],
        compiler_params=_params(("arbitrary",)),
        name="router",
    )(x, gain, mod, mod, wr)


def _row_copy(src_hbm, row, dst_ref, r, sem):
    src = src_hbm.at[pl.ds(pl.multiple_of(row * ROW_TILES, ROW_TILES), ROW_TILES), :]
    dst = dst_ref.at[pl.ds(pl.multiple_of(r * ROW_TILES, ROW_TILES), ROW_TILES), :]
    return pltpu.make_async_copy(src, dst, sem)


def _gather_kernel(idx_ref, ns_ref, x_hbm, o_ref, sem):
    b = pl.program_id(0)
    base = b * TG
    valid = (b % N_SUB) < ns_ref[b // N_SUB]

    @pl.when(valid)
    def _():
        def issue(r, carry):
            _row_copy(x_hbm, idx_ref[base + r], o_ref, r, sem).start()
            return carry

        lax.fori_loop(0, TG, issue, 0, unroll=8)

        def drain(r, carry):
            _row_copy(x_hbm, 0, o_ref, r, sem).wait()
            return carry

        lax.fori_loop(0, TG, drain, 0, unroll=8)

    @pl.when(jnp.logical_not(valid))
    def _():
        o_ref[...] = jnp.zeros_like(o_ref)


def _gather_rows(x_tiles, idx, tile_nsub):
    grid_spec = pltpu.PrefetchScalarGridSpec(
        num_scalar_prefetch=2,
        grid=(N_PAD // TG,),
        in_specs=[pl.BlockSpec(memory_space=pl.ANY)],
        out_specs=pl.BlockSpec((TG * ROW_TILES, LANES), lambda i, idx, ns: (i, 0)),
        scratch_shapes=[pltpu.SemaphoreType.DMA(())],
    )
    return pl.pallas_call(
        _gather_kernel,
        grid_spec=grid_spec,
        out_shape=jax.ShapeDtypeStruct((N_PAD * ROW_TILES, LANES), F32),
        compiler_params=_params(("arbitrary",)),
        name="gather_rows",
    )(idx, tile_nsub, x_tiles)


def _combine_kernel(pos_ref, y_hbm, res_ref, gate_ref, w_ref, o_ref, a_ref, b_ref, sem):
    base = pl.program_id(0) * TG

    def issue(r, carry):
        _row_copy(y_hbm, pos_ref[2 * (base + r)], a_ref, r, sem.at[0]).start()
        _row_copy(y_hbm, pos_ref[2 * (base + r) + 1], b_ref, r, sem.at[1]).start()
        return carry

    lax.fori_loop(0, TG, issue, 0, unroll=8)

    def drain(r, carry):
        _row_copy(y_hbm, 0, a_ref, r, sem.at[0]).wait()
        _row_copy(y_hbm, 0, b_ref, r, sem.at[1]).wait()
        return carry

    lax.fori_loop(0, TG, drain, 0, unroll=8)
    w = w_ref[...]
    w0, w1 = w[:, 0:1], w[:, 1:2]
    for c in range(ROW_TILES):
        cols = slice(c * LANES, (c + 1) * LANES)
        rows = pl.ds(c, TG, stride=ROW_TILES)
        mix = w0 * a_ref[rows, :] + w1 * b_ref[rows, :]
        o_ref[:, cols] = res_ref[:, cols] + gate_ref[:, cols] * mix


def _combine(yb, pos, res, mod, layer, top_w):
    grid_spec = pltpu.PrefetchScalarGridSpec(
        num_scalar_prefetch=1,
        grid=(N_TOK // TG,),
        in_specs=[
            pl.BlockSpec(memory_space=pl.ANY),
            pl.BlockSpec((TG, D_MODEL), lambda i, p: (i, 0)),
            pl.BlockSpec((None, None, None, 1, D_MODEL),
                         lambda i, p: (layer, 5, _group_of_row(i * TG), 0, 0)),
            pl.BlockSpec((TG, LANES), lambda i, p: (i, 0)),
        ],
        out_specs=pl.BlockSpec((TG, D_MODEL), lambda i, p: (i, 0)),
        scratch_shapes=[pltpu.VMEM((TG * ROW_TILES, LANES), F32), pltpu.VMEM((TG * ROW_TILES, LANES), F32),
                        pltpu.SemaphoreType.DMA((2,))],
    )
    return pl.pallas_call(
        _combine_kernel,
        grid_spec=grid_spec,
        out_shape=jax.ShapeDtypeStruct((N_TOK, D_MODEL), F32),
        compiler_params=_params(("arbitrary",)),
        name="moe_combine",
    )(pos, yb, res, mod, top_w)


def _moe_layer(x, gain, mod, layer, w_router, w_in, w_out, widx):
    h, info, top_w, cnt = _router(x, gain, mod, layer, w_router)
    experts = jnp.arange(N_EXPERTS, dtype=jnp.int32)
    counts = cnt[0, :N_EXPERTS]
    padded = (counts + TM_MOE - 1) // TM_MOE * TM_MOE
    pad_end = jnp.cumsum(padded)
    pad_start = pad_end - padded
    e_sel = info[:, :TOP_K]
    start_sel = jnp.sum(jnp.where(e_sel[..., None] == experts, pad_start, 0), axis=-1)
    dest = (start_sel + info[:, TOP_K:2 * TOP_K]).reshape(-1).astype(jnp.int32)
    slot_tok = jnp.zeros((N_PAD,), jnp.int32).at[dest].set(jnp.arange(N_SLOT, dtype=jnp.int32) // TOP_K)
    tile_start = jnp.arange(N_MOE_TILES, dtype=jnp.int32) * TM_MOE
    n_before = jnp.sum(pad_end[None, :] <= tile_start[:, None], axis=1)
    used = tile_start < pad_end[-1]
    last_expert = jnp.sum(pad_end < pad_end[-1])
    tile_expert = jnp.where(used, jnp.minimum(n_before, N_EXPERTS - 1), last_expert).astype(jnp.int32)
    seg_end = jnp.sum(jnp.where(tile_expert[:, None] == experts, pad_start + counts, 0), axis=-1)
    rows_used = jnp.clip(seg_end - tile_start, 0, TM_MOE)
    tile_nsub = jnp.where(used, (rows_used + SUB_MOE - 1) // SUB_MOE, 0).astype(jnp.int32)
    xg = _gather_rows(h, slot_tok, tile_nsub)
    yb = _moe_ffn(xg, tile_expert, tile_nsub, w_in, w_out, widx)
    return _combine(yb, dest, x, mod, layer, top_w)


def _conv_kernel(x_ref, w_ref, b_ref, o_ref):
    x = x_ref[...]
    n = x.shape[0]
    t = lax.broadcasted_iota(jnp.int32, x.shape, 0)
    acc = b_ref[...] + w_ref[SSD_CONV // 2:SSD_CONV // 2 + 1, :] * x
    for k in range(SSD_CONV):
        s = k - SSD_CONV // 2
        if s == 0:
            continue
        xs = pltpu.roll(x, (-s) % n, axis=0)
        ok = (t + s >= 0) & (t + s < n)
        acc = acc + w_ref[k:k + 1, :] * jnp.where(ok, xs, 0.0)
    o_ref[...] = _silu(acc)


def _ssd_conv(zx, conv_w, conv_b, row0, n_seq, seq_len):
    tn = 512
    c0 = SSD_INNER // tn
    r0 = row0 // seq_len
    return pl.pallas_call(
        _conv_kernel,
        grid=(n_seq, SSD_CONV_DIM // tn),
        in_specs=[
            pl.BlockSpec((seq_len, tn), lambda b, j: (r0 + b, c0 + j)),
            pl.BlockSpec((SSD_CONV, tn), lambda b, j: (0, j)),
            pl.BlockSpec((1, tn), lambda b, j: (0, j)),
        ],
        out_specs=pl.BlockSpec((seq_len, tn), lambda b, j: (b, j)),
        out_shape=jax.ShapeDtypeStruct((n_seq * seq_len, SSD_CONV_DIM), F32),
        compiler_params=_params(("arbitrary", "arbitrary")),
        name="ssd_conv",
    )(zx, conv_w, conv_b[None])


def _softplus(x):
    return jnp.maximum(x, 0.0) + jnp.log(1.0 + jnp.exp(-jnp.abs(x)))


def _ssd_scan_kernel(*refs, reverse, has_h0, out_state):
    refs = list(refs)
    xbc_ref, dt_ref, dtb_ref, alog_ref = refs[:4]
    pos = 4
    h0_ref = None
    if has_h0:
        h0_ref = refs[pos]
        pos += 1
    y_ref = refs[pos]
    pos += 1
    sf_ref = None
    if out_state:
        sf_ref = refs[pos]
        pos += 1
    s_ref = refs[pos]
    c = pl.program_id(1)
    q = SSD_CHUNK

    @pl.when(c == 0)
    def _():
        if has_h0:
            s_ref[...] = h0_ref[...]
        else:
            s_ref[...] = jnp.zeros_like(s_ref)

    dt = _softplus(dt_ref[...] + dtb_ref[...])
    da = dt * (-jnp.exp(alog_ref[...]))
    ri = lax.broadcasted_iota(jnp.int32, (q, q), 0)
    ci = lax.broadcasted_iota(jnp.int32, (q, q), 1)
    reach = (ri <= ci) if reverse else (ri >= ci)
    cs = jnp.dot(jnp.where(reach, 1.0, 0.0), da, precision=HIGHEST, preferred_element_type=F32)
    cs_t = cs.T
    cs_end = cs[0:1, :] if reverse else cs[q - 1:q, :]
    to_end = jnp.exp(cs_end - cs)
    ecs = jnp.exp(cs)
    dec = jnp.exp(cs_end)
    w_t = (dt * to_end).T
    x_t = xbc_ref[:, 0:SSD_INNER].T
    col0 = SSD_HEADS if reverse else 0
    gn = SSD_GROUPS * SSD_STATE
    ys, states = [], []
    for g in range(SSD_GROUPS):
        bg = xbc_ref[:, SSD_INNER + g * SSD_STATE:SSD_INNER + (g + 1) * SSD_STATE].astype(BF16)
        cg = xbc_ref[:, SSD_INNER + gn + g * SSD_STATE:SSD_INNER + gn + (g + 1) * SSD_STATE].astype(BF16)
        cb = _nt_dot(cg, bg)
        for e in range(HEADS_PER_GROUP):
            h = g * HEADS_PER_GROUP + e
            col = col0 + h
            hs = slice(h * SSD_HEADDIM, (h + 1) * SSD_HEADDIM)
            xdt = xbc_ref[:, hs] * dt[:, col:col + 1]
            seg = cs[:, col:col + 1] - cs_t[col:col + 1, :]
            decay = jnp.exp(jnp.where(reach, seg, -jnp.inf))
            y_diag = jnp.dot((cb * decay).astype(BF16), xdt.astype(BF16), preferred_element_type=F32)
            state = s_ref[hs, :]
            y_off = _nt_dot(cg, state.astype(BF16)) * ecs[:, col:col + 1]
            ys.append(y_diag + y_off)
            xw_t = (x_t[hs, :] * w_t[col:col + 1, :]).astype(BF16)
            states.append(dec[0:1, col:col + 1] * state + jnp.dot(xw_t, bg, preferred_element_type=F32))
    y_ref[...] = jnp.concatenate(ys, axis=1)
    new_state = jnp.concatenate(states, axis=0)
    s_ref[...] = new_state

    if out_state:
        @pl.when(c == pl.num_programs(1) - 1)
        def _():
            sf_ref[...] = new_state


def _ssd_scan(xbc, dt_all, dt_bias, a_log, row0, n_seq, seq_len, reverse, h0=None, out_state=False):
    nc = seq_len // SSD_CHUNK
    c0 = row0 // SSD_CHUNK

    def chunk(c):
        return nc - 1 - c if reverse else c

    in_specs = [
        pl.BlockSpec((SSD_CHUNK, SSD_CONV_DIM), lambda b, c: (b * nc + chunk(c), 0)),
        pl.BlockSpec((SSD_CHUNK, LANES), lambda b, c: (c0 + b * nc + chunk(c), 0)),
        pl.BlockSpec((1, LANES), lambda b, c: (0, 0)),
        pl.BlockSpec((1, LANES), lambda b, c: (0, 0)),
    ]
    args = [xbc, dt_all, dt_bias, a_log]
    if h0 is not None:
        in_specs.append(pl.BlockSpec((None, SSD_INNER, SSD_STATE), lambda b, c: (b, 0, 0)))
        args.append(h0)
    out_specs = [pl.BlockSpec((SSD_CHUNK, SSD_INNER), lambda b, c: (b * nc + chunk(c), 0))]
    out_shape = [jax.ShapeDtypeStruct((n_seq * seq_len, SSD_INNER), F32)]
    if out_state:
        out_specs.append(pl.BlockSpec((None, SSD_INNER, SSD_STATE), lambda b, c: (b, 0, 0)))
        out_shape.append(jax.ShapeDtypeStruct((n_seq, SSD_INNER, SSD_STATE), F32))
    return pl.pallas_call(
        functools.partial(_ssd_scan_kernel, reverse=reverse, has_h0=h0 is not None, out_state=out_state),
        grid=(n_seq, nc),
        in_specs=in_specs,
        out_specs=out_specs,
        out_shape=out_shape,
        scratch_shapes=[pltpu.VMEM((SSD_INNER, SSD_STATE), F32)],
        compiler_params=_params(("arbitrary", "arbitrary")),
        name="ssd_scan",
    )(*args)


def _ssd_out_kernel(yf_ref, yb_ref, x_ref, z_ref, d_ref, nw_ref, w_ref, r_ref, g_ref, o_ref):
    y = yf_ref[...] + yb_ref[...] + d_ref[...] * x_ref[...]
    y = y * _silu(z_ref[...])
    gw = SSD_INNER // SSD_GROUPS
    parts = []
    for g in range(SSD_GROUPS):
        yg = y[:, g * gw:(g + 1) * gw]
        yg = yg * lax.rsqrt(jnp.mean(yg * yg, axis=-1, keepdims=True) + NORM_EPS)
        parts.append((yg * nw_ref[:, g * gw:(g + 1) * gw]).astype(BF16))
    yn = jnp.concatenate(parts, axis=1)
    acc = jnp.dot(yn, w_ref[...].astype(BF16), preferred_element_type=F32)
    o_ref[...] = r_ref[...] + g_ref[...] * acc


def _ssd_out(yf, yb, xbc, zx, d_exp, norm_w, w_out, widx, res, mod, layer, row0, n_rows):
    tm = TM_SSD_OUT
    r0 = row0 // tm
    return pl.pallas_call(
        _ssd_out_kernel,
        grid=(n_rows // tm,),
        in_specs=[
            pl.BlockSpec((tm, SSD_INNER), lambda i: (i, 0)),
            pl.BlockSpec((tm, SSD_INNER), lambda i: (i, 0)),
            pl.BlockSpec((tm, SSD_INNER), lambda i: (i, 0)),
            pl.BlockSpec((tm, SSD_INNER), lambda i: (r0 + i, 0)),
            pl.BlockSpec((1, SSD_INNER), lambda i: (0, 0)),
            pl.BlockSpec((1, SSD_INNER), lambda i: (0, 0)),
            pl.BlockSpec((None, SSD_INNER, D_MODEL), lambda i: (widx, 0, 0)),
            pl.BlockSpec((tm, D_MODEL), lambda i: (r0 + i, 0)),
            pl.BlockSpec((None, None, None, 1, D_MODEL),
                         lambda i: (layer, 2, _group_of_row(row0 + i * tm), 0, 0)),
        ],
        out_specs=pl.BlockSpec((tm, D_MODEL), lambda i: (i, 0)),
        out_shape=jax.ShapeDtypeStruct((n_rows, D_MODEL), F32),
        compiler_params=_params(("arbitrary",)),
        name="ssd_out",
    )(yf, yb, xbc, zx, d_exp, norm_w, w_out, res, mod)


def _ssd_layer(x, gain, mod, layer, j, state_f, state_b, w_in, conv_w, conv_b, dt_bias, a_log, d_skip,
               norm_w, w_out):
    n_zx = SSD_INNER + SSD_CONV_DIM
    zx = _nm_matmul(x, gain, mod, layer, w_in, j, n_zx, name="ssd_in_proj")
    pad = LANES - 2 * SSD_HEADS
    w_dt = jnp.pad(w_in[j][:, n_zx:], ((0, 0), (0, pad)))[None]
    dt_all = _nm_matmul(x, gain, mod, layer, w_dt, 0, LANES, tn=LANES, name="ssd_dt_proj")
    dtb = jnp.pad(dt_bias.reshape(1, -1), ((0, 0), (0, pad)))
    alog = jnp.pad(a_log.reshape(1, -1), ((0, 0), (0, pad)))
    d_exp = jnp.repeat(d_skip, SSD_HEADDIM)[None]
    outs, states = [], []
    for row0, n_seq, seq_len, h0s in ((0, BATCH, SEQ, None), (N_PROMPT, DEC_BATCH, DEC_SEQ, (state_f, state_b))):
        xbc = _ssd_conv(zx, conv_w, conv_b, row0, n_seq, seq_len)
        ys = []
        for reverse in (False, True):
            h0 = None if h0s is None else h0s[int(reverse)].reshape(n_seq, SSD_INNER, SSD_STATE)
            res = _ssd_scan(xbc, dt_all, dtb, alog, row0, n_seq, seq_len, reverse, h0=h0,
                            out_state=h0s is None)
            ys.append(res[0])
            if h0s is None:
                states.append(res[1])
        outs.append(_ssd_out(ys[0], ys[1], xbc, zx, d_exp, norm_w[None], w_out, j, x, mod, layer,
                             row0, n_seq * seq_len))
    shape = (BATCH, SSD_HEADS, SSD_HEADDIM, SSD_STATE)
    return jnp.concatenate(outs, axis=0), states[0].reshape(shape), states[1].reshape(shape)


def _na_layer(x, gain, mod, layer, j, cache_k, cache_v, w_qkv, w_o, q_norm, k_norm, rpb):
    w = NA_HEADS * HEAD_DIM
    qp, kp, vp = _qkv_proj(x, gain, mod, layer, w_qkv, j, q_norm, k_norm, w, 0, N_PROMPT, F32)
    qs, ks, vs = _qkv_proj(x, gain, mod, layer, w_qkv, j, q_norm, k_norm, w, N_PROMPT, N_SAMPLE, BF16)
    op = _ctx_attn(qp, kp, vp)
    ck = cache_k.reshape(DEC_BATCH, PAST_LEN, w).astype(BF16)
    cv = cache_v.reshape(DEC_BATCH, PAST_LEN, w).astype(BF16)
    os_ = _na_latent(qs, ks, vs, ck, cv, _na_bias_table(rpb))
    o = jnp.concatenate([op, os_], axis=0)
    shape = (BATCH, SEQ, NA_HEADS, HEAD_DIM)
    return _linear_res(o, w_o, j, x, mod, layer, 2), kp.reshape(shape), vp.reshape(shape)


def _swa_layer(x, gain, mod, layer, j, cache_k, cache_v, w_qkv, w_o, q_norm, k_norm, sink):
    kw = SWA_KV_HEADS * HEAD_DIM
    qp, kp, vp = _qkv_proj(x, gain, mod, layer, w_qkv, j, q_norm, k_norm, kw, 0, N_PROMPT, F32)
    qs, ks, vs = _qkv_proj(x, gain, mod, layer, w_qkv, j, q_norm, k_norm, kw, N_PROMPT, N_SAMPLE, BF16,
                           rope_tabs=_rope_tables())
    sink = sink.astype(F32)
    op = _ctx_attn(qp, kp, vp, sink)
    ck = cache_k.reshape(DEC_BATCH, PAST_LEN, kw).astype(BF16)
    cv = cache_v.reshape(DEC_BATCH, PAST_LEN, kw).astype(BF16)
    os_ = _swa_latent(qs, ks, vs, ck, cv, sink)
    o = jnp.concatenate([op, os_], axis=0)
    shape = (BATCH, SEQ, SWA_KV_HEADS, HEAD_DIM)
    return _linear_res(o, w_o, j, x, mod, layer, 2), kp.reshape(shape), vp.reshape(shape)


def kernel(x_prompt, x_sample, cache_na_k, cache_na_v, cache_swa_k, cache_swa_v, state_ssd_fwd, state_ssd_bwd, c, c_ctx, ada_w, ada_b, norm_mix, norm_ffn, na_w_qkv, na_w_o, na_q_norm, na_k_norm, na_rpb, swa_w_qkv, swa_w_o, swa_q_norm, swa_k_norm, swa_sink, ssd_w_in, ssd_conv_w, ssd_conv_b, ssd_dt_bias, ssd_a_log, ssd_d, ssd_norm, ssd_w_out, ffn_w_in, ffn_w_out, moe_router, moe_w_in, moe_w_out):
    x = jnp.concatenate([x_prompt.reshape(N_PROMPT, D_MODEL), x_sample.reshape(N_SAMPLE, D_MODEL)], axis=0)
    cond = jnp.concatenate([c_ctx[None], c, jnp.zeros((N_GROUPS - 1 - DEC_BATCH, D_MODEL), F32)], axis=0)
    mod = _adaln(cond, ada_w, ada_b)
    ffn_in_bf, ffn_out_bf = ffn_w_in.astype(BF16), ffn_w_out.astype(BF16)
    na_k, na_v, swa_k, swa_v, ssd_f, ssd_b = [], [], [], [], [], []
    for i in range(DEPTH):
        kind, j = i % N_MIXERS, i // N_MIXERS
        g_mix, g_ffn = norm_mix[i][None], norm_ffn[i][None]
        if kind == 0:
            x, kc, vc = _na_layer(x, g_mix, mod, i, j, cache_na_k[:, j], cache_na_v[:, j], na_w_qkv,
                                  na_w_o, na_q_norm[j], na_k_norm[j], na_rpb[j])
            na_k.append(kc)
            na_v.append(vc)
        elif kind == 1:
            x, kc, vc = _swa_layer(x, g_mix, mod, i, j, cache_swa_k[:, j], cache_swa_v[:, j], swa_w_qkv,
                                   swa_w_o, swa_q_norm[j], swa_k_norm[j], swa_sink[j])
            swa_k.append(kc)
            swa_v.append(vc)
        else:
            x, sf, sb = _ssd_layer(x, g_mix, mod, i, j, state_ssd_fwd[:, j], state_ssd_bwd[:, j], ssd_w_in,
                                   ssd_conv_w[j], ssd_conv_b[j], ssd_dt_bias[j], ssd_a_log[j], ssd_d[j],
                                   ssd_norm[j], ssd_w_out)
            ssd_f.append(sf)
            ssd_b.append(sb)
        if i % 2 == 0:
            x = _dense_ffn(x, g_ffn, mod, i, ffn_in_bf, ffn_out_bf, i // 2)
        else:
            x = _moe_layer(x, g_ffn, mod, i, moe_router[i // 2], moe_w_in, moe_w_out, i // 2)
    yp = x[:N_PROMPT].reshape(BATCH, SEQ, D_MODEL)
    ys = x[N_PROMPT:].reshape(DEC_BATCH, DEC_SEQ, D_MODEL)
    return (yp, ys, jnp.stack(na_k, axis=1), jnp.stack(na_v, axis=1), jnp.stack(swa_k, axis=1),
            jnp.stack(swa_v, axis=1), jnp.stack(ssd_f, axis=1), jnp.stack(ssd_b, axis=1))
```

```python
import functools

import jax
import jax.numpy as jnp
from jax import lax
from jax.experimental import pallas as pl
from jax.experimental.pallas import tpu as pltpu

F32 = jnp.float32
BF16 = jnp.bfloat16
HIGHEST = lax.Precision.HIGHEST

D_MODEL = 1024
BATCH = 32
SEQ = 256
DEPTH = 4
DEC_BATCH = 4
DEC_SEQ = 1024
PAST_LEN = 256
GRID_W = 64
N_MIXERS = 3
HEAD_DIM = 64
NORM_EPS = 1e-6
ROPE_BASE = 10000.0
NA_HEADS = 16
NA_ROWS = 8
NA_COLS = 16
SWA_Q_HEADS = 16
SWA_KV_HEADS = 4
SWA_GROUP = SWA_Q_HEADS // SWA_KV_HEADS
SWA_WINDOW = 128
SWA_BLOCK = 128
SSD_INNER = 2 * D_MODEL
SSD_HEADDIM = 64
SSD_HEADS = SSD_INNER // SSD_HEADDIM
SSD_GROUPS = 4
SSD_STATE = 128
SSD_CONV = 5
SSD_CHUNK = 128
SSD_CONV_DIM = SSD_INNER + 2 * SSD_GROUPS * SSD_STATE
FFN_DIM = 2816
N_EXPERTS = 8
TOP_K = 2
EXPERT_DIM = 3584

N_PROMPT = BATCH * SEQ
N_SAMPLE = DEC_BATCH * DEC_SEQ
N_TOK = N_PROMPT + N_SAMPLE
N_GROUPS = 8
LANES = 128
SUBLANES = 8
MXU_DIM = 256
GRID_ROWS = DEC_SEQ // GRID_W
HEADS_PER_GROUP = SSD_HEADS // SSD_GROUPS
ROW_TILES = D_MODEL // LANES

VMEM_LIMIT = 56 * 1024 * 1024
TM_LIN = 1024
TN_LIN = 512
TM_FFN = 512
TF_DENSE = 1408
TF_MOE = 512
TM_MOE = 1024
SUB_MOE = 512
N_SUB = TM_MOE // SUB_MOE
N_SLOT = N_TOK * TOP_K
N_PAD = N_SLOT + N_EXPERTS * TM_MOE
N_MOE_TILES = N_PAD // TM_MOE
TG = 512
TM_ROUTER = 512
TM_SSD_OUT = 256
assert ROW_TILES == SUBLANES


def _params(sem):
    return pltpu.CompilerParams(dimension_semantics=sem, vmem_limit_bytes=VMEM_LIMIT)


def _group_of_row(start):
    return jnp.where(start < N_PROMPT, 0, 1 + (start - N_PROMPT) // DEC_SEQ)


def _silu(x):
    return x / (1.0 + jnp.exp(-x))


def _norm_mod(x, gain, shift, scale):
    y = x * lax.rsqrt(jnp.mean(x * x, axis=-1, keepdims=True) + NORM_EPS)
    return (y * gain) * (1.0 + scale) + shift


def _adaln_kernel(c_ref, w_ref, b_ref, o_ref):
    s = _silu(c_ref[...])
    o_ref[...] = jnp.dot(s, w_ref[...], precision=HIGHEST, preferred_element_type=F32) + b_ref[...]


def _adaln(cond, ada_w, ada_b):
    tn = 1024
    out = pl.pallas_call(
        _adaln_kernel,
        grid=(DEPTH, 6 * D_MODEL // tn),
        in_specs=[
            pl.BlockSpec((N_GROUPS, D_MODEL), lambda l, j: (0, 0)),
            pl.BlockSpec((None, D_MODEL, tn), lambda l, j: (l, 0, j)),
            pl.BlockSpec((None, 1, tn), lambda l, j: (l, 0, j)),
        ],
        out_specs=pl.BlockSpec((None, N_GROUPS, tn), lambda l, j: (l, 0, j)),
        out_shape=jax.ShapeDtypeStruct((DEPTH, N_GROUPS, 6 * D_MODEL), F32),
        compiler_params=_params(("arbitrary", "arbitrary")),
        name="adaln",
    )(cond, ada_w, ada_b.reshape(DEPTH, 1, 6 * D_MODEL))
    out = out.reshape(DEPTH, N_GROUPS, 6, D_MODEL)
    return jnp.transpose(out, (0, 2, 1, 3)).reshape(DEPTH, 6, N_GROUPS, 1, D_MODEL)


def _nm_matmul_kernel(x_ref, g_ref, sh_ref, sc_ref, w_ref, o_ref, h_ref):
    @pl.when(pl.program_id(1) == 0)
    def _():
        h_ref[...] = _norm_mod(x_ref[...], g_ref[...], sh_ref[...], sc_ref[...]).astype(BF16)

    o_ref[...] = jnp.dot(h_ref[...], w_ref[...].astype(BF16), preferred_element_type=F32)


def _nm_matmul(x, gain, mod, layer, w, widx, n_out, tn=TN_LIN, name="nm_matmul"):
    tm = TM_LIN
    return pl.pallas_call(
        _nm_matmul_kernel,
        grid=(N_TOK // tm, n_out // tn),
        in_specs=[
            pl.BlockSpec((tm, D_MODEL), lambda i, j: (i, 0)),
            pl.BlockSpec((1, D_MODEL), lambda i, j: (0, 0)),
            pl.BlockSpec((None, None, None, 1, D_MODEL),
                         lambda i, j: (layer, 0, _group_of_row(i * tm), 0, 0)),
            pl.BlockSpec((None, None, None, 1, D_MODEL),
                         lambda i, j: (layer, 1, _group_of_row(i * tm), 0, 0)),
            pl.BlockSpec((None, D_MODEL, tn), lambda i, j: (widx, 0, j)),
        ],
        out_specs=pl.BlockSpec((tm, tn), lambda i, j: (i, j)),
        out_shape=jax.ShapeDtypeStruct((N_TOK, n_out), F32),
        scratch_shapes=[pltpu.VMEM((tm, D_MODEL), BF16)],
        compiler_params=_params(("arbitrary", "arbitrary")),
        name=name,
    )(x, gain, mod, mod, w)


def _linear_res_kernel(xp_ref, xs_ref, w_ref, r_ref, g_ref, o_ref, *, n_prompt_tiles):
    x = jnp.where(pl.program_id(0) < n_prompt_tiles, xp_ref[...], xs_ref[...])
    acc = jnp.dot(x, w_ref[...].astype(BF16), preferred_element_type=F32)
    o_ref[...] = r_ref[...] + g_ref[...] * acc


def _linear_res(x_prompt, x_sample, w, widx, res, mod, layer, which):
    tm, tn = TM_LIN, TN_LIN
    k = x_prompt.shape[1]
    n_p = N_PROMPT // tm
    return pl.pallas_call(
        functools.partial(_linear_res_kernel, n_prompt_tiles=n_p),
        grid=(N_TOK // tm, D_MODEL // tn),
        in_specs=[
            pl.BlockSpec((tm, k), lambda i, j: (jnp.minimum(i, n_p - 1), 0)),
            pl.BlockSpec((tm, k), lambda i, j: (jnp.maximum(i - n_p, 0), 0)),
            pl.BlockSpec((None, k, tn), lambda i, j: (widx, 0, j)),
            pl.BlockSpec((tm, tn), lambda i, j: (i, j)),
            pl.BlockSpec((None, None, None, 1, tn),
                         lambda i, j: (layer, which, _group_of_row(i * tm), 0, j)),
        ],
        out_specs=pl.BlockSpec((tm, tn), lambda i, j: (i, j)),
        out_shape=jax.ShapeDtypeStruct((N_TOK, D_MODEL), F32),
        compiler_params=_params(("arbitrary", "arbitrary")),
        name="linear_res",
    )(x_prompt, x_sample, w, res, mod)


def _group_sumsq(x):
    r = lax.broadcasted_iota(jnp.int32, (MXU_DIM, MXU_DIM), 0) // HEAD_DIM
    c = lax.broadcasted_iota(jnp.int32, (MXU_DIM, MXU_DIM), 1) // HEAD_DIM
    ones = jnp.where(r == c, 1.0, 0.0).astype(BF16)
    outs = []
    for t in range(x.shape[1] // MXU_DIM):
        x2 = x[:, t * MXU_DIM:(t + 1) * MXU_DIM]
        x2 = x2 * x2
        hi = x2.astype(BF16)
        lo = (x2 - hi.astype(F32)).astype(BF16)
        outs.append(jnp.dot(hi, ones, preferred_element_type=F32)
                    + jnp.dot(lo, ones, preferred_element_type=F32))
    return jnp.concatenate(outs, axis=1) if len(outs) > 1 else outs[0]


def _head_norm(x, gain):
    return x * lax.rsqrt(_group_sumsq(x) * (1.0 / HEAD_DIM) + NORM_EPS) * gain


def _rope(x, cos, sin):
    w = x.shape[1]
    lane = lax.broadcasted_iota(jnp.int32, x.shape, 1)
    partner = jnp.where((lane % 32) < 16, pltpu.roll(x, w - 16, axis=1), pltpu.roll(x, 16, axis=1))
    return x * cos + partner * sin


def _qkv_kernel(*refs, nq, rope):
    if rope:
        (x_ref, g_ref, sh_ref, sc_ref, w_ref, qg_ref, kg_ref, cos_ref, sin_ref,
         q_ref, k_ref, v_ref, h_ref) = refs
    else:
        x_ref, g_ref, sh_ref, sc_ref, w_ref, qg_ref, kg_ref, q_ref, k_ref, v_ref, h_ref = refs
    j = pl.program_id(1)

    @pl.when(j == 0)
    def _():
        h_ref[...] = _norm_mod(x_ref[...], g_ref[...], sh_ref[...], sc_ref[...]).astype(BF16)

    acc = jnp.dot(h_ref[...], w_ref[...].astype(BF16), preferred_element_type=F32)

    def normed(gain_ref):
        y = _head_norm(acc, gain_ref[...])
        return _rope(y, cos_ref[...], sin_ref[...]) if rope else y

    @pl.when(j < nq)
    def _():
        q_ref[...] = (normed(qg_ref) * (HEAD_DIM ** -0.5)).astype(q_ref.dtype)

    @pl.when(j == nq)
    def _():
        k_ref[...] = normed(kg_ref).astype(k_ref.dtype)

    @pl.when(j == nq + 1)
    def _():
        v_ref[...] = acc.astype(v_ref.dtype)


def _qkv_proj(x, gain, mod, layer, w, widx, q_gain, k_gain, kw, row0, n_rows, kv_dtype, rope_tabs=None):
    tm = TM_LIN
    qw = NA_HEADS * HEAD_DIM
    nq = qw // kw
    r0 = row0 // tm
    in_specs = [
        pl.BlockSpec((tm, D_MODEL), lambda i, j: (r0 + i, 0)),
        pl.BlockSpec((1, D_MODEL), lambda i, j: (0, 0)),
        pl.BlockSpec((None, None, None, 1, D_MODEL),
                     lambda i, j: (layer, 0, _group_of_row(row0 + i * tm), 0, 0)),
        pl.BlockSpec((None, None, None, 1, D_MODEL),
                     lambda i, j: (layer, 1, _group_of_row(row0 + i * tm), 0, 0)),
        pl.BlockSpec((None, D_MODEL, kw), lambda i, j: (widx, 0, j)),
        pl.BlockSpec((1, kw), lambda i, j: (0, 0)),
        pl.BlockSpec((1, kw), lambda i, j: (0, 0)),
    ]
    args = [x, gain, mod, mod, w, jnp.tile(q_gain, kw // HEAD_DIM)[None], jnp.tile(k_gain, kw // HEAD_DIM)[None]]
    if rope_tabs is not None:
        in_specs += [pl.BlockSpec((tm, kw), lambda i, j: (0, 0))] * 2
        args += [t[:, :kw] for t in rope_tabs]
    return pl.pallas_call(
        functools.partial(_qkv_kernel, nq=nq, rope=rope_tabs is not None),
        grid=(n_rows // tm, nq + 2),
        in_specs=in_specs,
        out_specs=[
            pl.BlockSpec((tm, kw), lambda i, j: (i, jnp.minimum(j, nq - 1))),
            pl.BlockSpec((tm, kw), lambda i, j: (i, 0)),
            pl.BlockSpec((tm, kw), lambda i, j: (i, 0)),
        ],
        out_shape=[
            jax.ShapeDtypeStruct((n_rows, qw), BF16),
            jax.ShapeDtypeStruct((n_rows, kw), kv_dtype),
            jax.ShapeDtypeStruct((n_rows, kw), kv_dtype),
        ],
        scratch_shapes=[pltpu.VMEM((tm, D_MODEL), BF16)],
        compiler_params=_params(("arbitrary", "arbitrary")),
        name="qkv_proj",
    )(*args)


def _rope_tables():
    quarter = HEAD_DIM // 4
    t = jnp.arange(DEC_SEQ)
    pos = jnp.stack([t // GRID_W, t % GRID_W], axis=-1).astype(F32)
    inv = ROPE_BASE ** (-jnp.arange(quarter, dtype=F32) / quarter)
    ang = pos[:, :, None] * inv
    cos, sin = jnp.cos(ang), jnp.sin(ang)
    cos64 = jnp.concatenate([cos[:, 0], cos[:, 0], cos[:, 1], cos[:, 1]], axis=1)
    sin64 = jnp.concatenate([-sin[:, 0], sin[:, 0], -sin[:, 1], sin[:, 1]], axis=1)
    return jnp.tile(cos64, (1, SWA_KV_HEADS)), jnp.tile(sin64, (1, SWA_KV_HEADS))


def _nt_dot(a, b):
    return lax.dot_general(a, b, (((1,), (1,)), ((), ())), preferred_element_type=F32)


def _tn_dot(a, b):
    return lax.dot_general(a, b, (((0,), (0,)), ((), ())), preferred_element_type=F32)


def _pair_queries(q):
    n = q.shape[0]
    ri = lax.broadcasted_iota(jnp.int32, (2 * n, LANES), 0)
    ci = lax.broadcasted_iota(jnp.int32, (2 * n, LANES), 1)
    return jnp.where((ri // n) == (ci // HEAD_DIM), jnp.concatenate([q, q], axis=0), jnp.zeros((), BF16))


def _pair_outputs(o2, l):
    n = o2.shape[0] // 2
    o2 = o2 / jnp.broadcast_to(l, (LANES, 2 * n)).T
    first = lax.broadcasted_iota(jnp.int32, (n, LANES), 1) < HEAD_DIM
    return jnp.where(first, o2[:n], o2[n:])


def _ctx_attn_pairs(q_ref, k_ref, v_ref, o_ref):
    outs = []
    for p in range(q_ref.shape[1] // LANES):
        sl = slice(p * LANES, (p + 1) * LANES)
        s = _nt_dot(k_ref[:, sl].astype(BF16), _pair_queries(q_ref[:, sl]))
        e = jnp.exp(s - jnp.max(s, axis=0, keepdims=True))
        l = jnp.sum(e, axis=0, keepdims=True)
        outs.append(_pair_outputs(_tn_dot(e.astype(BF16), v_ref[:, sl].astype(BF16)), l))
    o_ref[...] = jnp.concatenate(outs, axis=1).astype(o_ref.dtype)


def _stack_heads(q_ref, heads):
    parts = [q_ref[:, h * HEAD_DIM:(h + 1) * HEAD_DIM] for h in heads]
    return jnp.concatenate(parts, axis=0) if len(parts) > 1 else parts[0]


def _per_head_column(scalars, rows):
    gid = lax.broadcasted_iota(jnp.int32, (len(scalars) * rows, 1), 0) // rows
    col = jnp.full(gid.shape, scalars[-1], F32)
    for g in range(len(scalars) - 1):
        col = jnp.where(gid == g, scalars[g], col)
    return col


def _ctx_attn_kernel(*refs, group, use_sink):
    if use_sink:
        q_ref, k_ref, v_ref, sink_ref, o_ref = refs
    else:
        q_ref, k_ref, v_ref, o_ref = refs
    rows = q_ref.shape[0]
    if group == 1:
        _ctx_attn_pairs(q_ref, k_ref, v_ref, o_ref)
        return
    outs = []
    for kh in range(k_ref.shape[1] // HEAD_DIM):
        heads = range(kh * group, (kh + 1) * group)
        q = _stack_heads(q_ref, heads)
        k = k_ref[:, kh * HEAD_DIM:(kh + 1) * HEAD_DIM].astype(BF16)
        v = v_ref[:, kh * HEAD_DIM:(kh + 1) * HEAD_DIM].astype(BF16)
        s = _nt_dot(q, k)
        m = jnp.max(s, axis=-1, keepdims=True)
        if use_sink:
            sink = _per_head_column([sink_ref[h] for h in heads], rows)
            m = jnp.maximum(m, sink)
        e = jnp.exp(s - m)
        l = jnp.sum(e, axis=-1, keepdims=True)
        if use_sink:
            l = l + jnp.exp(sink - m)
        o = jnp.dot(e.astype(BF16), v, preferred_element_type=F32) / l
        outs += [o[g * rows:(g + 1) * rows] for g in range(group)]
    o_ref[...] = jnp.concatenate(outs, axis=1).astype(o_ref.dtype)


def _ctx_attn(q, k, v, sink=None):
    kw = k.shape[1]
    group = q.shape[1] // kw
    in_specs = [
        pl.BlockSpec((SEQ, q.shape[1]), lambda b: (b, 0)),
        pl.BlockSpec((SEQ, kw), lambda b: (b, 0)),
        pl.BlockSpec((SEQ, kw), lambda b: (b, 0)),
    ]
    args = [q, k, v]
    if sink is not None:
        in_specs.append(pl.BlockSpec(memory_space=pltpu.SMEM))
        args.append(sink)
    return pl.pallas_call(
        functools.partial(_ctx_attn_kernel, group=group, use_sink=sink is not None),
        grid=(BATCH,),
        in_specs=in_specs,
        out_specs=pl.BlockSpec((SEQ, q.shape[1]), lambda b: (b, 0)),
        out_shape=jax.ShapeDtypeStruct((N_PROMPT, q.shape[1]), BF16),
        compiler_params=_params(("arbitrary",)),
        name="ctx_attn",
    )(*args)


def _na_latent_kernel(q_ref, k_ref, v_ref, ck_ref, cv_ref, bias_ref, o_ref):
    r = pl.program_id(1)
    kr = NA_ROWS
    start = jnp.clip(r - kr // 2, 0, GRID_ROWS - kr)
    row0 = pl.multiple_of(start * GRID_W, GRID_W)
    win = pl.ds(row0, kr * GRID_W)
    outs = []
    for p in range(NA_HEADS // 2):
        sl = slice(p * LANES, (p + 1) * LANES)
        qd = _pair_queries(q_ref[:, sl])
        s_nb = _nt_dot(k_ref[win, sl], qd) + bias_ref[p]
        s_cx = _nt_dot(ck_ref[:, sl], qd)
        m = jnp.maximum(jnp.max(s_nb, axis=0, keepdims=True), jnp.max(s_cx, axis=0, keepdims=True))
        e_nb = jnp.exp(s_nb - m)
        e_cx = jnp.exp(s_cx - m)
        l = jnp.sum(e_nb, axis=0, keepdims=True) + jnp.sum(e_cx, axis=0, keepdims=True)
        o2 = _tn_dot(e_nb.astype(BF16), v_ref[win, sl]) + _tn_dot(e_cx.astype(BF16), cv_ref[:, sl])
        outs.append(_pair_outputs(o2, l))
    o_ref[...] = jnp.concatenate(outs, axis=1).astype(o_ref.dtype)


def _na_bias_table(rpb):
    col = jnp.arange(GRID_W)
    col_start = jnp.clip(col - NA_COLS // 2, 0, GRID_W - NA_COLS)
    col_ok = (col[None, :] >= col_start[:, None]) & (col[None, :] < col_start[:, None] + NA_COLS)
    dc = jnp.clip(col[None, :] - col[:, None], 1 - NA_COLS, NA_COLS - 1) + NA_COLS - 1
    pick = (dc[None] == jnp.arange(2 * NA_COLS - 1)[:, None, None]).astype(F32)
    t = jnp.einsum('hrc,cqk->hrqk', rpb.astype(F32), pick, precision=HIGHEST)
    t = jnp.where(col_ok[None, None], t, -jnp.inf)
    tb = jnp.stack([t[:, d0:d0 + NA_ROWS] for d0 in range(NA_ROWS)], axis=1)
    tb = tb.reshape(NA_HEADS // 2, 2, NA_ROWS, NA_ROWS, GRID_W, GRID_W)
    tb = jnp.transpose(tb, (0, 2, 3, 5, 1, 4))
    return tb.reshape(NA_HEADS // 2, NA_ROWS, NA_ROWS * GRID_W, 2 * GRID_W)


def _na_latent(q, k, v, ck, cv, bias):
    w = NA_HEADS * HEAD_DIM

    def bias_map(b, r):
        start = jnp.clip(r - NA_ROWS // 2, 0, GRID_ROWS - NA_ROWS)
        return (0, start - r + NA_ROWS - 1, 0, 0)

    return pl.pallas_call(
        _na_latent_kernel,
        grid=(DEC_BATCH, GRID_ROWS),
        in_specs=[
            pl.BlockSpec((GRID_W, w), lambda b, r: (b * GRID_ROWS + r, 0)),
            pl.BlockSpec((DEC_SEQ, w), lambda b, r: (b, 0)),
            pl.BlockSpec((DEC_SEQ, w), lambda b, r: (b, 0)),
            pl.BlockSpec((None, PAST_LEN, w), lambda b, r: (b, 0, 0)),
            pl.BlockSpec((None, PAST_LEN, w), lambda b, r: (b, 0, 0)),
            pl.BlockSpec((NA_HEADS // 2, None, NA_ROWS * GRID_W, 2 * GRID_W), bias_map),
        ],
        out_specs=pl.BlockSpec((GRID_W, w), lambda b, r: (b * GRID_ROWS + r, 0)),
        out_shape=jax.ShapeDtypeStruct((N_SAMPLE, w), BF16),
        compiler_params=_params(("arbitrary", "arbitrary")),
        name="na_latent",
    )(q, k, v, ck, cv, bias)


SWA_SPAN = SWA_BLOCK + 2 * SWA_WINDOW


def _swa_latent_kernel(q_ref, k_ref, v_ref, ck_ref, cv_ref, sink_ref, o_ref):
    n = pl.program_id(1)
    k0 = pl.multiple_of(jnp.clip(n - 1, 0, DEC_SEQ // SWA_BLOCK - SWA_SPAN // SWA_BLOCK) * SWA_BLOCK,
                        SWA_BLOCK)
    rows = SWA_GROUP * SWA_BLOCK
    qpos = n * SWA_BLOCK + lax.broadcasted_iota(jnp.int32, (rows, SWA_SPAN), 0) % SWA_BLOCK
    kpos = k0 + lax.broadcasted_iota(jnp.int32, (rows, SWA_SPAN), 1)
    ok = jnp.abs(qpos - kpos) <= SWA_WINDOW
    outs = []
    for kh in range(SWA_KV_HEADS):
        ksl = slice(kh * HEAD_DIM, (kh + 1) * HEAD_DIM)
        heads = range(kh * SWA_GROUP, (kh + 1) * SWA_GROUP)
        q = _stack_heads(q_ref, heads)
        sink = _per_head_column([sink_ref[h] for h in heads], SWA_BLOCK)
        s_b = jnp.where(ok, _nt_dot(q, k_ref[pl.ds(k0, SWA_SPAN), ksl]), -jnp.inf)
        s_c = _nt_dot(q, ck_ref[:, ksl])
        m = jnp.maximum(jnp.maximum(jnp.max(s_b, axis=-1, keepdims=True),
                                    jnp.max(s_c, axis=-1, keepdims=True)), sink)
        e_b = jnp.exp(s_b - m)
        e_c = jnp.exp(s_c - m)
        l = (jnp.sum(e_b, axis=-1, keepdims=True) + jnp.sum(e_c, axis=-1, keepdims=True)
             + jnp.exp(sink - m))
        o = (jnp.dot(e_b.astype(BF16), v_ref[pl.ds(k0, SWA_SPAN), ksl], preferred_element_type=F32)
             + jnp.dot(e_c.astype(BF16), cv_ref[:, ksl], preferred_element_type=F32)) / l
        outs += [o[g * SWA_BLOCK:(g + 1) * SWA_BLOCK] for g in range(SWA_GROUP)]
    o_ref[...] = jnp.concatenate(outs, axis=1).astype(o_ref.dtype)


def _swa_latent(q, k, v, ck, cv, sink):
    qw = SWA_Q_HEADS * HEAD_DIM
    kw = SWA_KV_HEADS * HEAD_DIM
    nb = DEC_SEQ // SWA_BLOCK
    return pl.pallas_call(
        _swa_latent_kernel,
        grid=(DEC_BATCH, nb),
        in_specs=[
            pl.BlockSpec((SWA_BLOCK, qw), lambda b, n: (b * nb + n, 0)),
            pl.BlockSpec((DEC_SEQ, kw), lambda b, n: (b, 0)),
            pl.BlockSpec((DEC_SEQ, kw), lambda b, n: (b, 0)),
            pl.BlockSpec((None, PAST_LEN, kw), lambda b, n: (b, 0, 0)),
            pl.BlockSpec((None, PAST_LEN, kw), lambda b, n: (b, 0, 0)),
            pl.BlockSpec(memory_space=pltpu.SMEM),
        ],
        out_specs=pl.BlockSpec((SWA_BLOCK, qw), lambda b, n: (b * nb + n, 0)),
        out_shape=jax.ShapeDtypeStruct((N_SAMPLE, qw), BF16),
        compiler_params=_params(("arbitrary", "arbitrary")),
        name="swa_latent",
    )(q, k, v, ck, cv, sink)


def _swiglu_part(h, wg, wu, wo):
    g = jnp.dot(h, wg, preferred_element_type=F32)
    u = jnp.dot(h, wu, preferred_element_type=F32)
    a = (_silu(g) * u).astype(BF16)
    return jnp.dot(a, wo, preferred_element_type=F32)


def _dense_ffn_kernel(x_ref, g_ref, sh_ref, sc_ref, wg_ref, wu_ref, wo_ref, gate_ref, o_ref,
                      h_ref, acc_ref):
    j = pl.program_id(1)

    @pl.when(j == 0)
    def _():
        h_ref[...] = _norm_mod(x_ref[...], g_ref[...], sh_ref[...], sc_ref[...]).astype(BF16)

    part = _swiglu_part(h_ref[...], wg_ref[...], wu_ref[...], wo_ref[...])

    @pl.when(j == 0)
    def _():
        acc_ref[...] = part

    @pl.when(j > 0)
    def _():
        acc_ref[...] += part

    @pl.when(j == pl.num_programs(1) - 1)
    def _():
        o_ref[...] = x_ref[...] + gate_ref[...] * acc_ref[...]


def _dense_ffn(x, gain, mod, layer, w_in, w_out, widx):
    tm, tf = TM_FFN, TF_DENSE
    nf = FFN_DIM // tf

    def mspec(which):
        return pl.BlockSpec((None, None, None, 1, D_MODEL),
                            lambda i, j: (layer, which, _group_of_row(i * tm), 0, 0))

    return pl.pallas_call(
        _dense_ffn_kernel,
        grid=(N_TOK // tm, nf),
        in_specs=[
            pl.BlockSpec((tm, D_MODEL), lambda i, j: (i, 0)),
            pl.BlockSpec((1, D_MODEL), lambda i, j: (0, 0)),
            mspec(3), mspec(4),
            pl.BlockSpec((None, D_MODEL, tf), lambda i, j: (widx, 0, j)),
            pl.BlockSpec((None, D_MODEL, tf), lambda i, j: (widx, 0, nf + j)),
            pl.BlockSpec((None, tf, D_MODEL), lambda i, j: (widx, j, 0)),
            mspec(5),
        ],
        out_specs=pl.BlockSpec((tm, D_MODEL), lambda i, j: (i, 0)),
        out_shape=jax.ShapeDtypeStruct((N_TOK, D_MODEL), F32),
        scratch_shapes=[pltpu.VMEM((tm, D_MODEL), BF16), pltpu.VMEM((tm, D_MODEL), F32)],
        compiler_params=_params(("arbitrary", "arbitrary")),
        name="dense_ffn",
    )(x, gain, mod, mod, w_in, w_in, w_out, mod)


def _tile_rows(ref, sub, c, n):
    return ref.at[pl.ds(sub * n * ROW_TILES + c, n, stride=ROW_TILES), :]


def _moe_ffn_kernel(te_ref, ns_ref, x_ref, wg_ref, wu_ref, wo_ref, o_ref,
                    h_ref, acc_ref, wgb_ref, wub_ref, wob_ref):
    i, j = pl.program_id(0), pl.program_id(1)
    nsub = ns_ref[i]
    last = j == pl.num_programs(1) - 1

    @pl.when(nsub > 0)
    def _():
        wgb_ref[...] = wg_ref[...].astype(BF16)
        wub_ref[...] = wu_ref[...].astype(BF16)
        wob_ref[...] = wo_ref[...].astype(BF16)

    for s in range(N_SUB):
        rows = pl.ds(s * SUB_MOE, SUB_MOE)

        @pl.when(s < nsub)
        def _():
            @pl.when(j == 0)
            def _():
                for c in range(ROW_TILES):
                    h_ref[rows, c * LANES:(c + 1) * LANES] = _tile_rows(x_ref, s, c, SUB_MOE)[...].astype(BF16)

            part = _swiglu_part(h_ref[rows, :], wgb_ref[...], wub_ref[...], wob_ref[...])

            @pl.when(j == 0)
            def _():
                acc_ref[rows, :] = part

            @pl.when(j > 0)
            def _():
                acc_ref[rows, :] += part

            @pl.when(last)
            def _():
                for c in range(ROW_TILES):
                    _tile_rows(o_ref, s, c, SUB_MOE)[...] = acc_ref[rows, c * LANES:(c + 1) * LANES]

        @pl.when((s >= nsub) & last)
        def _():
            o_ref[pl.ds(s * SUB_MOE * ROW_TILES, SUB_MOE * ROW_TILES), :] = jnp.zeros(
                (SUB_MOE * ROW_TILES, LANES), F32)


def _moe_ffn(xg, tile_expert, tile_nsub, w_in, w_out, widx):
    tm, tf = TM_MOE, TF_MOE
    nf = EXPERT_DIM // tf

    def jj(i, j, ns):
        return jnp.where(ns[i] > 0, j, nf - 1)

    grid_spec = pltpu.PrefetchScalarGridSpec(
        num_scalar_prefetch=2,
        grid=(N_MOE_TILES, nf),
        in_specs=[
            pl.BlockSpec((tm * ROW_TILES, LANES), lambda i, j, te, ns: (i, 0)),
            pl.BlockSpec((None, None, D_MODEL, tf), lambda i, j, te, ns: (widx, te[i], 0, jj(i, j, ns))),
            pl.BlockSpec((None, None, D_MODEL, tf), lambda i, j, te, ns: (widx, te[i], 0, nf + jj(i, j, ns))),
            pl.BlockSpec((None, None, tf, D_MODEL), lambda i, j, te, ns: (widx, te[i], jj(i, j, ns), 0)),
        ],
        out_specs=pl.BlockSpec((tm * ROW_TILES, LANES), lambda i, j, te, ns: (i, 0)),
        scratch_shapes=[pltpu.VMEM((tm, D_MODEL), BF16), pltpu.VMEM((tm, D_MODEL), F32),
                        pltpu.VMEM((D_MODEL, tf), BF16), pltpu.VMEM((D_MODEL, tf), BF16),
                        pltpu.VMEM((tf, D_MODEL), BF16)],
    )
    return pl.pallas_call(
        _moe_ffn_kernel,
        grid_spec=grid_spec,
        out_shape=jax.ShapeDtypeStruct((N_PAD * ROW_TILES, LANES), F32),
        compiler_params=_params(("arbitrary", "arbitrary")),
        name="moe_ffn",
    )(tile_expert, tile_nsub, xg, w_in, w_in, w_out)


def _router_kernel(x_ref, g_ref, sh_ref, sc_ref, wr_ref, h_ref, info_ref, w_ref, cnt_ref, base_ref):
    tm = x_ref.shape[0]

    @pl.when(pl.program_id(0) == 0)
    def _():
        base_ref[...] = jnp.zeros_like(base_ref)

    h = _norm_mod(x_ref[...], g_ref[...], sh_ref[...], sc_ref[...])
    for c in range(ROW_TILES):
        h_ref[pl.ds(c, tm, stride=ROW_TILES), :] = h[:, c * LANES:(c + 1) * LANES]
    logits = jnp.dot(h, wr_ref[...], precision=HIGHEST, preferred_element_type=F32)
    lane = lax.broadcasted_iota(jnp.int32, logits.shape, 1)
    logits = jnp.where(lane < N_EXPERTS, logits, -jnp.inf)
    m1 = jnp.max(logits, axis=-1, keepdims=True)
    i1 = jnp.min(jnp.where(logits == m1, lane, LANES), axis=-1, keepdims=True)
    rest = jnp.where(lane == i1, -jnp.inf, logits)
    m2 = jnp.max(rest, axis=-1, keepdims=True)
    i2 = jnp.min(jnp.where(rest == m2, lane, LANES), axis=-1, keepdims=True)
    e2 = jnp.exp(m2 - m1)
    den = 1.0 + e2
    w_ref[...] = jnp.where(lane == 0, 1.0 / den, jnp.where(lane == 1, e2 / den, 0.0))
    chosen = jnp.where((lane == i1) | (lane == i2), 1.0, 0.0)
    rt = lax.broadcasted_iota(jnp.int32, (tm, tm), 0)
    ct = lax.broadcasted_iota(jnp.int32, (tm, tm), 1)
    earlier = jnp.where(ct < rt, 1.0, 0.0).astype(BF16)
    before = jnp.dot(earlier, chosen.astype(BF16), preferred_element_type=F32) + base_ref[0:1, :]
    r1 = jnp.sum(jnp.where(lane == i1, before, 0.0), axis=-1, keepdims=True).astype(jnp.int32)
    r2 = jnp.sum(jnp.where(lane == i2, before, 0.0), axis=-1, keepdims=True).astype(jnp.int32)
    info_ref[...] = jnp.where(lane == 0, i1, jnp.where(lane == 1, i2, jnp.where(
        lane == 2, r1, jnp.where(lane == 3, r2, 0))))
    total = base_ref[0:1, :] + jnp.sum(chosen, axis=0, keepdims=True)
    base_ref[...] = jnp.broadcast_to(total, base_ref.shape)
    cnt_ref[...] = jnp.broadcast_to(total, cnt_ref.shape).astype(jnp.int32)


def _router(x, gain, mod, layer, w_router):
    tm = TM_ROUTER
    wr = jnp.pad(w_router, ((0, 0), (0, LANES - N_EXPERTS)))

    def mspec(which):
        return pl.BlockSpec((None, None, None, 1, D_MODEL),
                            lambda i: (layer, which, _group_of_row(i * tm), 0, 0))

    return pl.pallas_call(
        _router_kernel,
        grid=(N_TOK // tm,),
        in_specs=[
            pl.BlockSpec((tm, D_MODEL), lambda i: (i, 0)),
            pl.BlockSpec((1, D_MODEL), lambda i: (0, 0)),
            mspec(3), mspec(4),
            pl.BlockSpec((D_MODEL, LANES), lambda i: (0, 0)),
        ],
        out_specs=[
            pl.BlockSpec((tm * ROW_TILES, LANES), lambda i: (i, 0)),
            pl.BlockSpec((tm, LANES), lambda i: (i, 0)),
            pl.BlockSpec((tm, LANES), lambda i: (i, 0)),
            pl.BlockSpec((SUBLANES, LANES), lambda i: (0, 0)),
        ],
        out_shape=[
            jax.ShapeDtypeStruct((N_TOK * ROW_TILES, LANES), F32),
            jax.ShapeDtypeStruct((N_TOK, LANES), jnp.int32),
            jax.ShapeDtypeStruct((N_TOK, LANES), F32),
            jax.ShapeDtypeStruct((SUBLANES, LANES), jnp.int32),
        ],
        scratch_shapes=[pltpu.VMEM((SUBLANES, LANES), F32)],
        compiler_params=_params(("arbitrary",)),
        name="router",
    )(x, gain, mod, mod, wr)


def _row_copy(src_hbm, row, dst_ref, r, sem):
    src = src_hbm.at[pl.ds(pl.multiple_of(row * ROW_TILES, ROW_TILES), ROW_TILES), :]
    dst = dst_ref.at[pl.ds(pl.multiple_of(r * ROW_TILES, ROW_TILES), ROW_TILES), :]
    return pltpu.make_async_copy(src, dst, sem)


def _invert_kernel(dest_ref, tok_ref):
    def fill(r, carry):
        tok_ref[r] = 0
        return carry

    lax.fori_loop(0, N_PAD, fill, 0, unroll=8)

    def put(s, carry):
        tok_ref[dest_ref[s]] = s // TOP_K
        return carry

    lax.fori_loop(0, N_SLOT, put, 0, unroll=8)


def _invert_slots(dest):
    return pl.pallas_call(
        _invert_kernel,
        in_specs=[pl.BlockSpec(memory_space=pltpu.SMEM)],
        out_specs=pl.BlockSpec(memory_space=pltpu.SMEM),
        out_shape=jax.ShapeDtypeStruct((N_PAD,), jnp.int32),
        name="invert_slots",
    )(dest)


def _gather_kernel(idx_ref, ns_ref, x_hbm, o_ref, sem):
    b = pl.program_id(0)
    base = b * TG
    valid = (b % N_SUB) < ns_ref[b // N_SUB]

    @pl.when(valid)
    def _():
        def issue(r, carry):
            _row_copy(x_hbm, idx_ref[base + r], o_ref, r, sem).start()
            return carry

        lax.fori_loop(0, TG, issue, 0, unroll=8)

        def drain(r, carry):
            _row_copy(x_hbm, 0, o_ref, r, sem).wait()
            return carry

        lax.fori_loop(0, TG, drain, 0, unroll=8)

    @pl.when(jnp.logical_not(valid))
    def _():
        o_ref[...] = jnp.zeros_like(o_ref)


def _gather_rows(x_tiles, idx, tile_nsub):
    grid_spec = pltpu.PrefetchScalarGridSpec(
        num_scalar_prefetch=2,
        grid=(N_PAD // TG,),
        in_specs=[pl.BlockSpec(memory_space=pl.ANY)],
        out_specs=pl.BlockSpec((TG * ROW_TILES, LANES), lambda i, idx, ns: (i, 0)),
        scratch_shapes=[pltpu.SemaphoreType.DMA(())],
    )
    return pl.pallas_call(
        _gather_kernel,
        grid_spec=grid_spec,
        out_shape=jax.ShapeDtypeStruct((N_PAD * ROW_TILES, LANES), F32),
        compiler_params=_params(("arbitrary",)),
        name="gather_rows",
    )(idx, tile_nsub, x_tiles)


def _combine_kernel(pos_ref, y_hbm, res_ref, gate_ref, w_ref, o_ref, a_ref, b_ref, sem):
    base = pl.program_id(0) * TG

    def issue(r, carry):
        _row_copy(y_hbm, pos_ref[2 * (base + r)], a_ref, r, sem.at[0]).start()
        _row_copy(y_hbm, pos_ref[2 * (base + r) + 1], b_ref, r, sem.at[1]).start()
        return carry

    lax.fori_loop(0, TG, issue, 0, unroll=8)

    def drain(r, carry):
        _row_copy(y_hbm, 0, a_ref, r, sem.at[0]).wait()
        _row_copy(y_hbm, 0, b_ref, r, sem.at[1]).wait()
        return carry

    lax.fori_loop(0, TG, drain, 0, unroll=8)
    w = w_ref[...]
    w0, w1 = w[:, 0:1], w[:, 1:2]
    for c in range(ROW_TILES):
        cols = slice(c * LANES, (c + 1) * LANES)
        rows = pl.ds(c, TG, stride=ROW_TILES)
        mix = w0 * a_ref[rows, :] + w1 * b_ref[rows, :]
        o_ref[:, cols] = res_ref[:, cols] + gate_ref[:, cols] * mix


def _combine(yb, pos, res, mod, layer, top_w):
    grid_spec = pltpu.PrefetchScalarGridSpec(
        num_scalar_prefetch=1,
        grid=(N_TOK // TG,),
        in_specs=[
            pl.BlockSpec(memory_space=pl.ANY),
            pl.BlockSpec((TG, D_MODEL), lambda i, p: (i, 0)),
            pl.BlockSpec((None, None, None, 1, D_MODEL),
                         lambda i, p: (layer, 5, _group_of_row(i * TG), 0, 0)),
            pl.BlockSpec((TG, LANES), lambda i, p: (i, 0)),
        ],
        out_specs=pl.BlockSpec((TG, D_MODEL), lambda i, p: (i, 0)),
        scratch_shapes=[pltpu.VMEM((TG * ROW_TILES, LANES), F32), pltpu.VMEM((TG * ROW_TILES, LANES), F32),
                        pltpu.SemaphoreType.DMA((2,))],
    )
    return pl.pallas_call(
        _combine_kernel,
        grid_spec=grid_spec,
        out_shape=jax.ShapeDtypeStruct((N_TOK, D_MODEL), F32),
        compiler_params=_params(("arbitrary",)),
        name="moe_combine",
    )(pos, yb, res, mod, top_w)


def _moe_layer(x, gain, mod, layer, w_router, w_in, w_out, widx):
    h, info, top_w, cnt = _router(x, gain, mod, layer, w_router)
    experts = jnp.arange(N_EXPERTS, dtype=jnp.int32)
    counts = cnt[0, :N_EXPERTS]
    padded = (counts + TM_MOE - 1) // TM_MOE * TM_MOE
    pad_end = jnp.cumsum(padded)
    pad_start = pad_end - padded
    e_sel = info[:, :TOP_K]
    start_sel = jnp.sum(jnp.where(e_sel[..., None] == experts, pad_start, 0), axis=-1)
    dest = (start_sel + info[:, TOP_K:2 * TOP_K]).reshape(-1).astype(jnp.int32)
    row_tok = _invert_slots(dest)
    tile_start = jnp.arange(N_MOE_TILES, dtype=jnp.int32) * TM_MOE
    n_before = jnp.sum(pad_end[None, :] <= tile_start[:, None], axis=1)
    used = tile_start < pad_end[-1]
    last_expert = jnp.sum(pad_end < pad_end[-1])
    tile_expert = jnp.where(used, jnp.minimum(n_before, N_EXPERTS - 1), last_expert).astype(jnp.int32)
    seg_end = jnp.sum(jnp.where(tile_expert[:, None] == experts, pad_start + counts, 0), axis=-1)
    rows_used = jnp.clip(seg_end - tile_start, 0, TM_MOE)
    tile_nsub = jnp.where(used, (rows_used + SUB_MOE - 1) // SUB_MOE, 0).astype(jnp.int32)
    xg = _gather_rows(h, row_tok, tile_nsub)
    yb = _moe_ffn(xg, tile_expert, tile_nsub, w_in, w_out, widx)
    return _combine(yb, dest, x, mod, layer, top_w)


def _conv_kernel(x_ref, w_ref, b_ref, o_ref):
    x = x_ref[...]
    n = x.shape[0]
    t = lax.broadcasted_iota(jnp.int32, x.shape, 0)
    acc = b_ref[...] + w_ref[SSD_CONV // 2:SSD_CONV // 2 + 1, :] * x
    for k in range(SSD_CONV):
        s = k - SSD_CONV // 2
        if s == 0:
            continue
        xs = pltpu.roll(x, (-s) % n, axis=0)
        ok = (t + s >= 0) & (t + s < n)
        acc = acc + w_ref[k:k + 1, :] * jnp.where(ok, xs, 0.0)
    o_ref[...] = _silu(acc)


def _ssd_conv(zx, conv_w, conv_b, row0, n_seq, seq_len):
    tn = 512
    c0 = SSD_INNER // tn
    r0 = row0 // seq_len
    return pl.pallas_call(
        _conv_kernel,
        grid=(n_seq, SSD_CONV_DIM // tn),
        in_specs=[
            pl.BlockSpec((seq_len, tn), lambda b, j: (r0 + b, c0 + j)),
            pl.BlockSpec((SSD_CONV, tn), lambda b, j: (0, j)),
            pl.BlockSpec((1, tn), lambda b, j: (0, j)),
        ],
        out_specs=pl.BlockSpec((seq_len, tn), lambda b, j: (b, j)),
        out_shape=jax.ShapeDtypeStruct((n_seq * seq_len, SSD_CONV_DIM), F32),
        compiler_params=_params(("arbitrary", "arbitrary")),
        name="ssd_conv",
    )(zx, conv_w, conv_b[None])


def _softplus(x):
    return jnp.maximum(x, 0.0) + jnp.log(1.0 + jnp.exp(-jnp.abs(x)))


def _ssd_scan_kernel(*refs, reverse, has_h0, out_state):
    refs = list(refs)
    xbc_ref, dt_ref, dtb_ref, alog_ref = refs[:4]
    pos = 4
    h0_ref = None
    if has_h0:
        h0_ref = refs[pos]
        pos += 1
    y_ref = refs[pos]
    pos += 1
    sf_ref = None
    if out_state:
        sf_ref = refs[pos]
        pos += 1
    s_ref = refs[pos]
    c = pl.program_id(1)
    q = SSD_CHUNK

    @pl.when(c == 0)
    def _():
        if has_h0:
            s_ref[...] = h0_ref[...]
        else:
            s_ref[...] = jnp.zeros_like(s_ref)

    dt = _softplus(dt_ref[...] + dtb_ref[...])
    da = dt * (-jnp.exp(alog_ref[...]))
    ri = lax.broadcasted_iota(jnp.int32, (q, q), 0)
    ci = lax.broadcasted_iota(jnp.int32, (q, q), 1)
    reach = (ri <= ci) if reverse else (ri >= ci)
    cs = jnp.dot(jnp.where(reach, 1.0, 0.0), da, precision=HIGHEST, preferred_element_type=F32)
    cs_t = cs.T
    cs_end = cs[0:1, :] if reverse else cs[q - 1:q, :]
    dec = jnp.exp(cs_end)
    dt_t = dt.T
    ecs_t = jnp.exp(cs).T
    w_t = (dt * jnp.exp(cs_end - cs)).T
    x_t = xbc_ref[:, 0:SSD_INNER].T
    col0 = SSD_HEADS if reverse else 0
    gn = SSD_GROUPS * SSD_STATE
    ys, states = [], []
    for g in range(SSD_GROUPS):
        bg = xbc_ref[:, SSD_INNER + g * SSD_STATE:SSD_INNER + (g + 1) * SSD_STATE].astype(BF16)
        cg = xbc_ref[:, SSD_INNER + gn + g * SSD_STATE:SSD_INNER + gn + (g + 1) * SSD_STATE].astype(BF16)
        cb = _nt_dot(cg, bg)
        for e in range(HEADS_PER_GROUP):
            h = g * HEADS_PER_GROUP + e
            col = col0 + h
            hs = slice(h * SSD_HEADDIM, (h + 1) * SSD_HEADDIM)
            xh_t = x_t[hs, :]
            seg = cs[:, col:col + 1] - cs_t[col:col + 1, :]
            decay = jnp.exp(jnp.where(reach, seg, -jnp.inf))
            state = s_ref[hs, :]
            y_diag_t = _nt_dot((xh_t * dt_t[col:col + 1, :]).astype(BF16), (cb * decay).astype(BF16))
            y_off_t = _nt_dot(state.astype(BF16), cg) * ecs_t[col:col + 1, :]
            ys.append(y_diag_t + y_off_t)
            xw_t = (xh_t * w_t[col:col + 1, :]).astype(BF16)
            states.append(dec[0:1, col:col + 1] * state + jnp.dot(xw_t, bg, preferred_element_type=F32))
    y_ref[...] = jnp.concatenate(ys, axis=0).T
    new_state = jnp.concatenate(states, axis=0)
    s_ref[...] = new_state

    if out_state:
        @pl.when(c == pl.num_programs(1) - 1)
        def _():
            sf_ref[...] = new_state


def _ssd_scan(xbc, dt_all, dt_bias, a_log, row0, n_seq, seq_len, reverse, h0=None, out_state=False):
    nc = seq_len // SSD_CHUNK
    c0 = row0 // SSD_CHUNK

    def chunk(c):
        return nc - 1 - c if reverse else c

    in_specs = [
        pl.BlockSpec((SSD_CHUNK, SSD_CONV_DIM), lambda b, c: (b * nc + chunk(c), 0)),
        pl.BlockSpec((SSD_CHUNK, LANES), lambda b, c: (c0 + b * nc + chunk(c), 0)),
        pl.BlockSpec((1, LANES), lambda b, c: (0, 0)),
        pl.BlockSpec((1, LANES), lambda b, c: (0, 0)),
    ]
    args = [xbc, dt_all, dt_bias, a_log]
    if h0 is not None:
        in_specs.append(pl.BlockSpec((None, SSD_INNER, SSD_STATE), lambda b, c: (b, 0, 0)))
        args.append(h0)
    out_specs = [pl.BlockSpec((SSD_CHUNK, SSD_INNER), lambda b, c: (b * nc + chunk(c), 0))]
    out_shape = [jax.ShapeDtypeStruct((n_seq * seq_len, SSD_INNER), F32)]
    if out_state:
        out_specs.append(pl.BlockSpec((None, SSD_INNER, SSD_STATE), lambda b, c: (b, 0, 0)))
        out_shape.append(jax.ShapeDtypeStruct((n_seq, SSD_INNER, SSD_STATE), F32))
    return pl.pallas_call(
        functools.partial(_ssd_scan_kernel, reverse=reverse, has_h0=h0 is not None, out_state=out_state),
        grid=(n_seq, nc),
        in_specs=in_specs,
        out_specs=out_specs,
        out_shape=out_shape,
        scratch_shapes=[pltpu.VMEM((SSD_INNER, SSD_STATE), F32)],
        compiler_params=_params(("arbitrary", "arbitrary")),
        name="ssd_scan",
    )(*args)


def _ssd_out_kernel(yf_ref, yb_ref, x_ref, z_ref, d_ref, nw_ref, w_ref, r_ref, g_ref, o_ref):
    y = yf_ref[...] + yb_ref[...] + d_ref[...] * x_ref[...]
    y = y * _silu(z_ref[...])
    gw = SSD_INNER // SSD_GROUPS
    parts = []
    for g in range(SSD_GROUPS):
        yg = y[:, g * gw:(g + 1) * gw]
        yg = yg * lax.rsqrt(jnp.mean(yg * yg, axis=-1, keepdims=True) + NORM_EPS)
        parts.append((yg * nw_ref[:, g * gw:(g + 1) * gw]).astype(BF16))
    yn = jnp.concatenate(parts, axis=1)
    acc = jnp.dot(yn, w_ref[...].astype(BF16), preferred_element_type=F32)
    o_ref[...] = r_ref[...] + g_ref[...] * acc


def _ssd_out(yf, yb, xbc, zx, d_exp, norm_w, w_out, widx, res, mod, layer, row0, n_rows):
    tm = TM_SSD_OUT
    r0 = row0 // tm
    return pl.pallas_call(
        _ssd_out_kernel,
        grid=(n_rows // tm,),
        in_specs=[
            pl.BlockSpec((tm, SSD_INNER), lambda i: (i, 0)),
            pl.BlockSpec((tm, SSD_INNER), lambda i: (i, 0)),
            pl.BlockSpec((tm, SSD_INNER), lambda i: (i, 0)),
            pl.BlockSpec((tm, SSD_INNER), lambda i: (r0 + i, 0)),
            pl.BlockSpec((1, SSD_INNER), lambda i: (0, 0)),
            pl.BlockSpec((1, SSD_INNER), lambda i: (0, 0)),
            pl.BlockSpec((None, SSD_INNER, D_MODEL), lambda i: (widx, 0, 0)),
            pl.BlockSpec((tm, D_MODEL), lambda i: (r0 + i, 0)),
            pl.BlockSpec((None, None, None, 1, D_MODEL),
                         lambda i: (layer, 2, _group_of_row(row0 + i * tm), 0, 0)),
        ],
        out_specs=pl.BlockSpec((tm, D_MODEL), lambda i: (i, 0)),
        out_shape=jax.ShapeDtypeStruct((n_rows, D_MODEL), F32),
        compiler_params=_params(("arbitrary",)),
        name="ssd_out",
    )(yf, yb, xbc, zx, d_exp, norm_w, w_out, res, mod)


def _ssd_layer(x, gain, mod, layer, j, state_f, state_b, w_in, conv_w, conv_b, dt_bias, a_log, d_skip,
               norm_w, w_out):
    n_zx = SSD_INNER + SSD_CONV_DIM
    zx = _nm_matmul(x, gain, mod, layer, w_in, j, n_zx, name="ssd_in_proj")
    pad = LANES - 2 * SSD_HEADS
    w_dt = jnp.pad(w_in[j][:, n_zx:], ((0, 0), (0, pad)))[None]
    dt_all = _nm_matmul(x, gain, mod, layer, w_dt, 0, LANES, tn=LANES, name="ssd_dt_proj")
    dtb = jnp.pad(dt_bias.reshape(1, -1), ((0, 0), (0, pad)))
    alog = jnp.pad(a_log.reshape(1, -1), ((0, 0), (0, pad)))
    d_exp = jnp.repeat(d_skip, SSD_HEADDIM)[None]
    outs, states = [], []
    for row0, n_seq, seq_len, h0s in ((0, BATCH, SEQ, None), (N_PROMPT, DEC_BATCH, DEC_SEQ, (state_f, state_b))):
        xbc = _ssd_conv(zx, conv_w, conv_b, row0, n_seq, seq_len)
        ys = []
        for reverse in (False, True):
            h0 = None if h0s is None else h0s[int(reverse)].reshape(n_seq, SSD_INNER, SSD_STATE)
            res = _ssd_scan(xbc, dt_all, dtb, alog, row0, n_seq, seq_len, reverse, h0=h0,
                            out_state=h0s is None)
            ys.append(res[0])
            if h0s is None:
                states.append(res[1])
        outs.append(_ssd_out(ys[0], ys[1], xbc, zx, d_exp, norm_w[None], w_out, j, x, mod, layer,
                             row0, n_seq * seq_len))
    shape = (BATCH, SSD_HEADS, SSD_HEADDIM, SSD_STATE)
    return jnp.concatenate(outs, axis=0), states[0].reshape(shape), states[1].reshape(shape)


def _na_layer(x, gain, mod, layer, j, cache_k, cache_v, w_qkv, w_o, q_norm, k_norm, rpb):
    w = NA_HEADS * HEAD_DIM
    qp, kp, vp = _qkv_proj(x, gain, mod, layer, w_qkv, j, q_norm, k_norm, w, 0, N_PROMPT, F32)
    qs, ks, vs = _qkv_proj(x, gain, mod, layer, w_qkv, j, q_norm, k_norm, w, N_PROMPT, N_SAMPLE, BF16)
    op = _ctx_attn(qp, kp, vp)
    ck = cache_k.reshape(DEC_BATCH, PAST_LEN, w).astype(BF16)
    cv = cache_v.reshape(DEC_BATCH, PAST_LEN, w).astype(BF16)
    os_ = _na_latent(qs, ks, vs, ck, cv, _na_bias_table(rpb))
    shape = (BATCH, SEQ, NA_HEADS, HEAD_DIM)
    return _linear_res(op, os_, w_o, j, x, mod, layer, 2), kp.reshape(shape), vp.reshape(shape)


def _swa_layer(x, gain, mod, layer, j, cache_k, cache_v, w_qkv, w_o, q_norm, k_norm, sink):
    kw = SWA_KV_HEADS * HEAD_DIM
    qp, kp, vp = _qkv_proj(x, gain, mod, layer, w_qkv, j, q_norm, k_norm, kw, 0, N_PROMPT, F32)
    qs, ks, vs = _qkv_proj(x, gain, mod, layer, w_qkv, j, q_norm, k_norm, kw, N_PROMPT, N_SAMPLE, BF16,
                           rope_tabs=_rope_tables())
    sink = sink.astype(F32)
    op = _ctx_attn(qp, kp, vp, sink)
    ck = cache_k.reshape(DEC_BATCH, PAST_LEN, kw).astype(BF16)
    cv = cache_v.reshape(DEC_BATCH, PAST_LEN, kw).astype(BF16)
    os_ = _swa_latent(qs, ks, vs, ck, cv, sink)
    shape = (BATCH, SEQ, SWA_KV_HEADS, HEAD_DIM)
    return _linear_res(op, os_, w_o, j, x, mod, layer, 2), kp.reshape(shape), vp.reshape(shape)


def kernel(x_prompt, x_sample, cache_na_k, cache_na_v, cache_swa_k, cache_swa_v, state_ssd_fwd, state_ssd_bwd, c, c_ctx, ada_w, ada_b, norm_mix, norm_ffn, na_w_qkv, na_w_o, na_q_norm, na_k_norm, na_rpb, swa_w_qkv, swa_w_o, swa_q_norm, swa_k_norm, swa_sink, ssd_w_in, ssd_conv_w, ssd_conv_b, ssd_dt_bias, ssd_a_log, ssd_d, ssd_norm, ssd_w_out, ffn_w_in, ffn_w_out, moe_router, moe_w_in, moe_w_out):
    x = jnp.concatenate([x_prompt.reshape(N_PROMPT, D_MODEL), x_sample.reshape(N_SAMPLE, D_MODEL)], axis=0)
    cond = jnp.concatenate([c_ctx[None], c, jnp.zeros((N_GROUPS - 1 - DEC_BATCH, D_MODEL), F32)], axis=0)
    mod = _adaln(cond, ada_w, ada_b)
    ffn_in_bf, ffn_out_bf = ffn_w_in.astype(BF16), ffn_w_out.astype(BF16)
    na_k, na_v, swa_k, swa_v, ssd_f, ssd_b = [], [], [], [], [], []
    for i in range(DEPTH):
        kind, j = i % N_MIXERS, i // N_MIXERS
        g_mix, g_ffn = norm_mix[i][None], norm_ffn[i][None]
        if kind == 0:
            x, kc, vc = _na_layer(x, g_mix, mod, i, j, cache_na_k[:, j], cache_na_v[:, j], na_w_qkv,
                                  na_w_o, na_q_norm[j], na_k_norm[j], na_rpb[j])
            na_k.append(kc)
            na_v.append(vc)
        elif kind == 1:
            x, kc, vc = _swa_layer(x, g_mix, mod, i, j, cache_swa_k[:, j], cache_swa_v[:, j], swa_w_qkv,
                                   swa_w_o, swa_q_norm[j], swa_k_norm[j], swa_sink[j])
            swa_k.append(kc)
            swa_v.append(vc)
        else:
            x, sf, sb = _ssd_layer(x, g_mix, mod, i, j, state_ssd_fwd[:, j], state_ssd_bwd[:, j], ssd_w_in,
                                   ssd_conv_w[j], ssd_conv_b[j], ssd_dt_bias[j], ssd_a_log[j], ssd_d[j],
                                   ssd_norm[j], ssd_w_out)
            ssd_f.append(sf)
            ssd_b.append(sb)
        if i % 2 == 0:
            x = _dense_ffn(x, g_ffn, mod, i, ffn_in_bf, ffn_out_bf, i // 2)
        else:
            x = _moe_layer(x, g_ffn, mod, i, moe_router[i // 2], moe_w_in, moe_w_out, i // 2)
    yp = x[:N_PROMPT].reshape(BATCH, SEQ, D_MODEL)
    ys = x[N_PROMPT:].reshape(DEC_BATCH, DEC_SEQ, D_MODEL)
    return (yp, ys, jnp.stack(na_k, axis=1), jnp.stack(na_v, axis=1), jnp.stack(swa_k, axis=1),
            jnp.stack(swa_v, axis=1), jnp.stack(ssd_f, axis=1), jnp.stack(ssd_b, axis=1))
```

```python
import functools

import jax
import jax.numpy as jnp
from jax import lax
from jax.experimental import pallas as pl
from jax.experimental.pallas import tpu as pltpu

F32 = jnp.float32
BF16 = jnp.bfloat16
HIGHEST = lax.Precision.HIGHEST

D_MODEL = 1024
BATCH = 32
SEQ = 256
DEPTH = 4
DEC_BATCH = 4
DEC_SEQ = 1024
PAST_LEN = 256
GRID_W = 64
N_MIXERS = 3
HEAD_DIM = 64
NORM_EPS = 1e-6
ROPE_BASE = 10000.0
NA_HEADS = 16
NA_ROWS = 8
NA_COLS = 16
SWA_Q_HEADS = 16
SWA_KV_HEADS = 4
SWA_GROUP = SWA_Q_HEADS // SWA_KV_HEADS
SWA_WINDOW = 128
SWA_BLOCK = 128
SSD_INNER = 2 * D_MODEL
SSD_HEADDIM = 64
SSD_HEADS = SSD_INNER // SSD_HEADDIM
SSD_GROUPS = 4
SSD_STATE = 128
SSD_CONV = 5
SSD_CHUNK = 128
SSD_CONV_DIM = SSD_INNER + 2 * SSD_GROUPS * SSD_STATE
FFN_DIM = 2816
N_EXPERTS = 8
TOP_K = 2
EXPERT_DIM = 3584

N_PROMPT = BATCH * SEQ
N_SAMPLE = DEC_BATCH * DEC_SEQ
N_TOK = N_PROMPT + N_SAMPLE
N_GROUPS = 8
LANES = 128
SUBLANES = 8
MXU_DIM = 256
GRID_ROWS = DEC_SEQ // GRID_W
HEADS_PER_GROUP = SSD_HEADS // SSD_GROUPS
ROW_TILES = D_MODEL // LANES

VMEM_LIMIT = 56 * 1024 * 1024
TM_LIN = 1024
TN_LIN = 512
TM_FFN = 512
TF_DENSE = 1408
TF_MOE = 512
TM_MOE = 1024
SUB_MOE = 512
N_SUB = TM_MOE // SUB_MOE
N_SLOT = N_TOK * TOP_K
N_PAD = N_SLOT + N_EXPERTS * TM_MOE
N_MOE_TILES = N_PAD // TM_MOE
TG = 512
TM_ROUTER = 512
TM_SSD_OUT = 256
assert ROW_TILES == SUBLANES


def _params(sem):
    return pltpu.CompilerParams(dimension_semantics=sem, vmem_limit_bytes=VMEM_LIMIT)


def _group_of_row(start):
    return jnp.where(start < N_PROMPT, 0, 1 + (start - N_PROMPT) // DEC_SEQ)


def _silu(x):
    return x / (1.0 + jnp.exp(-x))


def _norm_mod(x, gain, shift, scale):
    y = x * lax.rsqrt(jnp.mean(x * x, axis=-1, keepdims=True) + NORM_EPS)
    return (y * gain) * (1.0 + scale) + shift


def _adaln_kernel(c_ref, w_ref, b_ref, o_ref):
    s = _silu(c_ref[...])
    o_ref[...] = jnp.dot(s, w_ref[...], precision=HIGHEST, preferred_element_type=F32) + b_ref[...]


def _adaln(cond, ada_w, ada_b):
    tn = 1024
    out = pl.pallas_call(
        _adaln_kernel,
        grid=(DEPTH, 6 * D_MODEL // tn),
        in_specs=[
            pl.BlockSpec((N_GROUPS, D_MODEL), lambda l, j: (0, 0)),
            pl.BlockSpec((None, D_MODEL, tn), lambda l, j: (l, 0, j)),
            pl.BlockSpec((None, 1, tn), lambda l, j: (l, 0, j)),
        ],
        out_specs=pl.BlockSpec((None, N_GROUPS, tn), lambda l, j: (l, 0, j)),
        out_shape=jax.ShapeDtypeStruct((DEPTH, N_GROUPS, 6 * D_MODEL), F32),
        compiler_params=_params(("arbitrary", "arbitrary")),
        name="adaln",
    )(cond, ada_w, ada_b.reshape(DEPTH, 1, 6 * D_MODEL))
    out = out.reshape(DEPTH, N_GROUPS, 6, D_MODEL)
    return jnp.transpose(out, (0, 2, 1, 3)).reshape(DEPTH, 6, N_GROUPS, 1, D_MODEL)


def _nm_matmul_kernel(x_ref, g_ref, sh_ref, sc_ref, w_ref, o_ref, h_ref):
    @pl.when(pl.program_id(1) == 0)
    def _():
        h_ref[...] = _norm_mod(x_ref[...], g_ref[...], sh_ref[...], sc_ref[...]).astype(BF16)

    o_ref[...] = jnp.dot(h_ref[...], w_ref[...].astype(BF16), preferred_element_type=F32)


def _nm_matmul(x, gain, mod, layer, w, widx, n_out, tn=TN_LIN, name="nm_matmul"):
    tm = TM_LIN
    return pl.pallas_call(
        _nm_matmul_kernel,
        grid=(N_TOK // tm, n_out // tn),
        in_specs=[
            pl.BlockSpec((tm, D_MODEL), lambda i, j: (i, 0)),
            pl.BlockSpec((1, D_MODEL), lambda i, j: (0, 0)),
            pl.BlockSpec((None, None, None, 1, D_MODEL),
                         lambda i, j: (layer, 0, _group_of_row(i * tm), 0, 0)),
            pl.BlockSpec((None, None, None, 1, D_MODEL),
                         lambda i, j: (layer, 1, _group_of_row(i * tm), 0, 0)),
            pl.BlockSpec((None, D_MODEL, tn), lambda i, j: (widx, 0, j)),
        ],
        out_specs=pl.BlockSpec((tm, tn), lambda i, j: (i, j)),
        out_shape=jax.ShapeDtypeStruct((N_TOK, n_out), F32),
        scratch_shapes=[pltpu.VMEM((tm, D_MODEL), BF16)],
        compiler_params=_params(("arbitrary", "arbitrary")),
        name=name,
    )(x, gain, mod, mod, w)


def _linear_res_kernel(xp_ref, xs_ref, w_ref, r_ref, g_ref, o_ref, *, n_prompt_tiles):
    x = jnp.where(pl.program_id(0) < n_prompt_tiles, xp_ref[...], xs_ref[...])
    acc = jnp.dot(x, w_ref[...].astype(BF16), preferred_element_type=F32)
    o_ref[...] = r_ref[...] + g_ref[...] * acc


def _linear_res(x_prompt, x_sample, w, widx, res, mod, layer, which):
    tm, tn = TM_LIN, TN_LIN
    k = x_prompt.shape[1]
    n_p = N_PROMPT // tm
    return pl.pallas_call(
        functools.partial(_linear_res_kernel, n_prompt_tiles=n_p),
        grid=(N_TOK // tm, D_MODEL // tn),
        in_specs=[
            pl.BlockSpec((tm, k), lambda i, j: (jnp.minimum(i, n_p - 1), 0)),
            pl.BlockSpec((tm, k), lambda i, j: (jnp.maximum(i - n_p, 0), 0)),
            pl.BlockSpec((None, k, tn), lambda i, j: (widx, 0, j)),
            pl.BlockSpec((tm, tn), lambda i, j: (i, j)),
            pl.BlockSpec((None, None, None, 1, tn),
                         lambda i, j: (layer, which, _group_of_row(i * tm), 0, j)),
        ],
        out_specs=pl.BlockSpec((tm, tn), lambda i, j: (i, j)),
        out_shape=jax.ShapeDtypeStruct((N_TOK, D_MODEL), F32),
        compiler_params=_params(("arbitrary", "arbitrary")),
        name="linear_res",
    )(x_prompt, x_sample, w, res, mod)


def _group_sumsq(x):
    r = lax.broadcasted_iota(jnp.int32, (MXU_DIM, MXU_DIM), 0) // HEAD_DIM
    c = lax.broadcasted_iota(jnp.int32, (MXU_DIM, MXU_DIM), 1) // HEAD_DIM
    ones = jnp.where(r == c, 1.0, 0.0).astype(BF16)
    outs = []
    for t in range(x.shape[1] // MXU_DIM):
        x2 = x[:, t * MXU_DIM:(t + 1) * MXU_DIM]
        x2 = x2 * x2
        hi = x2.astype(BF16)
        lo = (x2 - hi.astype(F32)).astype(BF16)
        outs.append(jnp.dot(hi, ones, preferred_element_type=F32)
                    + jnp.dot(lo, ones, preferred_element_type=F32))
    return jnp.concatenate(outs, axis=1) if len(outs) > 1 else outs[0]


def _head_norm(x, gain):
    return x * lax.rsqrt(_group_sumsq(x) * (1.0 / HEAD_DIM) + NORM_EPS) * gain


def _rope(x, cos, sin):
    w = x.shape[1]
    lane = lax.broadcasted_iota(jnp.int32, x.shape, 1)
    partner = jnp.where((lane % 32) < 16, pltpu.roll(x, w - 16, axis=1), pltpu.roll(x, 16, axis=1))
    return x * cos + partner * sin


def _qkv_kernel(*refs, nq, rope):
    if rope:
        (x_ref, g_ref, sh_ref, sc_ref, w_ref, qg_ref, kg_ref, cos_ref, sin_ref,
         q_ref, k_ref, v_ref, h_ref) = refs
    else:
        x_ref, g_ref, sh_ref, sc_ref, w_ref, qg_ref, kg_ref, q_ref, k_ref, v_ref, h_ref = refs
    j = pl.program_id(1)

    @pl.when(j == 0)
    def _():
        h_ref[...] = _norm_mod(x_ref[...], g_ref[...], sh_ref[...], sc_ref[...]).astype(BF16)

    acc = jnp.dot(h_ref[...], w_ref[...].astype(BF16), preferred_element_type=F32)

    def normed(gain_ref):
        y = _head_norm(acc, gain_ref[...])
        return _rope(y, cos_ref[...], sin_ref[...]) if rope else y

    @pl.when(j < nq)
    def _():
        q_ref[...] = (normed(qg_ref) * (HEAD_DIM ** -0.5)).astype(q_ref.dtype)

    @pl.when(j == nq)
    def _():
        k_ref[...] = normed(kg_ref).astype(k_ref.dtype)

    @pl.when(j == nq + 1)
    def _():
        v_ref[...] = acc.astype(v_ref.dtype)


def _qkv_proj(x, gain, mod, layer, w, widx, q_gain, k_gain, kw, row0, n_rows, kv_dtype, rope_tabs=None):
    tm = TM_LIN
    qw = NA_HEADS * HEAD_DIM
    nq = qw // kw
    r0 = row0 // tm
    in_specs = [
        pl.BlockSpec((tm, D_MODEL), lambda i, j: (r0 + i, 0)),
        pl.BlockSpec((1, D_MODEL), lambda i, j: (0, 0)),
        pl.BlockSpec((None, None, None, 1, D_MODEL),
                     lambda i, j: (layer, 0, _group_of_row(row0 + i * tm), 0, 0)),
        pl.BlockSpec((None, None, None, 1, D_MODEL),
                     lambda i, j: (layer, 1, _group_of_row(row0 + i * tm), 0, 0)),
        pl.BlockSpec((None, D_MODEL, kw), lambda i, j: (widx, 0, j)),
        pl.BlockSpec((1, kw), lambda i, j: (0, 0)),
        pl.BlockSpec((1, kw), lambda i, j: (0, 0)),
    ]
    args = [x, gain, mod, mod, w, jnp.tile(q_gain, kw // HEAD_DIM)[None], jnp.tile(k_gain, kw // HEAD_DIM)[None]]
    if rope_tabs is not None:
        in_specs += [pl.BlockSpec((tm, kw), lambda i, j: (0, 0))] * 2
        args += [t[:, :kw] for t in rope_tabs]
    return pl.pallas_call(
        functools.partial(_qkv_kernel, nq=nq, rope=rope_tabs is not None),
        grid=(n_rows // tm, nq + 2),
        in_specs=in_specs,
        out_specs=[
            pl.BlockSpec((tm, kw), lambda i, j: (i, jnp.minimum(j, nq - 1))),
            pl.BlockSpec((tm, kw), lambda i, j: (i, 0)),
            pl.BlockSpec((tm, kw), lambda i, j: (i, 0)),
        ],
        out_shape=[
            jax.ShapeDtypeStruct((n_rows, qw), BF16),
            jax.ShapeDtypeStruct((n_rows, kw), kv_dtype),
            jax.ShapeDtypeStruct((n_rows, kw), kv_dtype),
        ],
        scratch_shapes=[pltpu.VMEM((tm, D_MODEL), BF16)],
        compiler_params=_params(("arbitrary", "arbitrary")),
        name="qkv_proj",
    )(*args)


def _rope_tables():
    quarter = HEAD_DIM // 4
    t = jnp.arange(DEC_SEQ)
    pos = jnp.stack([t // GRID_W, t % GRID_W], axis=-1).astype(F32)
    inv = ROPE_BASE ** (-jnp.arange(quarter, dtype=F32) / quarter)
    ang = pos[:, :, None] * inv
    cos, sin = jnp.cos(ang), jnp.sin(ang)
    cos64 = jnp.concatenate([cos[:, 0], cos[:, 0], cos[:, 1], cos[:, 1]], axis=1)
    sin64 = jnp.concatenate([-sin[:, 0], sin[:, 0], -sin[:, 1], sin[:, 1]], axis=1)
    return jnp.tile(cos64, (1, SWA_KV_HEADS)), jnp.tile(sin64, (1, SWA_KV_HEADS))


def _nt_dot(a, b):
    return lax.dot_general(a, b, (((1,), (1,)), ((), ())), preferred_element_type=F32)


def _tn_dot(a, b):
    return lax.dot_general(a, b, (((0,), (0,)), ((), ())), preferred_element_type=F32)


def _pair_queries(q):
    n = q.shape[0]
    ri = lax.broadcasted_iota(jnp.int32, (2 * n, LANES), 0)
    ci = lax.broadcasted_iota(jnp.int32, (2 * n, LANES), 1)
    return jnp.where((ri // n) == (ci // HEAD_DIM), jnp.concatenate([q, q], axis=0), jnp.zeros((), BF16))


def _pair_outputs(o2, l):
    n = o2.shape[0] // 2
    o2 = o2 / jnp.broadcast_to(l, (LANES, 2 * n)).T
    first = lax.broadcasted_iota(jnp.int32, (n, LANES), 1) < HEAD_DIM
    return jnp.where(first, o2[:n], o2[n:])


def _ctx_attn_pairs(q_ref, k_ref, v_ref, o_ref):
    outs = []
    for p in range(q_ref.shape[1] // LANES):
        sl = slice(p * LANES, (p + 1) * LANES)
        s = _nt_dot(k_ref[:, sl].astype(BF16), _pair_queries(q_ref[:, sl]))
        e = jnp.exp(s - jnp.max(s, axis=0, keepdims=True))
        l = jnp.sum(e, axis=0, keepdims=True)
        outs.append(_pair_outputs(_tn_dot(e.astype(BF16), v_ref[:, sl].astype(BF16)), l))
    o_ref[...] = jnp.concatenate(outs, axis=1).astype(o_ref.dtype)


def _stack_heads(q_ref, heads):
    parts = [q_ref[:, h * HEAD_DIM:(h + 1) * HEAD_DIM] for h in heads]
    return jnp.concatenate(parts, axis=0) if len(parts) > 1 else parts[0]


def _per_head_column(scalars, rows):
    gid = lax.broadcasted_iota(jnp.int32, (len(scalars) * rows, 1), 0) // rows
    col = jnp.full(gid.shape, scalars[-1], F32)
    for g in range(len(scalars) - 1):
        col = jnp.where(gid == g, scalars[g], col)
    return col


def _ctx_attn_kernel(*refs, group, use_sink):
    if use_sink:
        q_ref, k_ref, v_ref, sink_ref, o_ref = refs
    else:
        q_ref, k_ref, v_ref, o_ref = refs
    rows = q_ref.shape[0]
    if group == 1:
        _ctx_attn_pairs(q_ref, k_ref, v_ref, o_ref)
        return
    outs = []
    for kh in range(k_ref.shape[1] // HEAD_DIM):
        heads = range(kh * group, (kh + 1) * group)
        q = _stack_heads(q_ref, heads)
        k = k_ref[:, kh * HEAD_DIM:(kh + 1) * HEAD_DIM].astype(BF16)
        v = v_ref[:, kh * HEAD_DIM:(kh + 1) * HEAD_DIM].astype(BF16)
        s = _nt_dot(q, k)
        m = jnp.max(s, axis=-1, keepdims=True)
        if use_sink:
            sink = _per_head_column([sink_ref[h] for h in heads], rows)
            m = jnp.maximum(m, sink)
        e = jnp.exp(s - m)
        l = jnp.sum(e, axis=-1, keepdims=True)
        if use_sink:
            l = l + jnp.exp(sink - m)
        o = jnp.dot(e.astype(BF16), v, preferred_element_type=F32) / l
        outs += [o[g * rows:(g + 1) * rows] for g in range(group)]
    o_ref[...] = jnp.concatenate(outs, axis=1).astype(o_ref.dtype)


def _ctx_attn(q, k, v, sink=None):
    kw = k.shape[1]
    group = q.shape[1] // kw
    in_specs = [
        pl.BlockSpec((SEQ, q.shape[1]), lambda b: (b, 0)),
        pl.BlockSpec((SEQ, kw), lambda b: (b, 0)),
        pl.BlockSpec((SEQ, kw), lambda b: (b, 0)),
    ]
    args = [q, k, v]
    if sink is not None:
        in_specs.append(pl.BlockSpec(memory_space=pltpu.SMEM))
        args.append(sink)
    return pl.pallas_call(
        functools.partial(_ctx_attn_kernel, group=group, use_sink=sink is not None),
        grid=(BATCH,),
        in_specs=in_specs,
        out_specs=pl.BlockSpec((SEQ, q.shape[1]), lambda b: (b, 0)),
        out_shape=jax.ShapeDtypeStruct((N_PROMPT, q.shape[1]), BF16),
        compiler_params=_params(("arbitrary",)),
        name="ctx_attn",
    )(*args)


def _na_latent_kernel(q_ref, k_ref, v_ref, ck_ref, cv_ref, bias_ref, o_ref):
    r = pl.program_id(1)
    kr = NA_ROWS
    start = jnp.clip(r - kr // 2, 0, GRID_ROWS - kr)
    row0 = pl.multiple_of(start * GRID_W, GRID_W)
    win = pl.ds(row0, kr * GRID_W)
    outs = []
    for p in range(NA_HEADS // 2):
        sl = slice(p * LANES, (p + 1) * LANES)
        qd = _pair_queries(q_ref[:, sl])
        bias = bias_ref[p, pl.ds(start - r + kr - 1, kr)].reshape(kr * GRID_W, LANES)
        s_nb = _nt_dot(k_ref[win, sl], qd) + bias
        s_cx = _nt_dot(ck_ref[:, sl], qd)
        m = jnp.maximum(jnp.max(s_nb, axis=0, keepdims=True), jnp.max(s_cx, axis=0, keepdims=True))
        e_nb = jnp.exp(s_nb - m)
        e_cx = jnp.exp(s_cx - m)
        l = jnp.sum(e_nb, axis=0, keepdims=True) + jnp.sum(e_cx, axis=0, keepdims=True)
        o2 = _tn_dot(e_nb.astype(BF16), v_ref[win, sl]) + _tn_dot(e_cx.astype(BF16), cv_ref[:, sl])
        outs.append(_pair_outputs(o2, l))
    o_ref[...] = jnp.concatenate(outs, axis=1).astype(o_ref.dtype)


def _na_bias_table(rpb):
    col = jnp.arange(GRID_W)
    col_start = jnp.clip(col - NA_COLS // 2, 0, GRID_W - NA_COLS)
    col_ok = (col[None, :] >= col_start[:, None]) & (col[None, :] < col_start[:, None] + NA_COLS)
    dc = jnp.clip(col[None, :] - col[:, None], 1 - NA_COLS, NA_COLS - 1) + NA_COLS - 1
    pick = (dc[None] == jnp.arange(2 * NA_COLS - 1)[:, None, None]).astype(F32)
    t = jnp.einsum('hrc,cqk->hrqk', rpb.astype(F32), pick, precision=HIGHEST)
    t = jnp.where(col_ok[None, None], t, -jnp.inf)
    t = t.reshape(NA_HEADS // 2, 2, 2 * NA_ROWS - 1, GRID_W, GRID_W)
    t = jnp.transpose(t, (0, 2, 4, 1, 3))
    return t.reshape(NA_HEADS // 2, 2 * NA_ROWS - 1, GRID_W, 2 * GRID_W)


def _na_latent(q, k, v, ck, cv, bias):
    w = NA_HEADS * HEAD_DIM

    return pl.pallas_call(
        _na_latent_kernel,
        grid=(DEC_BATCH, GRID_ROWS),
        in_specs=[
            pl.BlockSpec((GRID_W, w), lambda b, r: (b * GRID_ROWS + r, 0)),
            pl.BlockSpec((DEC_SEQ, w), lambda b, r: (b, 0)),
            pl.BlockSpec((DEC_SEQ, w), lambda b, r: (b, 0)),
            pl.BlockSpec((None, PAST_LEN, w), lambda b, r: (b, 0, 0)),
            pl.BlockSpec((None, PAST_LEN, w), lambda b, r: (b, 0, 0)),
            pl.BlockSpec((NA_HEADS // 2, 2 * NA_ROWS - 1, GRID_W, 2 * GRID_W), lambda b, r: (0, 0, 0, 0)),
        ],
        out_specs=pl.BlockSpec((GRID_W, w), lambda b, r: (b * GRID_ROWS + r, 0)),
        out_shape=jax.ShapeDtypeStruct((N_SAMPLE, w), BF16),
        compiler_params=_params(("arbitrary", "arbitrary")),
        name="na_latent",
    )(q, k, v, ck, cv, bias)


SWA_SPAN = SWA_BLOCK + 2 * SWA_WINDOW


def _swa_latent_kernel(q_ref, k_ref, v_ref, ck_ref, cv_ref, sink_ref, o_ref):
    n = pl.program_id(1)
    k0 = pl.multiple_of(jnp.clip(n - 1, 0, DEC_SEQ // SWA_BLOCK - SWA_SPAN // SWA_BLOCK) * SWA_BLOCK,
                        SWA_BLOCK)
    rows = SWA_GROUP * SWA_BLOCK
    qpos = n * SWA_BLOCK + lax.broadcasted_iota(jnp.int32, (rows, SWA_SPAN), 0) % SWA_BLOCK
    kpos = k0 + lax.broadcasted_iota(jnp.int32, (rows, SWA_SPAN), 1)
    ok = jnp.abs(qpos - kpos) <= SWA_WINDOW
    outs = []
    for kh in range(SWA_KV_HEADS):
        ksl = slice(kh * HEAD_DIM, (kh + 1) * HEAD_DIM)
        heads = range(kh * SWA_GROUP, (kh + 1) * SWA_GROUP)
        q = _stack_heads(q_ref, heads)
        sink = _per_head_column([sink_ref[h] for h in heads], SWA_BLOCK)
        s_b = jnp.where(ok, _nt_dot(q, k_ref[pl.ds(k0, SWA_SPAN), ksl]), -jnp.inf)
        s_c = _nt_dot(q, ck_ref[:, ksl])
        m = jnp.maximum(jnp.maximum(jnp.max(s_b, axis=-1, keepdims=True),
                                    jnp.max(s_c, axis=-1, keepdims=True)), sink)
        e_b = jnp.exp(s_b - m)
        e_c = jnp.exp(s_c - m)
        l = (jnp.sum(e_b, axis=-1, keepdims=True) + jnp.sum(e_c, axis=-1, keepdims=True)
             + jnp.exp(sink - m))
        o = (jnp.dot(e_b.astype(BF16), v_ref[pl.ds(k0, SWA_SPAN), ksl], preferred_element_type=F32)
             + jnp.dot(e_c.astype(BF16), cv_ref[:, ksl], preferred_element_type=F32)) / l
        outs += [o[g * SWA_BLOCK:(g + 1) * SWA_BLOCK] for g in range(SWA_GROUP)]
    o_ref[...] = jnp.concatenate(outs, axis=1).astype(o_ref.dtype)


def _swa_latent(q, k, v, ck, cv, sink):
    qw = SWA_Q_HEADS * HEAD_DIM
    kw = SWA_KV_HEADS * HEAD_DIM
    nb = DEC_SEQ // SWA_BLOCK
    return pl.pallas_call(
        _swa_latent_kernel,
        grid=(DEC_BATCH, nb),
        in_specs=[
            pl.BlockSpec((SWA_BLOCK, qw), lambda b, n: (b * nb + n, 0)),
            pl.BlockSpec((DEC_SEQ, kw), lambda b, n: (b, 0)),
            pl.BlockSpec((DEC_SEQ, kw), lambda b, n: (b, 0)),
            pl.BlockSpec((None, PAST_LEN, kw), lambda b, n: (b, 0, 0)),
            pl.BlockSpec((None, PAST_LEN, kw), lambda b, n: (b, 0, 0)),
            pl.BlockSpec(memory_space=pltpu.SMEM),
        ],
        out_specs=pl.BlockSpec((SWA_BLOCK, qw), lambda b, n: (b * nb + n, 0)),
        out_shape=jax.ShapeDtypeStruct((N_SAMPLE, qw), BF16),
        compiler_params=_params(("arbitrary", "arbitrary")),
        name="swa_latent",
    )(q, k, v, ck, cv, sink)


def _swiglu_part(h, wg, wu, wo):
    g = jnp.dot(h, wg, preferred_element_type=F32)
    u = jnp.dot(h, wu, preferred_element_type=F32)
    a = (_silu(g) * u).astype(BF16)
    return jnp.dot(a, wo, preferred_element_type=F32)


def _dense_ffn_kernel(x_ref, g_ref, sh_ref, sc_ref, wg_ref, wu_ref, wo_ref, gate_ref, o_ref,
                      h_ref, acc_ref):
    j = pl.program_id(1)

    @pl.when(j == 0)
    def _():
        h_ref[...] = _norm_mod(x_ref[...], g_ref[...], sh_ref[...], sc_ref[...]).astype(BF16)

    part = _swiglu_part(h_ref[...], wg_ref[...], wu_ref[...], wo_ref[...])

    @pl.when(j == 0)
    def _():
        acc_ref[...] = part

    @pl.when(j > 0)
    def _():
        acc_ref[...] += part

    @pl.when(j == pl.num_programs(1) - 1)
    def _():
        o_ref[...] = x_ref[...] + gate_ref[...] * acc_ref[...]


def _dense_ffn(x, gain, mod, layer, w_in, w_out, widx):
    tm, tf = TM_FFN, TF_DENSE
    nf = FFN_DIM // tf

    def mspec(which):
        return pl.BlockSpec((None, None, None, 1, D_MODEL),
                            lambda i, j: (layer, which, _group_of_row(i * tm), 0, 0))

    return pl.pallas_call(
        _dense_ffn_kernel,
        grid=(N_TOK // tm, nf),
        in_specs=[
            pl.BlockSpec((tm, D_MODEL), lambda i, j: (i, 0)),
            pl.BlockSpec((1, D_MODEL), lambda i, j: (0, 0)),
            mspec(3), mspec(4),
            pl.BlockSpec((None, D_MODEL, tf), lambda i, j: (widx, 0, j)),
            pl.BlockSpec((None, D_MODEL, tf), lambda i, j: (widx, 0, nf + j)),
            pl.BlockSpec((None, tf, D_MODEL), lambda i, j: (widx, j, 0)),
            mspec(5),
        ],
        out_specs=pl.BlockSpec((tm, D_MODEL), lambda i, j: (i, 0)),
        out_shape=jax.ShapeDtypeStruct((N_TOK, D_MODEL), F32),
        scratch_shapes=[pltpu.VMEM((tm, D_MODEL), BF16), pltpu.VMEM((tm, D_MODEL), F32)],
        compiler_params=_params(("arbitrary", "arbitrary")),
        name="dense_ffn",
    )(x, gain, mod, mod, w_in, w_in, w_out, mod)


def _tile_rows(ref, sub, c, n):
    return ref.at[pl.ds(sub * n * ROW_TILES + c, n, stride=ROW_TILES), :]


def _moe_ffn_kernel(te_ref, ns_ref, tok_ref, x_hbm, wg_ref, wu_ref, wo_ref, o_ref,
                    xbuf_ref, sem, h_ref, acc_ref, wgb_ref, wub_ref, wob_ref):
    i, j = pl.program_id(0), pl.program_id(1)
    nsub = ns_ref[i]
    last = j == pl.num_programs(1) - 1
    slot = i % 2

    def gather(tile, buf, wait):
        for s in range(N_SUB):
            @pl.when(s < ns_ref[tile])
            def _():
                def body(r, carry):
                    src = 0 if wait else tok_ref[tile * TM_MOE + s * SUB_MOE + r]
                    cp = _row_copy(x_hbm, src, xbuf_ref.at[buf], s * SUB_MOE + r, sem.at[buf])
                    cp.wait() if wait else cp.start()
                    return carry

                lax.fori_loop(0, SUB_MOE, body, 0, unroll=8)

    @pl.when((i == 0) & (j == 0))
    def _():
        gather(0, 0, wait=False)

    @pl.when(j == 0)
    def _():
        gather(i, slot, wait=True)

        @pl.when(i + 1 < pl.num_programs(0))
        def _():
            gather(i + 1, 1 - slot, wait=False)

    @pl.when(nsub > 0)
    def _():
        wgb_ref[...] = wg_ref[...].astype(BF16)
        wub_ref[...] = wu_ref[...].astype(BF16)
        wob_ref[...] = wo_ref[...].astype(BF16)

    for s in range(N_SUB):
        rows = pl.ds(s * SUB_MOE, SUB_MOE)

        @pl.when(s < nsub)
        def _():
            @pl.when(j == 0)
            def _():
                for c in range(ROW_TILES):
                    h_ref[rows, c * LANES:(c + 1) * LANES] = _tile_rows(
                        xbuf_ref.at[slot], s, c, SUB_MOE)[...].astype(BF16)

            part = _swiglu_part(h_ref[rows, :], wgb_ref[...], wub_ref[...], wob_ref[...])

            @pl.when(j == 0)
            def _():
                acc_ref[rows, :] = part

            @pl.when(j > 0)
            def _():
                acc_ref[rows, :] += part

            @pl.when(last)
            def _():
                for c in range(ROW_TILES):
                    _tile_rows(o_ref, s, c, SUB_MOE)[...] = acc_ref[rows, c * LANES:(c + 1) * LANES]

        @pl.when((s >= nsub) & last)
        def _():
            o_ref[pl.ds(s * SUB_MOE * ROW_TILES, SUB_MOE * ROW_TILES), :] = jnp.zeros(
                (SUB_MOE * ROW_TILES, LANES), F32)


def _moe_ffn(x_tiles, row_tok, tile_expert, tile_nsub, w_in, w_out, widx):
    tm, tf = TM_MOE, TF_MOE
    nf = EXPERT_DIM // tf

    def jj(i, j, ns):
        return jnp.where(ns[i] > 0, j, nf - 1)

    grid_spec = pltpu.PrefetchScalarGridSpec(
        num_scalar_prefetch=3,
        grid=(N_MOE_TILES, nf),
        in_specs=[
            pl.BlockSpec(memory_space=pl.ANY),
            pl.BlockSpec((None, None, D_MODEL, tf), lambda i, j, te, ns, tok: (widx, te[i], 0, jj(i, j, ns))),
            pl.BlockSpec((None, None, D_MODEL, tf),
                         lambda i, j, te, ns, tok: (widx, te[i], 0, nf + jj(i, j, ns))),
            pl.BlockSpec((None, None, tf, D_MODEL), lambda i, j, te, ns, tok: (widx, te[i], jj(i, j, ns), 0)),
        ],
        out_specs=pl.BlockSpec((tm * ROW_TILES, LANES), lambda i, j, te, ns, tok: (i, 0)),
        scratch_shapes=[pltpu.VMEM((2, tm * ROW_TILES, LANES), F32), pltpu.SemaphoreType.DMA((2,)),
                        pltpu.VMEM((tm, D_MODEL), BF16), pltpu.VMEM((tm, D_MODEL), F32),
                        pltpu.VMEM((D_MODEL, tf), BF16), pltpu.VMEM((D_MODEL, tf), BF16),
                        pltpu.VMEM((tf, D_MODEL), BF16)],
    )
    return pl.pallas_call(
        _moe_ffn_kernel,
        grid_spec=grid_spec,
        out_shape=jax.ShapeDtypeStruct((N_PAD * ROW_TILES, LANES), F32),
        compiler_params=_params(("arbitrary", "arbitrary")),
        name="moe_ffn",
    )(tile_expert, tile_nsub, row_tok, x_tiles, w_in, w_in, w_out)


def _router_kernel(x_ref, g_ref, sh_ref, sc_ref, wr_ref, h_ref, info_ref, w_ref, cnt_ref, base_ref):
    tm = x_ref.shape[0]

    @pl.when(pl.program_id(0) == 0)
    def _():
        base_ref[...] = jnp.zeros_like(base_ref)

    h = _norm_mod(x_ref[...], g_ref[...], sh_ref[...], sc_ref[...])
    for c in range(ROW_TILES):
        h_ref[pl.ds(c, tm, stride=ROW_TILES), :] = h[:, c * LANES:(c + 1) * LANES]
    logits = jnp.dot(h, wr_ref[...], precision=HIGHEST, preferred_element_type=F32)
    lane = lax.broadcasted_iota(jnp.int32, logits.shape, 1)
    logits = jnp.where(lane < N_EXPERTS, logits, -jnp.inf)
    m1 = jnp.max(logits, axis=-1, keepdims=True)
    i1 = jnp.min(jnp.where(logits == m1, lane, LANES), axis=-1, keepdims=True)
    rest = jnp.where(lane == i1, -jnp.inf, logits)
    m2 = jnp.max(rest, axis=-1, keepdims=True)
    i2 = jnp.min(jnp.where(rest == m2, lane, LANES), axis=-1, keepdims=True)
    e2 = jnp.exp(m2 - m1)
    den = 1.0 + e2
    w_ref[...] = jnp.where(lane == 0, 1.0 / den, jnp.where(lane == 1, e2 / den, 0.0))
    chosen = jnp.where((lane == i1) | (lane == i2), 1.0, 0.0)
    rt = lax.broadcasted_iota(jnp.int32, (tm, tm), 0)
    ct = lax.broadcasted_iota(jnp.int32, (tm, tm), 1)
    earlier = jnp.where(ct < rt, 1.0, 0.0).astype(BF16)
    before = jnp.dot(earlier, chosen.astype(BF16), preferred_element_type=F32) + base_ref[0:1, :]
    r1 = jnp.sum(jnp.where(lane == i1, before, 0.0), axis=-1, keepdims=True).astype(jnp.int32)
    r2 = jnp.sum(jnp.where(lane == i2, before, 0.0), axis=-1, keepdims=True).astype(jnp.int32)
    info_ref[...] = jnp.where(lane == 0, i1, jnp.where(lane == 1, i2, jnp.where(
        lane == 2, r1, jnp.where(lane == 3, r2, 0))))
    total = base_ref[0:1, :] + jnp.sum(chosen, axis=0, keepdims=True)
    base_ref[...] = jnp.broadcast_to(total, base_ref.shape)
    cnt_ref[...] = jnp.broadcast_to(total, cnt_ref.shape).astype(jnp.int32)


def _router(x, gain, mod, layer, w_router):
    tm = TM_ROUTER
    wr = jnp.pad(w_router, ((0, 0), (0, LANES - N_EXPERTS)))

    def mspec(which):
        return pl.BlockSpec((None, None, None, 1, D_MODEL),
                            lambda i: (layer, which, _group_of_row(i * tm), 0, 0))

    return pl.pallas_call(
        _router_kernel,
        grid=(N_TOK // tm,),
        in_specs=[
            pl.BlockSpec((tm, D_MODEL), lambda i: (i, 0)),
            pl.BlockSpec((1, D_MODEL), lambda i: (0, 0)),
            mspec(3), mspec(4),
            pl.BlockSpec((D_MODEL, LANES), lambda i: (0, 0)),
        ],
        out_specs=[
            pl.BlockSpec((tm * ROW_TILES, LANES), lambda i: (i, 0)),
            pl.BlockSpec((tm, LANES), lambda i: (i, 0)),
            pl.BlockSpec((tm, LANES), lambda i: (i, 0)),
            pl.BlockSpec((SUBLANES, LANES), lambda i: (0, 0)),
        ],
        out_shape=[
            jax.ShapeDtypeStruct((N_TOK * ROW_TILES, LANES), F32),
            jax.ShapeDtypeStruct((N_TOK, LANES), jnp.int32),
            jax.ShapeDtypeStruct((N_TOK, LANES), F32),
            jax.ShapeDtypeStruct((SUBLANES, LANES), jnp.int32),
        ],
        scratch_shapes=[pltpu.VMEM((SUBLANES, LANES), F32)],
        compiler_params=_params(("arbitrary",)),
        name="router",
    )(x, gain, mod, mod, wr)


def _row_copy(src_hbm, row, dst_ref, r, sem):
    src = src_hbm.at[pl.ds(pl.multiple_of(row * ROW_TILES, ROW_TILES), ROW_TILES), :]
    dst = dst_ref.at[pl.ds(pl.multiple_of(r * ROW_TILES, ROW_TILES), ROW_TILES), :]
    return pltpu.make_async_copy(src, dst, sem)


def _combine_kernel(pos_ref, y_hbm, res_ref, gate_ref, w_ref, o_ref, a_ref, b_ref, sem):
    base = pl.program_id(0) * TG

    def issue(r, carry):
        _row_copy(y_hbm, pos_ref[2 * (base + r)], a_ref, r, sem.at[0]).start()
        _row_copy(y_hbm, pos_ref[2 * (base + r) + 1], b_ref, r, sem.at[1]).start()
        return carry

    lax.fori_loop(0, TG, issue, 0, unroll=8)

    def drain(r, carry):
        _row_copy(y_hbm, 0, a_ref, r, sem.at[0]).wait()
        _row_copy(y_hbm, 0, b_ref, r, sem.at[1]).wait()
        return carry

    lax.fori_loop(0, TG, drain, 0, unroll=8)
    w = w_ref[...]
    w0, w1 = w[:, 0:1], w[:, 1:2]
    for c in range(ROW_TILES):
        cols = slice(c * LANES, (c + 1) * LANES)
        rows = pl.ds(c, TG, stride=ROW_TILES)
        mix = w0 * a_ref[rows, :] + w1 * b_ref[rows, :]
        o_ref[:, cols] = res_ref[:, cols] + gate_ref[:, cols] * mix


def _combine(yb, pos, res, mod, layer, top_w):
    grid_spec = pltpu.PrefetchScalarGridSpec(
        num_scalar_prefetch=1,
        grid=(N_TOK // TG,),
        in_specs=[
            pl.BlockSpec(memory_space=pl.ANY),
            pl.BlockSpec((TG, D_MODEL), lambda i, p: (i, 0)),
            pl.BlockSpec((None, None, None, 1, D_MODEL),
                         lambda i, p: (layer, 5, _group_of_row(i * TG), 0, 0)),
            pl.BlockSpec((TG, LANES), lambda i, p: (i, 0)),
        ],
        out_specs=pl.BlockSpec((TG, D_MODEL), lambda i, p: (i, 0)),
        scratch_shapes=[pltpu.VMEM((TG * ROW_TILES, LANES), F32), pltpu.VMEM((TG * ROW_TILES, LANES), F32),
                        pltpu.SemaphoreType.DMA((2,))],
    )
    return pl.pallas_call(
        _combine_kernel,
        grid_spec=grid_spec,
        out_shape=jax.ShapeDtypeStruct((N_TOK, D_MODEL), F32),
        compiler_params=_params(("arbitrary",)),
        name="moe_combine",
    )(pos, yb, res, mod, top_w)


def _moe_layer(x, gain, mod, layer, w_router, w_in, w_out, widx):
    h, info, top_w, cnt = _router(x, gain, mod, layer, w_router)
    experts = jnp.arange(N_EXPERTS, dtype=jnp.int32)
    counts = cnt[0, :N_EXPERTS]
    padded = (counts + TM_MOE - 1) // TM_MOE * TM_MOE
    pad_end = jnp.cumsum(padded)
    pad_start = pad_end - padded
    e_sel = info[:, :TOP_K]
    start_sel = jnp.sum(jnp.where(e_sel[..., None] == experts, pad_start, 0), axis=-1)
    dest = (start_sel + info[:, TOP_K:2 * TOP_K]).reshape(-1).astype(jnp.int32)
    row_tok = jnp.zeros((N_PAD,), jnp.int32).at[dest].set(jnp.arange(N_SLOT, dtype=jnp.int32) // TOP_K)
    tile_start = jnp.arange(N_MOE_TILES, dtype=jnp.int32) * TM_MOE
    n_before = jnp.sum(pad_end[None, :] <= tile_start[:, None], axis=1)
    used = tile_start < pad_end[-1]
    last_expert = jnp.sum(pad_end < pad_end[-1])
    tile_expert = jnp.where(used, jnp.minimum(n_before, N_EXPERTS - 1), last_expert).astype(jnp.int32)
    seg_end = jnp.sum(jnp.where(tile_expert[:, None] == experts, pad_start + counts, 0), axis=-1)
    rows_used = jnp.clip(seg_end - tile_start, 0, TM_MOE)
    tile_nsub = jnp.where(used, (rows_used + SUB_MOE - 1) // SUB_MOE, 0).astype(jnp.int32)
    yb = _moe_ffn(h, row_tok, tile_expert, tile_nsub, w_in, w_out, widx)
    return _combine(yb, dest, x, mod, layer, top_w)


def _conv_kernel(x_ref, w_ref, b_ref, o_ref):
    x = x_ref[...]
    n = x.shape[0]
    t = lax.broadcasted_iota(jnp.int32, x.shape, 0)
    acc = b_ref[...] + w_ref[SSD_CONV // 2:SSD_CONV // 2 + 1, :] * x
    for k in range(SSD_CONV):
        s = k - SSD_CONV // 2
        if s == 0:
            continue
        xs = pltpu.roll(x, (-s) % n, axis=0)
        ok = (t + s >= 0) & (t + s < n)
        acc = acc + w_ref[k:k + 1, :] * jnp.where(ok, xs, 0.0)
    o_ref[...] = _silu(acc)


def _ssd_conv(zx, conv_w, conv_b, row0, n_seq, seq_len):
    tn = 512
    c0 = SSD_INNER // tn
    r0 = row0 // seq_len
    return pl.pallas_call(
        _conv_kernel,
        grid=(n_seq, SSD_CONV_DIM // tn),
        in_specs=[
            pl.BlockSpec((seq_len, tn), lambda b, j: (r0 + b, c0 + j)),
            pl.BlockSpec((SSD_CONV, tn), lambda b, j: (0, j)),
            pl.BlockSpec((1, tn), lambda b, j: (0, j)),
        ],
        out_specs=pl.BlockSpec((seq_len, tn), lambda b, j: (b, j)),
        out_shape=jax.ShapeDtypeStruct((n_seq * seq_len, SSD_CONV_DIM), F32),
        compiler_params=_params(("arbitrary", "arbitrary")),
        name="ssd_conv",
    )(zx, conv_w, conv_b[None])


def _softplus(x):
    return jnp.maximum(x, 0.0) + jnp.log(1.0 + jnp.exp(-jnp.abs(x)))


def _ssd_scan_kernel(*refs, reverse, has_h0, out_state):
    refs = list(refs)
    xbc_ref, dt_ref, dtb_ref, alog_ref = refs[:4]
    pos = 4
    h0_ref = None
    if has_h0:
        h0_ref = refs[pos]
        pos += 1
    y_ref = refs[pos]
    pos += 1
    sf_ref = None
    if out_state:
        sf_ref = refs[pos]
        pos += 1
    s_ref = refs[pos]
    c = pl.program_id(1)
    q = SSD_CHUNK

    @pl.when(c == 0)
    def _():
        if has_h0:
            s_ref[...] = h0_ref[...]
        else:
            s_ref[...] = jnp.zeros_like(s_ref)

    dt = _softplus(dt_ref[...] + dtb_ref[...])
    da = dt * (-jnp.exp(alog_ref[...]))
    ri = lax.broadcasted_iota(jnp.int32, (q, q), 0)
    ci = lax.broadcasted_iota(jnp.int32, (q, q), 1)
    reach = (ri <= ci) if reverse else (ri >= ci)
    cs = jnp.dot(jnp.where(reach, 1.0, 0.0), da, precision=HIGHEST, preferred_element_type=F32)
    cs_t = cs.T
    cs_end = cs[0:1, :] if reverse else cs[q - 1:q, :]
    dec = jnp.exp(cs_end)
    dt_t = dt.T
    ecs_t = jnp.exp(cs).T
    w_t = (dt * jnp.exp(cs_end - cs)).T
    x_t = xbc_ref[:, 0:SSD_INNER].T
    col0 = SSD_HEADS if reverse else 0
    gn = SSD_GROUPS * SSD_STATE
    ys, states = [], []
    for g in range(SSD_GROUPS):
        bg = xbc_ref[:, SSD_INNER + g * SSD_STATE:SSD_INNER + (g + 1) * SSD_STATE].astype(BF16)
        cg = xbc_ref[:, SSD_INNER + gn + g * SSD_STATE:SSD_INNER + gn + (g + 1) * SSD_STATE].astype(BF16)
        cb = _nt_dot(cg, bg)
        for e in range(HEADS_PER_GROUP):
            h = g * HEADS_PER_GROUP + e
            col = col0 + h
            hs = slice(h * SSD_HEADDIM, (h + 1) * SSD_HEADDIM)
            xh_t = x_t[hs, :]
            seg = cs[:, col:col + 1] - cs_t[col:col + 1, :]
            decay = jnp.exp(jnp.where(reach, seg, -jnp.inf))
            state = s_ref[hs, :]
            y_diag_t = _nt_dot((xh_t * dt_t[col:col + 1, :]).astype(BF16), (cb * decay).astype(BF16))
            y_off_t = _nt_dot(state.astype(BF16), cg) * ecs_t[col:col + 1, :]
            ys.append(y_diag_t + y_off_t)
            xw_t = (xh_t * w_t[col:col + 1, :]).astype(BF16)
            states.append(dec[0:1, col:col + 1] * state + jnp.dot(xw_t, bg, preferred_element_type=F32))
    y_ref[...] = jnp.concatenate(ys, axis=0).T
    new_state = jnp.concatenate(states, axis=0)
    s_ref[...] = new_state

    if out_state:
        @pl.when(c == pl.num_programs(1) - 1)
        def _():
            sf_ref[...] = new_state


def _ssd_scan(xbc, dt_all, dt_bias, a_log, row0, n_seq, seq_len, reverse, h0=None, out_state=False):
    nc = seq_len // SSD_CHUNK
    c0 = row0 // SSD_CHUNK

    def chunk(c):
        return nc - 1 - c if reverse else c

    in_specs = [
        pl.BlockSpec((SSD_CHUNK, SSD_CONV_DIM), lambda b, c: (b * nc + chunk(c), 0)),
        pl.BlockSpec((SSD_CHUNK, LANES), lambda b, c: (c0 + b * nc + chunk(c), 0)),
        pl.BlockSpec((1, LANES), lambda b, c: (0, 0)),
        pl.BlockSpec((1, LANES), lambda b, c: (0, 0)),
    ]
    args = [xbc, dt_all, dt_bias, a_log]
    if h0 is not None:
        in_specs.append(pl.BlockSpec((None, SSD_INNER, SSD_STATE), lambda b, c: (b, 0, 0)))
        args.append(h0)
    out_specs = [pl.BlockSpec((SSD_CHUNK, SSD_INNER), lambda b, c: (b * nc + chunk(c), 0))]
    out_shape = [jax.ShapeDtypeStruct((n_seq * seq_len, SSD_INNER), F32)]
    if out_state:
        out_specs.append(pl.BlockSpec((None, SSD_INNER, SSD_STATE), lambda b, c: (b, 0, 0)))
        out_shape.append(jax.ShapeDtypeStruct((n_seq, SSD_INNER, SSD_STATE), F32))
    return pl.pallas_call(
        functools.partial(_ssd_scan_kernel, reverse=reverse, has_h0=h0 is not None, out_state=out_state),
        grid=(n_seq, nc),
        in_specs=in_specs,
        out_specs=out_specs,
        out_shape=out_shape,
        scratch_shapes=[pltpu.VMEM((SSD_INNER, SSD_STATE), F32)],
        compiler_params=_params(("arbitrary", "arbitrary")),
        name="ssd_scan",
    )(*args)


def _ssd_out_kernel(yf_ref, yb_ref, x_ref, z_ref, d_ref, nw_ref, w_ref, r_ref, g_ref, o_ref):
    y = yf_ref[...] + yb_ref[...] + d_ref[...] * x_ref[...]
    y = y * _silu(z_ref[...])
    gw = SSD_INNER // SSD_GROUPS
    parts = []
    for g in range(SSD_GROUPS):
        yg = y[:, g * gw:(g + 1) * gw]
        yg = yg * lax.rsqrt(jnp.mean(yg * yg, axis=-1, keepdims=True) + NORM_EPS)
        parts.append((yg * nw_ref[:, g * gw:(g + 1) * gw]).astype(BF16))
    yn = jnp.concatenate(parts, axis=1)
    acc = jnp.dot(yn, w_ref[...].astype(BF16), preferred_element_type=F32)
    o_ref[...] = r_ref[...] + g_ref[...] * acc


def _ssd_out(yf, yb, xbc, zx, d_exp, norm_w, w_out, widx, res, mod, layer, row0, n_rows):
    tm = TM_SSD_OUT
    r0 = row0 // tm
    return pl.pallas_call(
        _ssd_out_kernel,
        grid=(n_rows // tm,),
        in_specs=[
            pl.BlockSpec((tm, SSD_INNER), lambda i: (i, 0)),
            pl.BlockSpec((tm, SSD_INNER), lambda i: (i, 0)),
            pl.BlockSpec((tm, SSD_INNER), lambda i: (i, 0)),
            pl.BlockSpec((tm, SSD_INNER), lambda i: (r0 + i, 0)),
            pl.BlockSpec((1, SSD_INNER), lambda i: (0, 0)),
            pl.BlockSpec((1, SSD_INNER), lambda i: (0, 0)),
            pl.BlockSpec((None, SSD_INNER, D_MODEL), lambda i: (widx, 0, 0)),
            pl.BlockSpec((tm, D_MODEL), lambda i: (r0 + i, 0)),
            pl.BlockSpec((None, None, None, 1, D_MODEL),
                         lambda i: (layer, 2, _group_of_row(row0 + i * tm), 0, 0)),
        ],
        out_specs=pl.BlockSpec((tm, D_MODEL), lambda i: (i, 0)),
        out_shape=jax.ShapeDtypeStruct((n_rows, D_MODEL), F32),
        compiler_params=_params(("arbitrary",)),
        name="ssd_out",
    )(yf, yb, xbc, zx, d_exp, norm_w, w_out, res, mod)


def _ssd_layer(x, gain, mod, layer, j, state_f, state_b, w_in, conv_w, conv_b, dt_bias, a_log, d_skip,
               norm_w, w_out):
    n_zx = SSD_INNER + SSD_CONV_DIM
    zx = _nm_matmul(x, gain, mod, layer, w_in, j, n_zx, name="ssd_in_proj")
    pad = LANES - 2 * SSD_HEADS
    w_dt = jnp.pad(w_in[j][:, n_zx:], ((0, 0), (0, pad)))[None]
    dt_all = _nm_matmul(x, gain, mod, layer, w_dt, 0, LANES, tn=LANES, name="ssd_dt_proj")
    dtb = jnp.pad(dt_bias.reshape(1, -1), ((0, 0), (0, pad)))
    alog = jnp.pad(a_log.reshape(1, -1), ((0, 0), (0, pad)))
    d_exp = jnp.repeat(d_skip, SSD_HEADDIM)[None]
    outs, states = [], []
    for row0, n_seq, seq_len, h0s in ((0, BATCH, SEQ, None), (N_PROMPT, DEC_BATCH, DEC_SEQ, (state_f, state_b))):
        xbc = _ssd_conv(zx, conv_w, conv_b, row0, n_seq, seq_len)
        ys = []
        for reverse in (False, True):
            h0 = None if h0s is None else h0s[int(reverse)].reshape(n_seq, SSD_INNER, SSD_STATE)
            res = _ssd_scan(xbc, dt_all, dtb, alog, row0, n_seq, seq_len, reverse, h0=h0,
                            out_state=h0s is None)
            ys.append(res[0])
            if h0s is None:
                states.append(res[1])
        outs.append(_ssd_out(ys[0], ys[1], xbc, zx, d_exp, norm_w[None], w_out, j, x, mod, layer,
                             row0, n_seq * seq_len))
    shape = (BATCH, SSD_HEADS, SSD_HEADDIM, SSD_STATE)
    return jnp.concatenate(outs, axis=0), states[0].reshape(shape), states[1].reshape(shape)


def _na_layer(x, gain, mod, layer, j, cache_k, cache_v, w_qkv, w_o, q_norm, k_norm, rpb):
    w = NA_HEADS * HEAD_DIM
    qp, kp, vp = _qkv_proj(x, gain, mod, layer, w_qkv, j, q_norm, k_norm, w, 0, N_PROMPT, F32)
    qs, ks, vs = _qkv_proj(x, gain, mod, layer, w_qkv, j, q_norm, k_norm, w, N_PROMPT, N_SAMPLE, BF16)
    op = _ctx_attn(qp, kp, vp)
    ck = cache_k.reshape(DEC_BATCH, PAST_LEN, w).astype(BF16)
    cv = cache_v.reshape(DEC_BATCH, PAST_LEN, w).astype(BF16)
    os_ = _na_latent(qs, ks, vs, ck, cv, _na_bias_table(rpb))
    shape = (BATCH, SEQ, NA_HEADS, HEAD_DIM)
    return _linear_res(op, os_, w_o, j, x, mod, layer, 2), kp.reshape(shape), vp.reshape(shape)


def _swa_layer(x, gain, mod, layer, j, cache_k, cache_v, w_qkv, w_o, q_norm, k_norm, sink):
    kw = SWA_KV_HEADS * HEAD_DIM
    qp, kp, vp = _qkv_proj(x, gain, mod, layer, w_qkv, j, q_norm, k_norm, kw, 0, N_PROMPT, F32)
    qs, ks, vs = _qkv_proj(x, gain, mod, layer, w_qkv, j, q_norm, k_norm, kw, N_PROMPT, N_SAMPLE, BF16,
                           rope_tabs=_rope_tables())
    sink = sink.astype(F32)
    op = _ctx_attn(qp, kp, vp, sink)
    ck = cache_k.reshape(DEC_BATCH, PAST_LEN, kw).astype(BF16)
    cv = cache_v.reshape(DEC_BATCH, PAST_LEN, kw).astype(BF16)
    os_ = _swa_latent(qs, ks, vs, ck, cv, sink)
    shape = (BATCH, SEQ, SWA_KV_HEADS, HEAD_DIM)
    return _linear_res(op, os_, w_o, j, x, mod, layer, 2), kp.reshape(shape), vp.reshape(shape)


def kernel(x_prompt, x_sample, cache_na_k, cache_na_v, cache_swa_k, cache_swa_v, state_ssd_fwd, state_ssd_bwd, c, c_ctx, ada_w, ada_b, norm_mix, norm_ffn, na_w_qkv, na_w_o, na_q_norm, na_k_norm, na_rpb, swa_w_qkv, swa_w_o, swa_q_norm, swa_k_norm, swa_sink, ssd_w_in, ssd_conv_w, ssd_conv_b, ssd_dt_bias, ssd_a_log, ssd_d, ssd_norm, ssd_w_out, ffn_w_in, ffn_w_out, moe_router, moe_w_in, moe_w_out):
    x = jnp.concatenate([x_prompt.reshape(N_PROMPT, D_MODEL), x_sample.reshape(N_SAMPLE, D_MODEL)], axis=0)
    cond = jnp.concatenate([c_ctx[None], c, jnp.zeros((N_GROUPS - 1 - DEC_BATCH, D_MODEL), F32)], axis=0)
    mod = _adaln(cond, ada_w, ada_b)
    ffn_in_bf, ffn_out_bf = ffn_w_in.astype(BF16), ffn_w_out.astype(BF16)
    na_k, na_v, swa_k, swa_v, ssd_f, ssd_b = [], [], [], [], [], []
    for i in range(DEPTH):
        kind, j = i % N_MIXERS, i // N_MIXERS
        g_mix, g_ffn = norm_mix[i][None], norm_ffn[i][None]
        if kind == 0:
            x, kc, vc = _na_layer(x, g_mix, mod, i, j, cache_na_k[:, j], cache_na_v[:, j], na_w_qkv,
                                  na_w_o, na_q_norm[j], na_k_norm[j], na_rpb[j])
            na_k.append(kc)
            na_v.append(vc)
        elif kind == 1:
            x, kc, vc = _swa_layer(x, g_mix, mod, i, j, cache_swa_k[:, j], cache_swa_v[:, j], swa_w_qkv,
                                   swa_w_o, swa_q_norm[j], swa_k_norm[j], swa_sink[j])
            swa_k.append(kc)
            swa_v.append(vc)
        else:
            x, sf, sb = _ssd_layer(x, g_mix, mod, i, j, state_ssd_fwd[:, j], state_ssd_bwd[:, j], ssd_w_in,
                                   ssd_conv_w[j], ssd_conv_b[j], ssd_dt_bias[j], ssd_a_log[j], ssd_d[j],
                                   ssd_norm[j], ssd_w_out)
            ssd_f.append(sf)
            ssd_b.append(sb)
        if i % 2 == 0:
            x = _dense_ffn(x, g_ffn, mod, i, ffn_in_bf, ffn_out_bf, i // 2)
        else:
            x = _moe_layer(x, g_ffn, mod, i, moe_router[i // 2], moe_w_in, moe_w_out, i // 2)
    yp = x[:N_PROMPT].reshape(BATCH, SEQ, D_MODEL)
    ys = x[N_PROMPT:].reshape(DEC_BATCH, DEC_SEQ, D_MODEL)
    return (yp, ys, jnp.stack(na_k, axis=1), jnp.stack(na_v, axis=1), jnp.stack(swa_k, axis=1),
            jnp.stack(swa_v, axis=1), jnp.stack(ssd_f, axis=1), jnp.stack(ssd_b, axis=1))
```

```python
import functools

import jax
import jax.numpy as jnp
from jax import lax
from jax.experimental import pallas as pl
from jax.experimental.pallas import tpu as pltpu

F32 = jnp.float32
BF16 = jnp.bfloat16
HIGHEST = lax.Precision.HIGHEST

D_MODEL = 1024
BATCH = 32
SEQ = 256
DEPTH = 4
DEC_BATCH = 4
DEC_SEQ = 1024
PAST_LEN = 256
GRID_W = 64
N_MIXERS = 3
HEAD_DIM = 64
NORM_EPS = 1e-6
ROPE_BASE = 10000.0
NA_HEADS = 16
NA_ROWS = 8
NA_COLS = 16
SWA_Q_HEADS = 16
SWA_KV_HEADS = 4
SWA_GROUP = SWA_Q_HEADS // SWA_KV_HEADS
SWA_WINDOW = 128
SWA_BLOCK = 128
SSD_INNER = 2 * D_MODEL
SSD_HEADDIM = 64
SSD_HEADS = SSD_INNER // SSD_HEADDIM
SSD_GROUPS = 4
SSD_STATE = 128
SSD_CONV = 5
SSD_CHUNK = 128
SSD_CONV_DIM = SSD_INNER + 2 * SSD_GROUPS * SSD_STATE
FFN_DIM = 2816
N_EXPERTS = 8
TOP_K = 2
EXPERT_DIM = 3584

N_PROMPT = BATCH * SEQ
N_SAMPLE = DEC_BATCH * DEC_SEQ
N_TOK = N_PROMPT + N_SAMPLE
N_GROUPS = 8
LANES = 128
SUBLANES = 8
MXU_DIM = 256
GRID_ROWS = DEC_SEQ // GRID_W
HEADS_PER_GROUP = SSD_HEADS // SSD_GROUPS
ROW_TILES = D_MODEL // LANES

VMEM_LIMIT = 56 * 1024 * 1024
TM_LIN = 1024
TN_LIN = 512
TM_FFN = 512
TF_DENSE = 1408
TF_MOE = 512
TM_MOE = 2048
SUB_MOE = 512
N_SUB = TM_MOE // SUB_MOE
N_SLOT = N_TOK * TOP_K
N_PAD = N_SLOT + N_EXPERTS * TM_MOE
N_MOE_TILES = N_PAD // TM_MOE
TG = 512
TM_ROUTER = 512
TM_SSD_OUT = 256
assert ROW_TILES == SUBLANES


def _params(sem):
    return pltpu.CompilerParams(dimension_semantics=sem, vmem_limit_bytes=VMEM_LIMIT)


def _group_of_row(start):
    return jnp.where(start < N_PROMPT, 0, 1 + (start - N_PROMPT) // DEC_SEQ)


def _silu(x):
    return x / (1.0 + jnp.exp(-x))


def _norm_mod(x, gain, shift, scale):
    y = x * lax.rsqrt(jnp.mean(x * x, axis=-1, keepdims=True) + NORM_EPS)
    return (y * gain) * (1.0 + scale) + shift


def _adaln_kernel(c_ref, w_ref, b_ref, o_ref):
    s = _silu(c_ref[...])
    o_ref[...] = jnp.dot(s, w_ref[...], precision=HIGHEST, preferred_element_type=F32) + b_ref[...]


def _adaln(cond, ada_w, ada_b):
    tn = 1024
    out = pl.pallas_call(
        _adaln_kernel,
        grid=(DEPTH, 6 * D_MODEL // tn),
        in_specs=[
            pl.BlockSpec((N_GROUPS, D_MODEL), lambda l, j: (0, 0)),
            pl.BlockSpec((None, D_MODEL, tn), lambda l, j: (l, 0, j)),
            pl.BlockSpec((None, 1, tn), lambda l, j: (l, 0, j)),
        ],
        out_specs=pl.BlockSpec((None, N_GROUPS, tn), lambda l, j: (l, 0, j)),
        out_shape=jax.ShapeDtypeStruct((DEPTH, N_GROUPS, 6 * D_MODEL), F32),
        compiler_params=_params(("arbitrary", "arbitrary")),
        name="adaln",
    )(cond, ada_w, ada_b.reshape(DEPTH, 1, 6 * D_MODEL))
    out = out.reshape(DEPTH, N_GROUPS, 6, D_MODEL)
    return jnp.transpose(out, (0, 2, 1, 3)).reshape(DEPTH, 6, N_GROUPS, 1, D_MODEL)


def _nm_matmul_kernel(x_ref, g_ref, sh_ref, sc_ref, w_ref, o_ref, h_ref):
    @pl.when(pl.program_id(1) == 0)
    def _():
        h_ref[...] = _norm_mod(x_ref[...], g_ref[...], sh_ref[...], sc_ref[...]).astype(BF16)

    o_ref[...] = jnp.dot(h_ref[...], w_ref[...].astype(BF16), preferred_element_type=F32)


def _nm_matmul(x, gain, mod, layer, w, widx, n_out, tn=TN_LIN, name="nm_matmul"):
    tm = TM_LIN
    return pl.pallas_call(
        _nm_matmul_kernel,
        grid=(N_TOK // tm, n_out // tn),
        in_specs=[
            pl.BlockSpec((tm, D_MODEL), lambda i, j: (i, 0)),
            pl.BlockSpec((1, D_MODEL), lambda i, j: (0, 0)),
            pl.BlockSpec((None, None, None, 1, D_MODEL),
                         lambda i, j: (layer, 0, _group_of_row(i * tm), 0, 0)),
            pl.BlockSpec((None, None, None, 1, D_MODEL),
                         lambda i, j: (layer, 1, _group_of_row(i * tm), 0, 0)),
            pl.BlockSpec((None, D_MODEL, tn), lambda i, j: (widx, 0, j)),
        ],
        out_specs=pl.BlockSpec((tm, tn), lambda i, j: (i, j)),
        out_shape=jax.ShapeDtypeStruct((N_TOK, n_out), F32),
        scratch_shapes=[pltpu.VMEM((tm, D_MODEL), BF16)],
        compiler_params=_params(("arbitrary", "arbitrary")),
        name=name,
    )(x, gain, mod, mod, w)


def _split_stream(arr, tm, n_col):
    a, b = arr if isinstance(arr, tuple) else (arr, arr)
    n_p = N_PROMPT // tm
    off = 0 if isinstance(arr, tuple) else n_p
    blk = (tm, a.shape[1] // n_col)

    def col(own, j, parked):
        return jnp.where(own, j, parked) if n_col > 1 else 0

    spec_a = pl.BlockSpec(blk, lambda i, j: (jnp.minimum(i, n_p - 1), col(i < n_p, j, n_col - 1)))
    spec_b = pl.BlockSpec(blk, lambda i, j: (off + jnp.maximum(i - n_p, 0), col(i >= n_p, j, 0)))
    return (a, b), (spec_a, spec_b)


def _pick_half(a_ref, b_ref, tm):
    return jnp.where(pl.program_id(0) < N_PROMPT // tm, a_ref[...], b_ref[...])


def _linear_res_kernel(xp_ref, xs_ref, w_ref, rp_ref, rs_ref, g_ref, o_ref):
    tm = o_ref.shape[0]
    acc = jnp.dot(_pick_half(xp_ref, xs_ref, tm), w_ref[...].astype(BF16), preferred_element_type=F32)
    o_ref[...] = _pick_half(rp_ref, rs_ref, tm) + g_ref[...] * acc


def _linear_res(x, w, widx, res, mod, layer, which):
    tm, tn = TM_LIN, TN_LIN
    n_col = D_MODEL // tn
    x_arrs, x_specs = _split_stream(x, tm, 1)
    r_arrs, r_specs = _split_stream(res, tm, n_col)
    k = x_arrs[0].shape[1]
    return pl.pallas_call(
        _linear_res_kernel,
        grid=(N_TOK // tm, n_col),
        in_specs=[
            *x_specs,
            pl.BlockSpec((None, k, tn), lambda i, j: (widx, 0, j)),
            *r_specs,
            pl.BlockSpec((None, None, None, 1, tn),
                         lambda i, j: (layer, which, _group_of_row(i * tm), 0, j)),
        ],
        out_specs=pl.BlockSpec((tm, tn), lambda i, j: (i, j)),
        out_shape=jax.ShapeDtypeStruct((N_TOK, D_MODEL), F32),
        compiler_params=_params(("arbitrary", "arbitrary")),
        name="linear_res",
    )(*x_arrs, w, *r_arrs, mod)


def _group_sumsq(x):
    r = lax.broadcasted_iota(jnp.int32, (MXU_DIM, MXU_DIM), 0) // HEAD_DIM
    c = lax.broadcasted_iota(jnp.int32, (MXU_DIM, MXU_DIM), 1) // HEAD_DIM
    ones = jnp.where(r == c, 1.0, 0.0).astype(BF16)
    outs = []
    for t in range(x.shape[1] // MXU_DIM):
        x2 = x[:, t * MXU_DIM:(t + 1) * MXU_DIM]
        x2 = x2 * x2
        hi = x2.astype(BF16)
        lo = (x2 - hi.astype(F32)).astype(BF16)
        outs.append(jnp.dot(hi, ones, preferred_element_type=F32)
                    + jnp.dot(lo, ones, preferred_element_type=F32))
    return jnp.concatenate(outs, axis=1) if len(outs) > 1 else outs[0]


def _head_norm(x, gain):
    return x * lax.rsqrt(_group_sumsq(x) * (1.0 / HEAD_DIM) + NORM_EPS) * gain


def _rope(x, cos, sin):
    w = x.shape[1]
    lane = lax.broadcasted_iota(jnp.int32, x.shape, 1)
    partner = jnp.where((lane % 32) < 16, pltpu.roll(x, w - 16, axis=1), pltpu.roll(x, 16, axis=1))
    return x * cos + partner * sin


def _qkv_kernel(*refs, nq, rope):
    if rope:
        (x_ref, g_ref, sh_ref, sc_ref, w_ref, qg_ref, kg_ref, cos_ref, sin_ref,
         q_ref, k_ref, v_ref, h_ref) = refs
    else:
        x_ref, g_ref, sh_ref, sc_ref, w_ref, qg_ref, kg_ref, q_ref, k_ref, v_ref, h_ref = refs
    j = pl.program_id(1)

    @pl.when(j == 0)
    def _():
        h_ref[...] = _norm_mod(x_ref[...], g_ref[...], sh_ref[...], sc_ref[...]).astype(BF16)

    acc = jnp.dot(h_ref[...], w_ref[...].astype(BF16), preferred_element_type=F32)

    def normed(gain_ref):
        y = _head_norm(acc, gain_ref[...])
        return _rope(y, cos_ref[...], sin_ref[...]) if rope else y

    @pl.when(j < nq)
    def _():
        q_ref[...] = (normed(qg_ref) * (HEAD_DIM ** -0.5)).astype(q_ref.dtype)

    @pl.when(j == nq)
    def _():
        k_ref[...] = normed(kg_ref).astype(k_ref.dtype)

    @pl.when(j == nq + 1)
    def _():
        v_ref[...] = acc.astype(v_ref.dtype)


def _qkv_proj(x, gain, mod, layer, w, widx, q_gain, k_gain, kw, row0, n_rows, kv_dtype, rope_tabs=None):
    tm = TM_LIN
    qw = NA_HEADS * HEAD_DIM
    nq = qw // kw
    r0 = row0 // tm
    if isinstance(x, tuple):
        x, r0 = (x[0], 0) if row0 == 0 else (x[1], 0)
    in_specs = [
        pl.BlockSpec((tm, D_MODEL), lambda i, j: (r0 + i, 0)),
        pl.BlockSpec((1, D_MODEL), lambda i, j: (0, 0)),
        pl.BlockSpec((None, None, None, 1, D_MODEL),
                     lambda i, j: (layer, 0, _group_of_row(row0 + i * tm), 0, 0)),
        pl.BlockSpec((None, None, None, 1, D_MODEL),
                     lambda i, j: (layer, 1, _group_of_row(row0 + i * tm), 0, 0)),
        pl.BlockSpec((None, D_MODEL, kw), lambda i, j: (widx, 0, j)),
        pl.BlockSpec((1, kw), lambda i, j: (0, 0)),
        pl.BlockSpec((1, kw), lambda i, j: (0, 0)),
    ]
    args = [x, gain, mod, mod, w, jnp.tile(q_gain, kw // HEAD_DIM)[None], jnp.tile(k_gain, kw // HEAD_DIM)[None]]
    if rope_tabs is not None:
        in_specs += [pl.BlockSpec((tm, kw), lambda i, j: (0, 0))] * 2
        args += [t[:, :kw] for t in rope_tabs]
    return pl.pallas_call(
        functools.partial(_qkv_kernel, nq=nq, rope=rope_tabs is not None),
        grid=(n_rows // tm, nq + 2),
        in_specs=in_specs,
        out_specs=[
            pl.BlockSpec((tm, kw), lambda i, j: (i, jnp.minimum(j, nq - 1))),
            pl.BlockSpec((tm, kw), lambda i, j: (i, 0)),
            pl.BlockSpec((tm, kw), lambda i, j: (i, 0)),
        ],
        out_shape=[
            jax.ShapeDtypeStruct((n_rows, qw), BF16),
            jax.ShapeDtypeStruct((n_rows, kw), kv_dtype),
            jax.ShapeDtypeStruct((n_rows, kw), kv_dtype),
        ],
        scratch_shapes=[pltpu.VMEM((tm, D_MODEL), BF16)],
        compiler_params=_params(("arbitrary", "arbitrary")),
        name="qkv_proj",
    )(*args)


def _rope_tables():
    quarter = HEAD_DIM // 4
    t = jnp.arange(DEC_SEQ)
    pos = jnp.stack([t // GRID_W, t % GRID_W], axis=-1).astype(F32)
    inv = ROPE_BASE ** (-jnp.arange(quarter, dtype=F32) / quarter)
    ang = pos[:, :, None] * inv
    cos, sin = jnp.cos(ang), jnp.sin(ang)
    cos64 = jnp.concatenate([cos[:, 0], cos[:, 0], cos[:, 1], cos[:, 1]], axis=1)
    sin64 = jnp.concatenate([-sin[:, 0], sin[:, 0], -sin[:, 1], sin[:, 1]], axis=1)
    return jnp.tile(cos64, (1, SWA_KV_HEADS)), jnp.tile(sin64, (1, SWA_KV_HEADS))


def _nt_dot(a, b):
    return lax.dot_general(a, b, (((1,), (1,)), ((), ())), preferred_element_type=F32)


def _tn_dot(a, b):
    return lax.dot_general(a, b, (((0,), (0,)), ((), ())), preferred_element_type=F32)


def _pair_queries(q):
    n = q.shape[0]
    ri = lax.broadcasted_iota(jnp.int32, (2 * n, LANES), 0)
    ci = lax.broadcasted_iota(jnp.int32, (2 * n, LANES), 1)
    return jnp.where((ri // n) == (ci // HEAD_DIM), jnp.concatenate([q, q], axis=0), jnp.zeros((), BF16))


def _pair_outputs(o2, l):
    n = o2.shape[0] // 2
    o2 = o2 / jnp.broadcast_to(l, (LANES, 2 * n)).T
    first = lax.broadcasted_iota(jnp.int32, (n, LANES), 1) < HEAD_DIM
    return jnp.where(first, o2[:n], o2[n:])


def _ctx_attn_pairs(q_ref, k_ref, v_ref, o_ref):
    outs = []
    for p in range(q_ref.shape[1] // LANES):
        sl = slice(p * LANES, (p + 1) * LANES)
        s = _nt_dot(k_ref[:, sl].astype(BF16), _pair_queries(q_ref[:, sl]))
        e = jnp.exp(s - jnp.max(s, axis=0, keepdims=True))
        l = jnp.sum(e, axis=0, keepdims=True)
        outs.append(_pair_outputs(_tn_dot(e.astype(BF16), v_ref[:, sl].astype(BF16)), l))
    o_ref[...] = jnp.concatenate(outs, axis=1).astype(o_ref.dtype)


def _stack_heads(q_ref, heads):
    parts = [q_ref[:, h * HEAD_DIM:(h + 1) * HEAD_DIM] for h in heads]
    return jnp.concatenate(parts, axis=0) if len(parts) > 1 else parts[0]


def _per_head_column(scalars, rows):
    gid = lax.broadcasted_iota(jnp.int32, (len(scalars) * rows, 1), 0) // rows
    col = jnp.full(gid.shape, scalars[-1], F32)
    for g in range(len(scalars) - 1):
        col = jnp.where(gid == g, scalars[g], col)
    return col


def _ctx_attn_kernel(*refs, group, use_sink):
    if use_sink:
        q_ref, k_ref, v_ref, sink_ref, o_ref = refs
    else:
        q_ref, k_ref, v_ref, o_ref = refs
    rows = q_ref.shape[0]
    if group == 1:
        _ctx_attn_pairs(q_ref, k_ref, v_ref, o_ref)
        return
    outs = []
    for kh in range(k_ref.shape[1] // HEAD_DIM):
        heads = range(kh * group, (kh + 1) * group)
        q = _stack_heads(q_ref, heads)
        k = k_ref[:, kh * HEAD_DIM:(kh + 1) * HEAD_DIM].astype(BF16)
        v = v_ref[:, kh * HEAD_DIM:(kh + 1) * HEAD_DIM].astype(BF16)
        s = _nt_dot(q, k)
        m = jnp.max(s, axis=-1, keepdims=True)
        if use_sink:
            sink = _per_head_column([sink_ref[h] for h in heads], rows)
            m = jnp.maximum(m, sink)
        e = jnp.exp(s - m)
        l = jnp.sum(e, axis=-1, keepdims=True)
        if use_sink:
            l = l + jnp.exp(sink - m)
        o = jnp.dot(e.astype(BF16), v, preferred_element_type=F32) / l
        outs += [o[g * rows:(g + 1) * rows] for g in range(group)]
    o_ref[...] = jnp.concatenate(outs, axis=1).astype(o_ref.dtype)


def _ctx_attn(q, k, v, sink=None):
    kw = k.shape[1]
    group = q.shape[1] // kw
    in_specs = [
        pl.BlockSpec((SEQ, q.shape[1]), lambda b: (b, 0)),
        pl.BlockSpec((SEQ, kw), lambda b: (b, 0)),
        pl.BlockSpec((SEQ, kw), lambda b: (b, 0)),
    ]
    args = [q, k, v]
    if sink is not None:
        in_specs.append(pl.BlockSpec(memory_space=pltpu.SMEM))
        args.append(sink)
    return pl.pallas_call(
        functools.partial(_ctx_attn_kernel, group=group, use_sink=sink is not None),
        grid=(BATCH,),
        in_specs=in_specs,
        out_specs=pl.BlockSpec((SEQ, q.shape[1]), lambda b: (b, 0)),
        out_shape=jax.ShapeDtypeStruct((N_PROMPT, q.shape[1]), BF16),
        compiler_params=_params(("arbitrary",)),
        name="ctx_attn",
    )(*args)


def _na_latent_kernel(q_ref, k_ref, v_ref, ck_ref, cv_ref, bias_ref, o_ref):
    r = pl.program_id(1)
    kr = NA_ROWS
    start = jnp.clip(r - kr // 2, 0, GRID_ROWS - kr)
    row0 = pl.multiple_of(start * GRID_W, GRID_W)
    win = pl.ds(row0, kr * GRID_W)
    outs = []
    for p in range(NA_HEADS // 2):
        sl = slice(p * LANES, (p + 1) * LANES)
        qd = _pair_queries(q_ref[:, sl])
        bias = bias_ref[p, pl.ds(start - r + kr - 1, kr)].reshape(kr * GRID_W, LANES)
        s_nb = _nt_dot(k_ref[win, sl], qd) + bias
        s_cx = _nt_dot(ck_ref[:, sl], qd)
        m = jnp.maximum(jnp.max(s_nb, axis=0, keepdims=True), jnp.max(s_cx, axis=0, keepdims=True))
        e_nb = jnp.exp(s_nb - m)
        e_cx = jnp.exp(s_cx - m)
        l = jnp.sum(e_nb, axis=0, keepdims=True) + jnp.sum(e_cx, axis=0, keepdims=True)
        o2 = _tn_dot(e_nb.astype(BF16), v_ref[win, sl]) + _tn_dot(e_cx.astype(BF16), cv_ref[:, sl])
        outs.append(_pair_outputs(o2, l))
    o_ref[...] = jnp.concatenate(outs, axis=1).astype(o_ref.dtype)


def _na_bias_table(rpb):
    col = jnp.arange(GRID_W)
    col_start = jnp.clip(col - NA_COLS // 2, 0, GRID_W - NA_COLS)
    col_ok = (col[None, :] >= col_start[:, None]) & (col[None, :] < col_start[:, None] + NA_COLS)
    dc = jnp.clip(col[None, :] - col[:, None], 1 - NA_COLS, NA_COLS - 1) + NA_COLS - 1
    pick = (dc[None] == jnp.arange(2 * NA_COLS - 1)[:, None, None]).astype(F32)
    t = jnp.einsum('hrc,cqk->hrqk', rpb.astype(F32), pick, precision=HIGHEST)
    t = jnp.where(col_ok[None, None], t, -jnp.inf)
    t = t.reshape(NA_HEADS // 2, 2, 2 * NA_ROWS - 1, GRID_W, GRID_W)
    t = jnp.transpose(t, (0, 2, 4, 1, 3))
    return t.reshape(NA_HEADS // 2, 2 * NA_ROWS - 1, GRID_W, 2 * GRID_W)


def _na_latent(q, k, v, ck, cv, bias):
    w = NA_HEADS * HEAD_DIM

    return pl.pallas_call(
        _na_latent_kernel,
        grid=(DEC_BATCH, GRID_ROWS),
        in_specs=[
            pl.BlockSpec((GRID_W, w), lambda b, r: (b * GRID_ROWS + r, 0)),
            pl.BlockSpec((DEC_SEQ, w), lambda b, r: (b, 0)),
            pl.BlockSpec((DEC_SEQ, w), lambda b, r: (b, 0)),
            pl.BlockSpec((None, PAST_LEN, w), lambda b, r: (b, 0, 0)),
            pl.BlockSpec((None, PAST_LEN, w), lambda b, r: (b, 0, 0)),
            pl.BlockSpec((NA_HEADS // 2, 2 * NA_ROWS - 1, GRID_W, 2 * GRID_W), lambda b, r: (0, 0, 0, 0)),
        ],
        out_specs=pl.BlockSpec((GRID_W, w), lambda b, r: (b * GRID_ROWS + r, 0)),
        out_shape=jax.ShapeDtypeStruct((N_SAMPLE, w), BF16),
        compiler_params=_params(("arbitrary", "arbitrary")),
        name="na_latent",
    )(q, k, v, ck, cv, bias)


SWA_SPAN = SWA_BLOCK + 2 * SWA_WINDOW


def _swa_latent_kernel(q_ref, k_ref, v_ref, ck_ref, cv_ref, sink_ref, o_ref):
    n = pl.program_id(1)
    k0 = pl.multiple_of(jnp.clip(n - 1, 0, DEC_SEQ // SWA_BLOCK - SWA_SPAN // SWA_BLOCK) * SWA_BLOCK,
                        SWA_BLOCK)
    rows = SWA_GROUP * SWA_BLOCK
    qpos = n * SWA_BLOCK + lax.broadcasted_iota(jnp.int32, (rows, SWA_SPAN), 0) % SWA_BLOCK
    kpos = k0 + lax.broadcasted_iota(jnp.int32, (rows, SWA_SPAN), 1)
    ok = jnp.abs(qpos - kpos) <= SWA_WINDOW
    outs = []
    for kh in range(SWA_KV_HEADS):
        ksl = slice(kh * HEAD_DIM, (kh + 1) * HEAD_DIM)
        heads = range(kh * SWA_GROUP, (kh + 1) * SWA_GROUP)
        q = _stack_heads(q_ref, heads)
        sink = _per_head_column([sink_ref[h] for h in heads], SWA_BLOCK)
        s_b = jnp.where(ok, _nt_dot(q, k_ref[pl.ds(k0, SWA_SPAN), ksl]), -jnp.inf)
        s_c = _nt_dot(q, ck_ref[:, ksl])
        m = jnp.maximum(jnp.maximum(jnp.max(s_b, axis=-1, keepdims=True),
                                    jnp.max(s_c, axis=-1, keepdims=True)), sink)
        e_b = jnp.exp(s_b - m)
        e_c = jnp.exp(s_c - m)
        l = (jnp.sum(e_b, axis=-1, keepdims=True) + jnp.sum(e_c, axis=-1, keepdims=True)
             + jnp.exp(sink - m))
        o = (jnp.dot(e_b.astype(BF16), v_ref[pl.ds(k0, SWA_SPAN), ksl], preferred_element_type=F32)
             + jnp.dot(e_c.astype(BF16), cv_ref[:, ksl], preferred_element_type=F32)) / l
        outs += [o[g * SWA_BLOCK:(g + 1) * SWA_BLOCK] for g in range(SWA_GROUP)]
    o_ref[...] = jnp.concatenate(outs, axis=1).astype(o_ref.dtype)


def _swa_latent(q, k, v, ck, cv, sink):
    qw = SWA_Q_HEADS * HEAD_DIM
    kw = SWA_KV_HEADS * HEAD_DIM
    nb = DEC_SEQ // SWA_BLOCK
    return pl.pallas_call(
        _swa_latent_kernel,
        grid=(DEC_BATCH, nb),
        in_specs=[
            pl.BlockSpec((SWA_BLOCK, qw), lambda b, n: (b * nb + n, 0)),
            pl.BlockSpec((DEC_SEQ, kw), lambda b, n: (b, 0)),
            pl.BlockSpec((DEC_SEQ, kw), lambda b, n: (b, 0)),
            pl.BlockSpec((None, PAST_LEN, kw), lambda b, n: (b, 0, 0)),
            pl.BlockSpec((None, PAST_LEN, kw), lambda b, n: (b, 0, 0)),
            pl.BlockSpec(memory_space=pltpu.SMEM),
        ],
        out_specs=pl.BlockSpec((SWA_BLOCK, qw), lambda b, n: (b * nb + n, 0)),
        out_shape=jax.ShapeDtypeStruct((N_SAMPLE, qw), BF16),
        compiler_params=_params(("arbitrary", "arbitrary")),
        name="swa_latent",
    )(q, k, v, ck, cv, sink)


def _swiglu_part(h, wg, wu, wo):
    g = jnp.dot(h, wg, preferred_element_type=F32)
    u = jnp.dot(h, wu, preferred_element_type=F32)
    a = (_silu(g) * u).astype(BF16)
    return jnp.dot(a, wo, preferred_element_type=F32)


def _dense_ffn_kernel(xp_ref, xs_ref, g_ref, sh_ref, sc_ref, wg_ref, wu_ref, wo_ref, gate_ref, o_ref,
                      h_ref, acc_ref):
    j = pl.program_id(1)
    tm = o_ref.shape[0]

    @pl.when(j == 0)
    def _():
        x = _pick_half(xp_ref, xs_ref, tm)
        h_ref[...] = _norm_mod(x, g_ref[...], sh_ref[...], sc_ref[...]).astype(BF16)

    part = _swiglu_part(h_ref[...], wg_ref[...], wu_ref[...], wo_ref[...])

    @pl.when(j == 0)
    def _():
        acc_ref[...] = part

    @pl.when(j > 0)
    def _():
        acc_ref[...] += part

    @pl.when(j == pl.num_programs(1) - 1)
    def _():
        o_ref[...] = _pick_half(xp_ref, xs_ref, tm) + gate_ref[...] * acc_ref[...]


def _dense_ffn(x, gain, mod, layer, w_in, w_out, widx):
    tm, tf = TM_FFN, TF_DENSE
    nf = FFN_DIM // tf
    x_arrs, x_specs = _split_stream(x, tm, 1)

    def mspec(which):
        return pl.BlockSpec((None, None, None, 1, D_MODEL),
                            lambda i, j: (layer, which, _group_of_row(i * tm), 0, 0))

    return pl.pallas_call(
        _dense_ffn_kernel,
        grid=(N_TOK // tm, nf),
        in_specs=[
            *x_specs,
            pl.BlockSpec((1, D_MODEL), lambda i, j: (0, 0)),
            mspec(3), mspec(4),
            pl.BlockSpec((None, D_MODEL, tf), lambda i, j: (widx, 0, j)),
            pl.BlockSpec((None, D_MODEL, tf), lambda i, j: (widx, 0, nf + j)),
            pl.BlockSpec((None, tf, D_MODEL), lambda i, j: (widx, j, 0)),
            mspec(5),
        ],
        out_specs=pl.BlockSpec((tm, D_MODEL), lambda i, j: (i, 0)),
        out_shape=jax.ShapeDtypeStruct((N_TOK, D_MODEL), F32),
        scratch_shapes=[pltpu.VMEM((tm, D_MODEL), BF16), pltpu.VMEM((tm, D_MODEL), F32)],
        compiler_params=_params(("arbitrary", "arbitrary")),
        name="dense_ffn",
    )(*x_arrs, gain, mod, mod, w_in, w_in, w_out, mod)


def _tile_rows(ref, sub, c, n):
    return ref.at[pl.ds(sub * n * ROW_TILES + c, n, stride=ROW_TILES), :]


def _moe_ffn_kernel(te_ref, ns_ref, tok_ref, x_hbm, wg_ref, wu_ref, wo_ref, o_ref,
                    xbuf_ref, sem, h_ref, acc_ref, wgb_ref, wub_ref, wob_ref):
    i, j = pl.program_id(0), pl.program_id(1)
    nsub = ns_ref[i]
    last = j == pl.num_programs(1) - 1

    def gather(tile, wait):
        for s in range(N_SUB):
            @pl.when(s < ns_ref[tile])
            def _():
                def body(r, carry):
                    src = 0 if wait else tok_ref[tile * TM_MOE + s * SUB_MOE + r]
                    cp = _row_copy(x_hbm, src, xbuf_ref, s * SUB_MOE + r, sem)
                    cp.wait() if wait else cp.start()
                    return carry

                lax.fori_loop(0, SUB_MOE, body, 0, unroll=8)

    @pl.when((i == 0) & (j == 0))
    def _():
        gather(0, wait=False)

    @pl.when(j == 0)
    def _():
        gather(i, wait=True)
        for s in range(N_SUB):
            @pl.when(s < nsub)
            def _():
                for c in range(ROW_TILES):
                    h_ref[pl.ds(s * SUB_MOE, SUB_MOE), c * LANES:(c + 1) * LANES] = _tile_rows(
                        xbuf_ref, s, c, SUB_MOE)[...].astype(BF16)

        @pl.when(i + 1 < pl.num_programs(0))
        def _():
            gather(i + 1, wait=False)

    @pl.when(nsub > 0)
    def _():
        wgb_ref[...] = wg_ref[...].astype(BF16)
        wub_ref[...] = wu_ref[...].astype(BF16)
        wob_ref[...] = wo_ref[...].astype(BF16)

    for s in range(N_SUB):
        rows = pl.ds(s * SUB_MOE, SUB_MOE)

        @pl.when(s < nsub)
        def _():
            part = _swiglu_part(h_ref[rows, :], wgb_ref[...], wub_ref[...], wob_ref[...])

            @pl.when(j == 0)
            def _():
                acc_ref[rows, :] = part

            @pl.when(j > 0)
            def _():
                acc_ref[rows, :] += part

            @pl.when(last)
            def _():
                for c in range(ROW_TILES):
                    _tile_rows(o_ref, s, c, SUB_MOE)[...] = acc_ref[rows, c * LANES:(c + 1) * LANES]

        @pl.when((s >= nsub) & last)
        def _():
            o_ref[pl.ds(s * SUB_MOE * ROW_TILES, SUB_MOE * ROW_TILES), :] = jnp.zeros(
                (SUB_MOE * ROW_TILES, LANES), F32)


def _moe_ffn(x_tiles, row_tok, tile_expert, tile_nsub, w_in, w_out, widx):
    tm, tf = TM_MOE, TF_MOE
    nf = EXPERT_DIM // tf

    def jj(i, j, ns):
        return jnp.where(ns[i] > 0, j, nf - 1)

    grid_spec = pltpu.PrefetchScalarGridSpec(
        num_scalar_prefetch=3,
        grid=(N_MOE_TILES, nf),
        in_specs=[
            pl.BlockSpec(memory_space=pl.ANY),
            pl.BlockSpec((None, None, D_MODEL, tf), lambda i, j, te, ns, tok: (widx, te[i], 0, jj(i, j, ns))),
            pl.BlockSpec((None, None, D_MODEL, tf),
                         lambda i, j, te, ns, tok: (widx, te[i], 0, nf + jj(i, j, ns))),
            pl.BlockSpec((None, None, tf, D_MODEL), lambda i, j, te, ns, tok: (widx, te[i], jj(i, j, ns), 0)),
        ],
        out_specs=pl.BlockSpec((tm * ROW_TILES, LANES), lambda i, j, te, ns, tok: (i, 0)),
        scratch_shapes=[pltpu.VMEM((tm * ROW_TILES, LANES), F32), pltpu.SemaphoreType.DMA(()),
                        pltpu.VMEM((tm, D_MODEL), BF16), pltpu.VMEM((tm, D_MODEL), F32),
                        pltpu.VMEM((D_MODEL, tf), BF16), pltpu.VMEM((D_MODEL, tf), BF16),
                        pltpu.VMEM((tf, D_MODEL), BF16)],
    )
    return pl.pallas_call(
        _moe_ffn_kernel,
        grid_spec=grid_spec,
        out_shape=jax.ShapeDtypeStruct((N_PAD * ROW_TILES, LANES), F32),
        compiler_params=_params(("arbitrary", "arbitrary")),
        name="moe_ffn",
    )(tile_expert, tile_nsub, row_tok, x_tiles, w_in, w_in, w_out)


def _router_kernel(x_ref, g_ref, sh_ref, sc_ref, wr_ref, h_ref, info_ref, w_ref, cnt_ref, base_ref):
    tm = x_ref.shape[0]

    @pl.when(pl.program_id(0) == 0)
    def _():
        base_ref[...] = jnp.zeros_like(base_ref)

    h = _norm_mod(x_ref[...], g_ref[...], sh_ref[...], sc_ref[...])
    for c in range(ROW_TILES):
        h_ref[pl.ds(c, tm, stride=ROW_TILES), :] = h[:, c * LANES:(c + 1) * LANES]
    logits = jnp.dot(h, wr_ref[...], precision=HIGHEST, preferred_element_type=F32)
    lane = lax.broadcasted_iota(jnp.int32, logits.shape, 1)
    logits = jnp.where(lane < N_EXPERTS, logits, -jnp.inf)
    m1 = jnp.max(logits, axis=-1, keepdims=True)
    i1 = jnp.min(jnp.where(logits == m1, lane, LANES), axis=-1, keepdims=True)
    rest = jnp.where(lane == i1, -jnp.inf, logits)
    m2 = jnp.max(rest, axis=-1, keepdims=True)
    i2 = jnp.min(jnp.where(rest == m2, lane, LANES), axis=-1, keepdims=True)
    e2 = jnp.exp(m2 - m1)
    den = 1.0 + e2
    w_ref[...] = jnp.where(lane == 0, 1.0 / den, jnp.where(lane == 1, e2 / den, 0.0))
    chosen = jnp.where((lane == i1) | (lane == i2), 1.0, 0.0)
    rt = lax.broadcasted_iota(jnp.int32, (tm, tm), 0)
    ct = lax.broadcasted_iota(jnp.int32, (tm, tm), 1)
    earlier = jnp.where(ct < rt, 1.0, 0.0).astype(BF16)
    before = jnp.dot(earlier, chosen.astype(BF16), preferred_element_type=F32) + base_ref[0:1, :]
    r1 = jnp.sum(jnp.where(lane == i1, before, 0.0), axis=-1, keepdims=True).astype(jnp.int32)
    r2 = jnp.sum(jnp.where(lane == i2, before, 0.0), axis=-1, keepdims=True).astype(jnp.int32)
    info_ref[...] = jnp.where(lane == 0, i1, jnp.where(lane == 1, i2, jnp.where(
        lane == 2, r1, jnp.where(lane == 3, r2, 0))))
    total = base_ref[0:1, :] + jnp.sum(chosen, axis=0, keepdims=True)
    base_ref[...] = jnp.broadcast_to(total, base_ref.shape)
    cnt_ref[...] = jnp.broadcast_to(total, cnt_ref.shape).astype(jnp.int32)


def _router(x, gain, mod, layer, w_router):
    tm = TM_ROUTER
    wr = jnp.pad(w_router, ((0, 0), (0, LANES - N_EXPERTS)))

    def mspec(which):
        return pl.BlockSpec((None, None, None, 1, D_MODEL),
                            lambda i: (layer, which, _group_of_row(i * tm), 0, 0))

    return pl.pallas_call(
        _router_kernel,
        grid=(N_TOK // tm,),
        in_specs=[
            pl.BlockSpec((tm, D_MODEL), lambda i: (i, 0)),
            pl.BlockSpec((1, D_MODEL), lambda i: (0, 0)),
            mspec(3), mspec(4),
            pl.BlockSpec((D_MODEL, LANES), lambda i: (0, 0)),
        ],
        out_specs=[
            pl.BlockSpec((tm * ROW_TILES, LANES), lambda i: (i, 0)),
            pl.BlockSpec((tm, LANES), lambda i: (i, 0)),
            pl.BlockSpec((tm, LANES), lambda i: (i, 0)),
            pl.BlockSpec((SUBLANES, LANES), lambda i: (0, 0)),
        ],
        out_shape=[
            jax.ShapeDtypeStruct((N_TOK * ROW_TILES, LANES), F32),
            jax.ShapeDtypeStruct((N_TOK, LANES), jnp.int32),
            jax.ShapeDtypeStruct((N_TOK, LANES), F32),
            jax.ShapeDtypeStruct((SUBLANES, LANES), jnp.int32),
        ],
        scratch_shapes=[pltpu.VMEM((SUBLANES, LANES), F32)],
        compiler_params=_params(("arbitrary",)),
        name="router",
    )(x, gain, mod, mod, wr)


def _row_copy(src_hbm, row, dst_ref, r, sem):
    src = src_hbm.at[pl.ds(pl.multiple_of(row * ROW_TILES, ROW_TILES), ROW_TILES), :]
    dst = dst_ref.at[pl.ds(pl.multiple_of(r * ROW_TILES, ROW_TILES), ROW_TILES), :]
    return pltpu.make_async_copy(src, dst, sem)


def _combine_kernel(pos_ref, y_hbm, res_ref, gate_ref, w_ref, o_ref, a_ref, b_ref, sem):
    base = pl.program_id(0) * TG

    def issue(r, carry):
        _row_copy(y_hbm, pos_ref[2 * (base + r)], a_ref, r, sem.at[0]).start()
        _row_copy(y_hbm, pos_ref[2 * (base + r) + 1], b_ref, r, sem.at[1]).start()
        return carry

    lax.fori_loop(0, TG, issue, 0, unroll=8)

    def drain(r, carry):
        _row_copy(y_hbm, 0, a_ref, r, sem.at[0]).wait()
        _row_copy(y_hbm, 0, b_ref, r, sem.at[1]).wait()
        return carry

    lax.fori_loop(0, TG, drain, 0, unroll=8)
    w = w_ref[...]
    w0, w1 = w[:, 0:1], w[:, 1:2]
    for c in range(ROW_TILES):
        cols = slice(c * LANES, (c + 1) * LANES)
        rows = pl.ds(c, TG, stride=ROW_TILES)
        mix = w0 * a_ref[rows, :] + w1 * b_ref[rows, :]
        o_ref[:, cols] = res_ref[:, cols] + gate_ref[:, cols] * mix


def _combine(yb, pos, res, mod, layer, top_w):
    grid_spec = pltpu.PrefetchScalarGridSpec(
        num_scalar_prefetch=1,
        grid=(N_TOK // TG,),
        in_specs=[
            pl.BlockSpec(memory_space=pl.ANY),
            pl.BlockSpec((TG, D_MODEL), lambda i, p: (i, 0)),
            pl.BlockSpec((None, None, None, 1, D_MODEL),
                         lambda i, p: (layer, 5, _group_of_row(i * TG), 0, 0)),
            pl.BlockSpec((TG, LANES), lambda i, p: (i, 0)),
        ],
        out_specs=pl.BlockSpec((TG, D_MODEL), lambda i, p: (i, 0)),
        scratch_shapes=[pltpu.VMEM((TG * ROW_TILES, LANES), F32), pltpu.VMEM((TG * ROW_TILES, LANES), F32),
                        pltpu.SemaphoreType.DMA((2,))],
    )
    return pl.pallas_call(
        _combine_kernel,
        grid_spec=grid_spec,
        out_shape=jax.ShapeDtypeStruct((N_TOK, D_MODEL), F32),
        compiler_params=_params(("arbitrary",)),
        name="moe_combine",
    )(pos, yb, res, mod, top_w)


def _moe_layer(x, gain, mod, layer, w_router, w_in, w_out, widx):
    h, info, top_w, cnt = _router(x, gain, mod, layer, w_router)
    experts = jnp.arange(N_EXPERTS, dtype=jnp.int32)
    counts = cnt[0, :N_EXPERTS]
    padded = (counts + TM_MOE - 1) // TM_MOE * TM_MOE
    pad_end = jnp.cumsum(padded)
    pad_start = pad_end - padded
    e_sel = info[:, :TOP_K]
    start_sel = jnp.sum(jnp.where(e_sel[..., None] == experts, pad_start, 0), axis=-1)
    dest = (start_sel + info[:, TOP_K:2 * TOP_K]).reshape(-1).astype(jnp.int32)
    row_tok = jnp.zeros((N_PAD,), jnp.int32).at[dest].set(
        jnp.arange(N_SLOT, dtype=jnp.int32) // TOP_K, unique_indices=True)
    tile_start = jnp.arange(N_MOE_TILES, dtype=jnp.int32) * TM_MOE
    n_before = jnp.sum(pad_end[None, :] <= tile_start[:, None], axis=1)
    used = tile_start < pad_end[-1]
    last_expert = jnp.sum(pad_end < pad_end[-1])
    tile_expert = jnp.where(used, jnp.minimum(n_before, N_EXPERTS - 1), last_expert).astype(jnp.int32)
    seg_end = jnp.sum(jnp.where(tile_expert[:, None] == experts, pad_start + counts, 0), axis=-1)
    rows_used = jnp.clip(seg_end - tile_start, 0, TM_MOE)
    tile_nsub = jnp.where(used, (rows_used + SUB_MOE - 1) // SUB_MOE, 0).astype(jnp.int32)
    yb = _moe_ffn(h, row_tok, tile_expert, tile_nsub, w_in, w_out, widx)
    return _combine(yb, dest, x, mod, layer, top_w)


def _conv_kernel(x_ref, w_ref, b_ref, o_ref):
    x = x_ref[...]
    n = x.shape[0]
    t = lax.broadcasted_iota(jnp.int32, x.shape, 0)
    acc = b_ref[...] + w_ref[SSD_CONV // 2:SSD_CONV // 2 + 1, :] * x
    for k in range(SSD_CONV):
        s = k - SSD_CONV // 2
        if s == 0:
            continue
        xs = pltpu.roll(x, (-s) % n, axis=0)
        ok = (t + s >= 0) & (t + s < n)
        acc = acc + w_ref[k:k + 1, :] * jnp.where(ok, xs, 0.0)
    o_ref[...] = _silu(acc)


def _ssd_conv(zx, conv_w, conv_b, row0, n_seq, seq_len):
    tn = 512
    c0 = SSD_INNER // tn
    r0 = row0 // seq_len
    return pl.pallas_call(
        _conv_kernel,
        grid=(n_seq, SSD_CONV_DIM // tn),
        in_specs=[
            pl.BlockSpec((seq_len, tn), lambda b, j: (r0 + b, c0 + j)),
            pl.BlockSpec((SSD_CONV, tn), lambda b, j: (0, j)),
            pl.BlockSpec((1, tn), lambda b, j: (0, j)),
        ],
        out_specs=pl.BlockSpec((seq_len, tn), lambda b, j: (b, j)),
        out_shape=jax.ShapeDtypeStruct((n_seq * seq_len, SSD_CONV_DIM), F32),
        compiler_params=_params(("arbitrary", "arbitrary")),
        name="ssd_conv",
    )(zx, conv_w, conv_b[None])


def _softplus(x):
    return jnp.maximum(x, 0.0) + jnp.log(1.0 + jnp.exp(-jnp.abs(x)))


def _ssd_scan_kernel(*refs, reverse, has_h0, out_state):
    refs = list(refs)
    xbc_ref, dt_ref, dtb_ref, alog_ref = refs[:4]
    pos = 4
    h0_ref = None
    if has_h0:
        h0_ref = refs[pos]
        pos += 1
    y_ref = refs[pos]
    pos += 1
    sf_ref = None
    if out_state:
        sf_ref = refs[pos]
        pos += 1
    s_ref = refs[pos]
    c = pl.program_id(1)
    q = SSD_CHUNK

    @pl.when(c == 0)
    def _():
        if has_h0:
            s_ref[...] = h0_ref[...]
        else:
            s_ref[...] = jnp.zeros_like(s_ref)

    dt = _softplus(dt_ref[...] + dtb_ref[...])
    da = dt * (-jnp.exp(alog_ref[...]))
    ri = lax.broadcasted_iota(jnp.int32, (q, q), 0)
    ci = lax.broadcasted_iota(jnp.int32, (q, q), 1)
    reach = (ri <= ci) if reverse else (ri >= ci)
    cs = jnp.dot(jnp.where(reach, 1.0, 0.0), da, precision=HIGHEST, preferred_element_type=F32)
    cs_t = cs.T
    cs_end = cs[0:1, :] if reverse else cs[q - 1:q, :]
    dec = jnp.exp(cs_end)
    dt_t = dt.T
    ecs_t = jnp.exp(cs).T
    w_t = (dt * jnp.exp(cs_end - cs)).T
    x_t = xbc_ref[:, 0:SSD_INNER].T
    col0 = SSD_HEADS if reverse else 0
    gn = SSD_GROUPS * SSD_STATE
    ys, states = [], []
    for g in range(SSD_GROUPS):
        bg = xbc_ref[:, SSD_INNER + g * SSD_STATE:SSD_INNER + (g + 1) * SSD_STATE].astype(BF16)
        cg = xbc_ref[:, SSD_INNER + gn + g * SSD_STATE:SSD_INNER + gn + (g + 1) * SSD_STATE].astype(BF16)
        cb = _nt_dot(cg, bg)
        for e in range(HEADS_PER_GROUP):
            h = g * HEADS_PER_GROUP + e
            col = col0 + h
            hs = slice(h * SSD_HEADDIM, (h + 1) * SSD_HEADDIM)
            xh_t = x_t[hs, :]
            seg = cs[:, col:col + 1] - cs_t[col:col + 1, :]
            decay = jnp.exp(jnp.where(reach, seg, -jnp.inf))
            state = s_ref[hs, :]
            y_diag_t = _nt_dot((xh_t * dt_t[col:col + 1, :]).astype(BF16), (cb * decay).astype(BF16))
            y_off_t = _nt_dot(state.astype(BF16), cg) * ecs_t[col:col + 1, :]
            ys.append(y_diag_t + y_off_t)
            xw_t = (xh_t * w_t[col:col + 1, :]).astype(BF16)
            states.append(dec[0:1, col:col + 1] * state + jnp.dot(xw_t, bg, preferred_element_type=F32))
    y_ref[...] = jnp.concatenate(ys, axis=0).T
    new_state = jnp.concatenate(states, axis=0)
    s_ref[...] = new_state

    if out_state:
        @pl.when(c == pl.num_programs(1) - 1)
        def _():
            sf_ref[...] = new_state


def _ssd_scan(xbc, dt_all, dt_bias, a_log, row0, n_seq, seq_len, reverse, h0=None, out_state=False):
    nc = seq_len // SSD_CHUNK
    c0 = row0 // SSD_CHUNK

    def chunk(c):
        return nc - 1 - c if reverse else c

    in_specs = [
        pl.BlockSpec((SSD_CHUNK, SSD_CONV_DIM), lambda b, c: (b * nc + chunk(c), 0)),
        pl.BlockSpec((SSD_CHUNK, LANES), lambda b, c: (c0 + b * nc + chunk(c), 0)),
        pl.BlockSpec((1, LANES), lambda b, c: (0, 0)),
        pl.BlockSpec((1, LANES), lambda b, c: (0, 0)),
    ]
    args = [xbc, dt_all, dt_bias, a_log]
    if h0 is not None:
        in_specs.append(pl.BlockSpec((None, SSD_INNER, SSD_STATE), lambda b, c: (b, 0, 0)))
        args.append(h0)
    out_specs = [pl.BlockSpec((SSD_CHUNK, SSD_INNER), lambda b, c: (b * nc + chunk(c), 0))]
    out_shape = [jax.ShapeDtypeStruct((n_seq * seq_len, SSD_INNER), F32)]
    if out_state:
        out_specs.append(pl.BlockSpec((None, SSD_INNER, SSD_STATE), lambda b, c: (b, 0, 0)))
        out_shape.append(jax.ShapeDtypeStruct((n_seq, SSD_INNER, SSD_STATE), F32))
    return pl.pallas_call(
        functools.partial(_ssd_scan_kernel, reverse=reverse, has_h0=h0 is not None, out_state=out_state),
        grid=(n_seq, nc),
        in_specs=in_specs,
        out_specs=out_specs,
        out_shape=out_shape,
        scratch_shapes=[pltpu.VMEM((SSD_INNER, SSD_STATE), F32)],
        compiler_params=_params(("arbitrary", "arbitrary")),
        name="ssd_scan",
    )(*args)


def _ssd_out_kernel(yf_ref, yb_ref, x_ref, z_ref, d_ref, nw_ref, w_ref, r_ref, g_ref, o_ref):
    y = yf_ref[...] + yb_ref[...] + d_ref[...] * x_ref[...]
    y = y * _silu(z_ref[...])
    gw = SSD_INNER // SSD_GROUPS
    parts = []
    for g in range(SSD_GROUPS):
        yg = y[:, g * gw:(g + 1) * gw]
        yg = yg * lax.rsqrt(jnp.mean(yg * yg, axis=-1, keepdims=True) + NORM_EPS)
        parts.append((yg * nw_ref[:, g * gw:(g + 1) * gw]).astype(BF16))
    yn = jnp.concatenate(parts, axis=1)
    acc = jnp.dot(yn, w_ref[...].astype(BF16), preferred_element_type=F32)
    o_ref[...] = r_ref[...] + g_ref[...] * acc


def _ssd_out(yf, yb, xbc, zx, d_exp, norm_w, w_out, widx, res, mod, layer, row0, n_rows):
    tm = TM_SSD_OUT
    r0 = row0 // tm
    return pl.pallas_call(
        _ssd_out_kernel,
        grid=(n_rows // tm,),
        in_specs=[
            pl.BlockSpec((tm, SSD_INNER), lambda i: (i, 0)),
            pl.BlockSpec((tm, SSD_INNER), lambda i: (i, 0)),
            pl.BlockSpec((tm, SSD_INNER), lambda i: (i, 0)),
            pl.BlockSpec((tm, SSD_INNER), lambda i: (r0 + i, 0)),
            pl.BlockSpec((1, SSD_INNER), lambda i: (0, 0)),
            pl.BlockSpec((1, SSD_INNER), lambda i: (0, 0)),
            pl.BlockSpec((None, SSD_INNER, D_MODEL), lambda i: (widx, 0, 0)),
            pl.BlockSpec((tm, D_MODEL), lambda i: (r0 + i, 0)),
            pl.BlockSpec((None, None, None, 1, D_MODEL),
                         lambda i: (layer, 2, _group_of_row(row0 + i * tm), 0, 0)),
        ],
        out_specs=pl.BlockSpec((tm, D_MODEL), lambda i: (i, 0)),
        out_shape=jax.ShapeDtypeStruct((n_rows, D_MODEL), F32),
        compiler_params=_params(("arbitrary",)),
        name="ssd_out",
    )(yf, yb, xbc, zx, d_exp, norm_w, w_out, res, mod)


def _ssd_layer(x, gain, mod, layer, j, state_f, state_b, w_in, conv_w, conv_b, dt_bias, a_log, d_skip,
               norm_w, w_out):
    n_zx = SSD_INNER + SSD_CONV_DIM
    zx = _nm_matmul(x, gain, mod, layer, w_in, j, n_zx, name="ssd_in_proj")
    pad = LANES - 2 * SSD_HEADS
    w_dt = jnp.pad(w_in[j][:, n_zx:], ((0, 0), (0, pad)))[None]
    dt_all = _nm_matmul(x, gain, mod, layer, w_dt, 0, LANES, tn=LANES, name="ssd_dt_proj")
    dtb = jnp.pad(dt_bias.reshape(1, -1), ((0, 0), (0, pad)))
    alog = jnp.pad(a_log.reshape(1, -1), ((0, 0), (0, pad)))
    d_exp = jnp.repeat(d_skip, SSD_HEADDIM)[None]
    outs, states = [], []
    for row0, n_seq, seq_len, h0s in ((0, BATCH, SEQ, None), (N_PROMPT, DEC_BATCH, DEC_SEQ, (state_f, state_b))):
        xbc = _ssd_conv(zx, conv_w, conv_b, row0, n_seq, seq_len)
        ys = []
        for reverse in (False, True):
            h0 = None if h0s is None else h0s[int(reverse)].reshape(n_seq, SSD_INNER, SSD_STATE)
            res = _ssd_scan(xbc, dt_all, dtb, alog, row0, n_seq, seq_len, reverse, h0=h0,
                            out_state=h0s is None)
            ys.append(res[0])
            if h0s is None:
                states.append(res[1])
        outs.append(_ssd_out(ys[0], ys[1], xbc, zx, d_exp, norm_w[None], w_out, j, x, mod, layer,
                             row0, n_seq * seq_len))
    shape = (BATCH, SSD_HEADS, SSD_HEADDIM, SSD_STATE)
    return tuple(outs), states[0].reshape(shape), states[1].reshape(shape)


def _na_layer(x, gain, mod, layer, j, cache_k, cache_v, w_qkv, w_o, q_norm, k_norm, rpb):
    w = NA_HEADS * HEAD_DIM
    qp, kp, vp = _qkv_proj(x, gain, mod, layer, w_qkv, j, q_norm, k_norm, w, 0, N_PROMPT, F32)
    qs, ks, vs = _qkv_proj(x, gain, mod, layer, w_qkv, j, q_norm, k_norm, w, N_PROMPT, N_SAMPLE, BF16)
    op = _ctx_attn(qp, kp, vp)
    ck = cache_k.reshape(DEC_BATCH, PAST_LEN, w).astype(BF16)
    cv = cache_v.reshape(DEC_BATCH, PAST_LEN, w).astype(BF16)
    os_ = _na_latent(qs, ks, vs, ck, cv, _na_bias_table(rpb))
    shape = (BATCH, SEQ, NA_HEADS, HEAD_DIM)
    return _linear_res((op, os_), w_o, j, x, mod, layer, 2), kp.reshape(shape), vp.reshape(shape)


def _swa_layer(x, gain, mod, layer, j, cache_k, cache_v, w_qkv, w_o, q_norm, k_norm, sink):
    kw = SWA_KV_HEADS * HEAD_DIM
    qp, kp, vp = _qkv_proj(x, gain, mod, layer, w_qkv, j, q_norm, k_norm, kw, 0, N_PROMPT, F32)
    qs, ks, vs = _qkv_proj(x, gain, mod, layer, w_qkv, j, q_norm, k_norm, kw, N_PROMPT, N_SAMPLE, BF16,
                           rope_tabs=_rope_tables())
    sink = sink.astype(F32)
    op = _ctx_attn(qp, kp, vp, sink)
    ck = cache_k.reshape(DEC_BATCH, PAST_LEN, kw).astype(BF16)
    cv = cache_v.reshape(DEC_BATCH, PAST_LEN, kw).astype(BF16)
    os_ = _swa_latent(qs, ks, vs, ck, cv, sink)
    shape = (BATCH, SEQ, SWA_KV_HEADS, HEAD_DIM)
    return _linear_res((op, os_), w_o, j, x, mod, layer, 2), kp.reshape(shape), vp.reshape(shape)


def kernel(x_prompt, x_sample, cache_na_k, cache_na_v, cache_swa_k, cache_swa_v, state_ssd_fwd, state_ssd_bwd, c, c_ctx, ada_w, ada_b, norm_mix, norm_ffn, na_w_qkv, na_w_o, na_q_norm, na_k_norm, na_rpb, swa_w_qkv, swa_w_o, swa_q_norm, swa_k_norm, swa_sink, ssd_w_in, ssd_conv_w, ssd_conv_b, ssd_dt_bias, ssd_a_log, ssd_d, ssd_norm, ssd_w_out, ffn_w_in, ffn_w_out, moe_router, moe_w_in, moe_w_out):
    x = (x_prompt.reshape(N_PROMPT, D_MODEL), x_sample.reshape(N_SAMPLE, D_MODEL))
    cond = jnp.concatenate([c_ctx[None], c, jnp.zeros((N_GROUPS - 1 - DEC_BATCH, D_MODEL), F32)], axis=0)
    mod = _adaln(cond, ada_w, ada_b)
    ffn_in_bf, ffn_out_bf = ffn_w_in.astype(BF16), ffn_w_out.astype(BF16)
    na_k, na_v, swa_k, swa_v, ssd_f, ssd_b = [], [], [], [], [], []
    for i in range(DEPTH):
        kind, j = i % N_MIXERS, i // N_MIXERS
        g_mix, g_ffn = norm_mix[i][None], norm_ffn[i][None]
        if kind == 0:
            x, kc, vc = _na_layer(x, g_mix, mod, i, j, cache_na_k[:, j], cache_na_v[:, j], na_w_qkv,
                                  na_w_o, na_q_norm[j], na_k_norm[j], na_rpb[j])
            na_k.append(kc)
            na_v.append(vc)
        elif kind == 1:
            x, kc, vc = _swa_layer(x, g_mix, mod, i, j, cache_swa_k[:, j], cache_swa_v[:, j], swa_w_qkv,
                                   swa_w_o, swa_q_norm[j], swa_k_norm[j], swa_sink[j])
            swa_k.append(kc)
            swa_v.append(vc)
        else:
            x, sf, sb = _ssd_layer(x, g_mix, mod, i, j, state_ssd_fwd[:, j], state_ssd_bwd[:, j], ssd_w_in,
                                   ssd_conv_w[j], ssd_conv_b[j], ssd_dt_bias[j], ssd_a_log[j], ssd_d[j],
                                   ssd_norm[j], ssd_w_out)
            ssd_f.append(sf)
            ssd_b.append(sb)
        if i % 2 == 0:
            x = _dense_ffn(x, g_ffn, mod, i, ffn_in_bf, ffn_out_bf, i // 2)
        else:
            x = _moe_layer(x, g_ffn, mod, i, moe_router[i // 2], moe_w_in, moe_w_out, i // 2)
    yp = x[:N_PROMPT].reshape(BATCH, SEQ, D_MODEL)
    ys = x[N_PROMPT:].reshape(DEC_BATCH, DEC_SEQ, D_MODEL)
    return (yp, ys, jnp.stack(na_k, axis=1), jnp.stack(na_v, axis=1), jnp.stack(swa_k, axis=1),
            jnp.stack(swa_v, axis=1), jnp.stack(ssd_f, axis=1), jnp.stack(ssd_b, axis=1))
```

```python
import functools

import jax
import jax.numpy as jnp
from jax import lax
from jax.experimental import pallas as pl
from jax.experimental.pallas import tpu as pltpu

F32 = jnp.float32
BF16 = jnp.bfloat16
HIGHEST = lax.Precision.HIGHEST

D_MODEL = 1024
BATCH = 32
SEQ = 256
DEPTH = 4
DEC_BATCH = 4
DEC_SEQ = 1024
PAST_LEN = 256
GRID_W = 64
N_MIXERS = 3
HEAD_DIM = 64
NORM_EPS = 1e-6
ROPE_BASE = 10000.0
NA_HEADS = 16
NA_ROWS = 8
NA_COLS = 16
SWA_Q_HEADS = 16
SWA_KV_HEADS = 4
SWA_GROUP = SWA_Q_HEADS // SWA_KV_HEADS
SWA_WINDOW = 128
SWA_BLOCK = 128
SSD_INNER = 2 * D_MODEL
SSD_HEADDIM = 64
SSD_HEADS = SSD_INNER // SSD_HEADDIM
SSD_GROUPS = 4
SSD_STATE = 128
SSD_CONV = 5
SSD_CHUNK = 128
SSD_CONV_DIM = SSD_INNER + 2 * SSD_GROUPS * SSD_STATE
FFN_DIM = 2816
N_EXPERTS = 8
TOP_K = 2
EXPERT_DIM = 3584

N_PROMPT = BATCH * SEQ
N_SAMPLE = DEC_BATCH * DEC_SEQ
N_TOK = N_PROMPT + N_SAMPLE
N_GROUPS = 8
LANES = 128
SUBLANES = 8
MXU_DIM = 256
GRID_ROWS = DEC_SEQ // GRID_W
HEADS_PER_GROUP = SSD_HEADS // SSD_GROUPS
ROW_TILES = D_MODEL // LANES

VMEM_LIMIT = 56 * 1024 * 1024
TM_LIN = 1024
TN_LIN = 512
TM_FFN = 512
TF_DENSE = 1408
TF_MOE = 512
TM_MOE = 2048
SUB_MOE = 512
N_SUB = TM_MOE // SUB_MOE
N_SLOT = N_TOK * TOP_K
N_PAD = N_SLOT + N_EXPERTS * TM_MOE
N_MOE_TILES = N_PAD // TM_MOE
TG = 512
TM_ROUTER = 512
TM_SSD_OUT = 256
assert ROW_TILES == SUBLANES


def _params(sem):
    return pltpu.CompilerParams(dimension_semantics=sem, vmem_limit_bytes=VMEM_LIMIT)


def _group_of_row(start):
    return jnp.where(start < N_PROMPT, 0, 1 + (start - N_PROMPT) // DEC_SEQ)


def _silu(x):
    return x / (1.0 + jnp.exp(-x))


def _norm_mod(x, gain, shift, scale):
    y = x * lax.rsqrt(jnp.mean(x * x, axis=-1, keepdims=True) + NORM_EPS)
    return (y * gain) * (1.0 + scale) + shift


def _adaln_kernel(c_ref, w_ref, b_ref, o_ref):
    s = _silu(c_ref[...])
    o_ref[...] = jnp.dot(s, w_ref[...], precision=HIGHEST, preferred_element_type=F32) + b_ref[...]


def _adaln(cond, ada_w, ada_b):
    tn = 1024
    out = pl.pallas_call(
        _adaln_kernel,
        grid=(DEPTH, 6 * D_MODEL // tn),
        in_specs=[
            pl.BlockSpec((N_GROUPS, D_MODEL), lambda l, j: (0, 0)),
            pl.BlockSpec((None, D_MODEL, tn), lambda l, j: (l, 0, j)),
            pl.BlockSpec((None, 1, tn), lambda l, j: (l, 0, j)),
        ],
        out_specs=pl.BlockSpec((None, N_GROUPS, tn), lambda l, j: (l, 0, j)),
        out_shape=jax.ShapeDtypeStruct((DEPTH, N_GROUPS, 6 * D_MODEL), F32),
        compiler_params=_params(("arbitrary", "arbitrary")),
        name="adaln",
    )(cond, ada_w, ada_b.reshape(DEPTH, 1, 6 * D_MODEL))
    out = out.reshape(DEPTH, N_GROUPS, 6, D_MODEL)
    return jnp.transpose(out, (0, 2, 1, 3)).reshape(DEPTH, 6, N_GROUPS, 1, D_MODEL)


def _nm_matmul_kernel(x_ref, g_ref, sh_ref, sc_ref, w_ref, o_ref, h_ref):
    @pl.when(pl.program_id(1) == 0)
    def _():
        h_ref[...] = _norm_mod(x_ref[...], g_ref[...], sh_ref[...], sc_ref[...]).astype(BF16)

    o_ref[...] = jnp.dot(h_ref[...], w_ref[...].astype(BF16), preferred_element_type=F32)


def _nm_matmul(x, gain, mod, layer, w, widx, n_out, tn=TN_LIN, name="nm_matmul"):
    tm = TM_LIN
    return pl.pallas_call(
        _nm_matmul_kernel,
        grid=(N_TOK // tm, n_out // tn),
        in_specs=[
            pl.BlockSpec((tm, D_MODEL), lambda i, j: (i, 0)),
            pl.BlockSpec((1, D_MODEL), lambda i, j: (0, 0)),
            pl.BlockSpec((None, None, None, 1, D_MODEL),
                         lambda i, j: (layer, 0, _group_of_row(i * tm), 0, 0)),
            pl.BlockSpec((None, None, None, 1, D_MODEL),
                         lambda i, j: (layer, 1, _group_of_row(i * tm), 0, 0)),
            pl.BlockSpec((None, D_MODEL, tn), lambda i, j: (widx, 0, j)),
        ],
        out_specs=pl.BlockSpec((tm, tn), lambda i, j: (i, j)),
        out_shape=jax.ShapeDtypeStruct((N_TOK, n_out), F32),
        scratch_shapes=[pltpu.VMEM((tm, D_MODEL), BF16)],
        compiler_params=_params(("arbitrary", "arbitrary")),
        name=name,
    )(x, gain, mod, mod, w)


def _split_stream(arr, tm, n_col):
    a, b = arr if isinstance(arr, tuple) else (arr, arr)
    n_p = N_PROMPT // tm
    off = 0 if isinstance(arr, tuple) else n_p
    blk = (tm, a.shape[1] // n_col)

    def col(own, j, parked):
        return jnp.where(own, j, parked) if n_col > 1 else 0

    spec_a = pl.BlockSpec(blk, lambda i, j: (jnp.minimum(i, n_p - 1), col(i < n_p, j, n_col - 1)))
    spec_b = pl.BlockSpec(blk, lambda i, j: (off + jnp.maximum(i - n_p, 0), col(i >= n_p, j, 0)))
    return (a, b), (spec_a, spec_b)


def _pick_half(a_ref, b_ref, tm):
    return jnp.where(pl.program_id(0) < N_PROMPT // tm, a_ref[...], b_ref[...])


def _linear_res_kernel(xp_ref, xs_ref, w_ref, rp_ref, rs_ref, g_ref, o_ref):
    tm = o_ref.shape[0]
    acc = jnp.dot(_pick_half(xp_ref, xs_ref, tm), w_ref[...].astype(BF16), preferred_element_type=F32)
    o_ref[...] = _pick_half(rp_ref, rs_ref, tm) + g_ref[...] * acc


def _linear_res(x, w, widx, res, mod, layer, which):
    tm, tn = TM_LIN, TN_LIN
    n_col = D_MODEL // tn
    x_arrs, x_specs = _split_stream(x, tm, 1)
    r_arrs, r_specs = _split_stream(res, tm, n_col)
    k = x_arrs[0].shape[1]
    return pl.pallas_call(
        _linear_res_kernel,
        grid=(N_TOK // tm, n_col),
        in_specs=[
            *x_specs,
            pl.BlockSpec((None, k, tn), lambda i, j: (widx, 0, j)),
            *r_specs,
            pl.BlockSpec((None, None, None, 1, tn),
                         lambda i, j: (layer, which, _group_of_row(i * tm), 0, j)),
        ],
        out_specs=pl.BlockSpec((tm, tn), lambda i, j: (i, j)),
        out_shape=jax.ShapeDtypeStruct((N_TOK, D_MODEL), F32),
        compiler_params=_params(("arbitrary", "arbitrary")),
        name="linear_res",
    )(*x_arrs, w, *r_arrs, mod)


def _group_sumsq(x):
    r = lax.broadcasted_iota(jnp.int32, (MXU_DIM, MXU_DIM), 0) // HEAD_DIM
    c = lax.broadcasted_iota(jnp.int32, (MXU_DIM, MXU_DIM), 1) // HEAD_DIM
    ones = jnp.where(r == c, 1.0, 0.0).astype(BF16)
    outs = []
    for t in range(x.shape[1] // MXU_DIM):
        x2 = x[:, t * MXU_DIM:(t + 1) * MXU_DIM]
        x2 = x2 * x2
        hi = x2.astype(BF16)
        lo = (x2 - hi.astype(F32)).astype(BF16)
        outs.append(jnp.dot(hi, ones, preferred_element_type=F32)
                    + jnp.dot(lo, ones, preferred_element_type=F32))
    return jnp.concatenate(outs, axis=1) if len(outs) > 1 else outs[0]


def _head_norm(x, gain):
    return x * lax.rsqrt(_group_sumsq(x) * (1.0 / HEAD_DIM) + NORM_EPS) * gain


def _rope(x, cos, sin):
    w = x.shape[1]
    lane = lax.broadcasted_iota(jnp.int32, x.shape, 1)
    partner = jnp.where((lane % 32) < 16, pltpu.roll(x, w - 16, axis=1), pltpu.roll(x, 16, axis=1))
    return x * cos + partner * sin


def _qkv_kernel(*refs, nq, rope):
    if rope:
        (x_ref, g_ref, sh_ref, sc_ref, w_ref, qg_ref, kg_ref, cos_ref, sin_ref,
         q_ref, k_ref, v_ref, h_ref) = refs
    else:
        x_ref, g_ref, sh_ref, sc_ref, w_ref, qg_ref, kg_ref, q_ref, k_ref, v_ref, h_ref = refs
    j = pl.program_id(1)

    @pl.when(j == 0)
    def _():
        h_ref[...] = _norm_mod(x_ref[...], g_ref[...], sh_ref[...], sc_ref[...]).astype(BF16)

    acc = jnp.dot(h_ref[...], w_ref[...].astype(BF16), preferred_element_type=F32)

    def normed(gain_ref):
        y = _head_norm(acc, gain_ref[...])
        return _rope(y, cos_ref[...], sin_ref[...]) if rope else y

    @pl.when(j < nq)
    def _():
        q_ref[...] = (normed(qg_ref) * (HEAD_DIM ** -0.5)).astype(q_ref.dtype)

    @pl.when(j == nq)
    def _():
        k_ref[...] = normed(kg_ref).astype(k_ref.dtype)

    @pl.when(j == nq + 1)
    def _():
        v_ref[...] = acc.astype(v_ref.dtype)


def _qkv_proj(x, gain, mod, layer, w, widx, q_gain, k_gain, kw, row0, n_rows, kv_dtype, rope_tabs=None):
    tm = TM_LIN
    qw = NA_HEADS * HEAD_DIM
    nq = qw // kw
    r0 = row0 // tm
    if isinstance(x, tuple):
        x, r0 = (x[0], 0) if row0 == 0 else (x[1], 0)
    in_specs = [
        pl.BlockSpec((tm, D_MODEL), lambda i, j: (r0 + i, 0)),
        pl.BlockSpec((1, D_MODEL), lambda i, j: (0, 0)),
        pl.BlockSpec((None, None, None, 1, D_MODEL),
                     lambda i, j: (layer, 0, _group_of_row(row0 + i * tm), 0, 0)),
        pl.BlockSpec((None, None, None, 1, D_MODEL),
                     lambda i, j: (layer, 1, _group_of_row(row0 + i * tm), 0, 0)),
        pl.BlockSpec((None, D_MODEL, kw), lambda i, j: (widx, 0, j)),
        pl.BlockSpec((1, kw), lambda i, j: (0, 0)),
        pl.BlockSpec((1, kw), lambda i, j: (0, 0)),
    ]
    args = [x, gain, mod, mod, w, jnp.tile(q_gain, kw // HEAD_DIM)[None], jnp.tile(k_gain, kw // HEAD_DIM)[None]]
    if rope_tabs is not None:
        in_specs += [pl.BlockSpec((tm, kw), lambda i, j: (0, 0))] * 2
        args += [t[:, :kw] for t in rope_tabs]
    return pl.pallas_call(
        functools.partial(_qkv_kernel, nq=nq, rope=rope_tabs is not None),
        grid=(n_rows // tm, nq + 2),
        in_specs=in_specs,
        out_specs=[
            pl.BlockSpec((tm, kw), lambda i, j: (i, jnp.minimum(j, nq - 1))),
            pl.BlockSpec((tm, kw), lambda i, j: (i, 0)),
            pl.BlockSpec((tm, kw), lambda i, j: (i, 0)),
        ],
        out_shape=[
            jax.ShapeDtypeStruct((n_rows, qw), BF16),
            jax.ShapeDtypeStruct((n_rows, kw), kv_dtype),
            jax.ShapeDtypeStruct((n_rows, kw), kv_dtype),
        ],
        scratch_shapes=[pltpu.VMEM((tm, D_MODEL), BF16)],
        compiler_params=_params(("arbitrary", "arbitrary")),
        name="qkv_proj",
    )(*args)


def _rope_tables():
    quarter = HEAD_DIM // 4
    t = jnp.arange(DEC_SEQ)
    pos = jnp.stack([t // GRID_W, t % GRID_W], axis=-1).astype(F32)
    inv = ROPE_BASE ** (-jnp.arange(quarter, dtype=F32) / quarter)
    ang = pos[:, :, None] * inv
    cos, sin = jnp.cos(ang), jnp.sin(ang)
    cos64 = jnp.concatenate([cos[:, 0], cos[:, 0], cos[:, 1], cos[:, 1]], axis=1)
    sin64 = jnp.concatenate([-sin[:, 0], sin[:, 0], -sin[:, 1], sin[:, 1]], axis=1)
    return jnp.tile(cos64, (1, SWA_KV_HEADS)), jnp.tile(sin64, (1, SWA_KV_HEADS))


def _nt_dot(a, b):
    return lax.dot_general(a, b, (((1,), (1,)), ((), ())), preferred_element_type=F32)


def _tn_dot(a, b):
    return lax.dot_general(a, b, (((0,), (0,)), ((), ())), preferred_element_type=F32)


def _pair_queries(q):
    n = q.shape[0]
    ri = lax.broadcasted_iota(jnp.int32, (2 * n, LANES), 0)
    ci = lax.broadcasted_iota(jnp.int32, (2 * n, LANES), 1)
    return jnp.where((ri // n) == (ci // HEAD_DIM), jnp.concatenate([q, q], axis=0), jnp.zeros((), BF16))


def _pair_outputs(o2, l):
    n = o2.shape[0] // 2
    o2 = o2 / jnp.broadcast_to(l, (LANES, 2 * n)).T
    first = lax.broadcasted_iota(jnp.int32, (n, LANES), 1) < HEAD_DIM
    return jnp.where(first, o2[:n], o2[n:])


def _ctx_attn_pairs(q_ref, k_ref, v_ref, o_ref):
    outs = []
    for p in range(q_ref.shape[1] // LANES):
        sl = slice(p * LANES, (p + 1) * LANES)
        s = _nt_dot(k_ref[:, sl].astype(BF16), _pair_queries(q_ref[:, sl]))
        e = jnp.exp(s - jnp.max(s, axis=0, keepdims=True))
        l = jnp.sum(e, axis=0, keepdims=True)
        outs.append(_pair_outputs(_tn_dot(e.astype(BF16), v_ref[:, sl].astype(BF16)), l))
    o_ref[...] = jnp.concatenate(outs, axis=1).astype(o_ref.dtype)


def _stack_heads(q_ref, heads):
    parts = [q_ref[:, h * HEAD_DIM:(h + 1) * HEAD_DIM] for h in heads]
    return jnp.concatenate(parts, axis=0) if len(parts) > 1 else parts[0]


def _per_head_column(scalars, rows):
    gid = lax.broadcasted_iota(jnp.int32, (len(scalars) * rows, 1), 0) // rows
    col = jnp.full(gid.shape, scalars[-1], F32)
    for g in range(len(scalars) - 1):
        col = jnp.where(gid == g, scalars[g], col)
    return col


def _ctx_attn_kernel(*refs, group, use_sink):
    if use_sink:
        q_ref, k_ref, v_ref, sink_ref, o_ref = refs
    else:
        q_ref, k_ref, v_ref, o_ref = refs
    rows = q_ref.shape[0]
    if group == 1:
        _ctx_attn_pairs(q_ref, k_ref, v_ref, o_ref)
        return
    outs = []
    for kh in range(k_ref.shape[1] // HEAD_DIM):
        heads = range(kh * group, (kh + 1) * group)
        q = _stack_heads(q_ref, heads)
        k = k_ref[:, kh * HEAD_DIM:(kh + 1) * HEAD_DIM].astype(BF16)
        v = v_ref[:, kh * HEAD_DIM:(kh + 1) * HEAD_DIM].astype(BF16)
        s = _nt_dot(q, k)
        m = jnp.max(s, axis=-1, keepdims=True)
        if use_sink:
            sink = _per_head_column([sink_ref[h] for h in heads], rows)
            m = jnp.maximum(m, sink)
        e = jnp.exp(s - m)
        l = jnp.sum(e, axis=-1, keepdims=True)
        if use_sink:
            l = l + jnp.exp(sink - m)
        o = jnp.dot(e.astype(BF16), v, preferred_element_type=F32) / l
        outs += [o[g * rows:(g + 1) * rows] for g in range(group)]
    o_ref[...] = jnp.concatenate(outs, axis=1).astype(o_ref.dtype)


def _ctx_attn(q, k, v, sink=None):
    kw = k.shape[1]
    group = q.shape[1] // kw
    in_specs = [
        pl.BlockSpec((SEQ, q.shape[1]), lambda b: (b, 0)),
        pl.BlockSpec((SEQ, kw), lambda b: (b, 0)),
        pl.BlockSpec((SEQ, kw), lambda b: (b, 0)),
    ]
    args = [q, k, v]
    if sink is not None:
        in_specs.append(pl.BlockSpec(memory_space=pltpu.SMEM))
        args.append(sink)
    return pl.pallas_call(
        functools.partial(_ctx_attn_kernel, group=group, use_sink=sink is not None),
        grid=(BATCH,),
        in_specs=in_specs,
        out_specs=pl.BlockSpec((SEQ, q.shape[1]), lambda b: (b, 0)),
        out_shape=jax.ShapeDtypeStruct((N_PROMPT, q.shape[1]), BF16),
        compiler_params=_params(("arbitrary",)),
        name="ctx_attn",
    )(*args)


def _na_latent_kernel(q_ref, k_ref, v_ref, ck_ref, cv_ref, bias_ref, o_ref):
    r = pl.program_id(1)
    kr = NA_ROWS
    start = jnp.clip(r - kr // 2, 0, GRID_ROWS - kr)
    row0 = pl.multiple_of(start * GRID_W, GRID_W)
    win = pl.ds(row0, kr * GRID_W)
    outs = []
    for p in range(NA_HEADS // 2):
        sl = slice(p * LANES, (p + 1) * LANES)
        qd = _pair_queries(q_ref[:, sl])
        bias = bias_ref[p, pl.ds(start - r + kr - 1, kr)].reshape(kr * GRID_W, LANES)
        s_nb = _nt_dot(k_ref[win, sl], qd) + bias
        s_cx = _nt_dot(ck_ref[:, sl], qd)
        m = jnp.maximum(jnp.max(s_nb, axis=0, keepdims=True), jnp.max(s_cx, axis=0, keepdims=True))
        e_nb = jnp.exp(s_nb - m)
        e_cx = jnp.exp(s_cx - m)
        l = jnp.sum(e_nb, axis=0, keepdims=True) + jnp.sum(e_cx, axis=0, keepdims=True)
        o2 = _tn_dot(e_nb.astype(BF16), v_ref[win, sl]) + _tn_dot(e_cx.astype(BF16), cv_ref[:, sl])
        outs.append(_pair_outputs(o2, l))
    o_ref[...] = jnp.concatenate(outs, axis=1).astype(o_ref.dtype)


def _na_bias_table(rpb):
    col = jnp.arange(GRID_W)
    col_start = jnp.clip(col - NA_COLS // 2, 0, GRID_W - NA_COLS)
    col_ok = (col[None, :] >= col_start[:, None]) & (col[None, :] < col_start[:, None] + NA_COLS)
    dc = jnp.clip(col[None, :] - col[:, None], 1 - NA_COLS, NA_COLS - 1) + NA_COLS - 1
    pick = (dc[None] == jnp.arange(2 * NA_COLS - 1)[:, None, None]).astype(F32)
    t = jnp.einsum('hrc,cqk->hrqk', rpb.astype(F32), pick, precision=HIGHEST)
    t = jnp.where(col_ok[None, None], t, -jnp.inf)
    t = t.reshape(NA_HEADS // 2, 2, 2 * NA_ROWS - 1, GRID_W, GRID_W)
    t = jnp.transpose(t, (0, 2, 4, 1, 3))
    return t.reshape(NA_HEADS // 2, 2 * NA_ROWS - 1, GRID_W, 2 * GRID_W)


def _na_latent(q, k, v, ck, cv, bias):
    w = NA_HEADS * HEAD_DIM

    return pl.pallas_call(
        _na_latent_kernel,
        grid=(DEC_BATCH, GRID_ROWS),
        in_specs=[
            pl.BlockSpec((GRID_W, w), lambda b, r: (b * GRID_ROWS + r, 0)),
            pl.BlockSpec((DEC_SEQ, w), lambda b, r: (b, 0)),
            pl.BlockSpec((DEC_SEQ, w), lambda b, r: (b, 0)),
            pl.BlockSpec((None, PAST_LEN, w), lambda b, r: (b, 0, 0)),
            pl.BlockSpec((None, PAST_LEN, w), lambda b, r: (b, 0, 0)),
            pl.BlockSpec((NA_HEADS // 2, 2 * NA_ROWS - 1, GRID_W, 2 * GRID_W), lambda b, r: (0, 0, 0, 0)),
        ],
        out_specs=pl.BlockSpec((GRID_W, w), lambda b, r: (b * GRID_ROWS + r, 0)),
        out_shape=jax.ShapeDtypeStruct((N_SAMPLE, w), BF16),
        compiler_params=_params(("arbitrary", "arbitrary")),
        name="na_latent",
    )(q, k, v, ck, cv, bias)


SWA_SPAN = SWA_BLOCK + 2 * SWA_WINDOW


def _swa_latent_kernel(q_ref, k_ref, v_ref, ck_ref, cv_ref, sink_ref, o_ref):
    n = pl.program_id(1)
    k0 = pl.multiple_of(jnp.clip(n - 1, 0, DEC_SEQ // SWA_BLOCK - SWA_SPAN // SWA_BLOCK) * SWA_BLOCK,
                        SWA_BLOCK)
    rows = SWA_GROUP * SWA_BLOCK
    qpos = n * SWA_BLOCK + lax.broadcasted_iota(jnp.int32, (rows, SWA_SPAN), 0) % SWA_BLOCK
    kpos = k0 + lax.broadcasted_iota(jnp.int32, (rows, SWA_SPAN), 1)
    ok = jnp.abs(qpos - kpos) <= SWA_WINDOW
    outs = []
    for kh in range(SWA_KV_HEADS):
        ksl = slice(kh * HEAD_DIM, (kh + 1) * HEAD_DIM)
        heads = range(kh * SWA_GROUP, (kh + 1) * SWA_GROUP)
        q = _stack_heads(q_ref, heads)
        sink = _per_head_column([sink_ref[h] for h in heads], SWA_BLOCK)
        s_b = jnp.where(ok, _nt_dot(q, k_ref[pl.ds(k0, SWA_SPAN), ksl]), -jnp.inf)
        s_c = _nt_dot(q, ck_ref[:, ksl])
        m = jnp.maximum(jnp.maximum(jnp.max(s_b, axis=-1, keepdims=True),
                                    jnp.max(s_c, axis=-1, keepdims=True)), sink)
        e_b = jnp.exp(s_b - m)
        e_c = jnp.exp(s_c - m)
        l = (jnp.sum(e_b, axis=-1, keepdims=True) + jnp.sum(e_c, axis=-1, keepdims=True)
             + jnp.exp(sink - m))
        o = (jnp.dot(e_b.astype(BF16), v_ref[pl.ds(k0, SWA_SPAN), ksl], preferred_element_type=F32)
             + jnp.dot(e_c.astype(BF16), cv_ref[:, ksl], preferred_element_type=F32)) / l
        outs += [o[g * SWA_BLOCK:(g + 1) * SWA_BLOCK] for g in range(SWA_GROUP)]
    o_ref[...] = jnp.concatenate(outs, axis=1).astype(o_ref.dtype)


def _swa_latent(q, k, v, ck, cv, sink):
    qw = SWA_Q_HEADS * HEAD_DIM
    kw = SWA_KV_HEADS * HEAD_DIM
    nb = DEC_SEQ // SWA_BLOCK
    return pl.pallas_call(
        _swa_latent_kernel,
        grid=(DEC_BATCH, nb),
        in_specs=[
            pl.BlockSpec((SWA_BLOCK, qw), lambda b, n: (b * nb + n, 0)),
            pl.BlockSpec((DEC_SEQ, kw), lambda b, n: (b, 0)),
            pl.BlockSpec((DEC_SEQ, kw), lambda b, n: (b, 0)),
            pl.BlockSpec((None, PAST_LEN, kw), lambda b, n: (b, 0, 0)),
            pl.BlockSpec((None, PAST_LEN, kw), lambda b, n: (b, 0, 0)),
            pl.BlockSpec(memory_space=pltpu.SMEM),
        ],
        out_specs=pl.BlockSpec((SWA_BLOCK, qw), lambda b, n: (b * nb + n, 0)),
        out_shape=jax.ShapeDtypeStruct((N_SAMPLE, qw), BF16),
        compiler_params=_params(("arbitrary", "arbitrary")),
        name="swa_latent",
    )(q, k, v, ck, cv, sink)


def _swiglu_part(h, wg, wu, wo):
    g = jnp.dot(h, wg, preferred_element_type=F32)
    u = jnp.dot(h, wu, preferred_element_type=F32)
    a = (_silu(g) * u).astype(BF16)
    return jnp.dot(a, wo, preferred_element_type=F32)


def _dense_ffn_kernel(xp_ref, xs_ref, g_ref, sh_ref, sc_ref, wg_ref, wu_ref, wo_ref, gate_ref, o_ref,
                      h_ref, acc_ref):
    j = pl.program_id(1)
    tm = o_ref.shape[0]

    @pl.when(j == 0)
    def _():
        x = _pick_half(xp_ref, xs_ref, tm)
        h_ref[...] = _norm_mod(x, g_ref[...], sh_ref[...], sc_ref[...]).astype(BF16)

    part = _swiglu_part(h_ref[...], wg_ref[...], wu_ref[...], wo_ref[...])

    @pl.when(j == 0)
    def _():
        acc_ref[...] = part

    @pl.when(j > 0)
    def _():
        acc_ref[...] += part

    @pl.when(j == pl.num_programs(1) - 1)
    def _():
        o_ref[...] = _pick_half(xp_ref, xs_ref, tm) + gate_ref[...] * acc_ref[...]


def _dense_ffn(x, gain, mod, layer, w_in, w_out, widx):
    tm, tf = TM_FFN, TF_DENSE
    nf = FFN_DIM // tf
    x_arrs, x_specs = _split_stream(x, tm, 1)

    def mspec(which):
        return pl.BlockSpec((None, None, None, 1, D_MODEL),
                            lambda i, j: (layer, which, _group_of_row(i * tm), 0, 0))

    return pl.pallas_call(
        _dense_ffn_kernel,
        grid=(N_TOK // tm, nf),
        in_specs=[
            *x_specs,
            pl.BlockSpec((1, D_MODEL), lambda i, j: (0, 0)),
            mspec(3), mspec(4),
            pl.BlockSpec((None, D_MODEL, tf), lambda i, j: (widx, 0, j)),
            pl.BlockSpec((None, D_MODEL, tf), lambda i, j: (widx, 0, nf + j)),
            pl.BlockSpec((None, tf, D_MODEL), lambda i, j: (widx, j, 0)),
            mspec(5),
        ],
        out_specs=pl.BlockSpec((tm, D_MODEL), lambda i, j: (i, 0)),
        out_shape=jax.ShapeDtypeStruct((N_TOK, D_MODEL), F32),
        scratch_shapes=[pltpu.VMEM((tm, D_MODEL), BF16), pltpu.VMEM((tm, D_MODEL), F32)],
        compiler_params=_params(("arbitrary", "arbitrary")),
        name="dense_ffn",
    )(*x_arrs, gain, mod, mod, w_in, w_in, w_out, mod)


def _tile_rows(ref, sub, c, n):
    return ref.at[pl.ds(sub * n * ROW_TILES + c, n, stride=ROW_TILES), :]


def _moe_ffn_kernel(te_ref, ns_ref, tok_ref, x_hbm, wg_ref, wu_ref, wo_ref, o_ref,
                    xbuf_ref, sem, h_ref, acc_ref, wgb_ref, wub_ref, wob_ref):
    i, j = pl.program_id(0), pl.program_id(1)
    nsub = ns_ref[i]
    last = j == pl.num_programs(1) - 1

    def gather(tile, wait):
        for s in range(N_SUB):
            @pl.when(s < ns_ref[tile])
            def _():
                def body(r, carry):
                    src = 0 if wait else tok_ref[tile * TM_MOE + s * SUB_MOE + r]
                    cp = _row_copy(x_hbm, src, xbuf_ref, s * SUB_MOE + r, sem)
                    cp.wait() if wait else cp.start(priority=s % 2)
                    return carry

                lax.fori_loop(0, SUB_MOE, body, 0, unroll=8)

    @pl.when((i == 0) & (j == 0))
    def _():
        gather(0, wait=False)

    @pl.when(j == 0)
    def _():
        gather(i, wait=True)
        for s in range(N_SUB):
            @pl.when(s < nsub)
            def _():
                for c in range(ROW_TILES):
                    h_ref[pl.ds(s * SUB_MOE, SUB_MOE), c * LANES:(c + 1) * LANES] = _tile_rows(
                        xbuf_ref, s, c, SUB_MOE)[...].astype(BF16)

        @pl.when(i + 1 < pl.num_programs(0))
        def _():
            gather(i + 1, wait=False)

    @pl.when(nsub > 0)
    def _():
        wgb_ref[...] = wg_ref[...].astype(BF16)
        wub_ref[...] = wu_ref[...].astype(BF16)
        wob_ref[...] = wo_ref[...].astype(BF16)

    for s in range(N_SUB):
        rows = pl.ds(s * SUB_MOE, SUB_MOE)

        @pl.when(s < nsub)
        def _():
            part = _swiglu_part(h_ref[rows, :], wgb_ref[...], wub_ref[...], wob_ref[...])

            @pl.when(j == 0)
            def _():
                acc_ref[rows, :] = part

            @pl.when(j > 0)
            def _():
                acc_ref[rows, :] += part

            @pl.when(last)
            def _():
                for c in range(ROW_TILES):
                    _tile_rows(o_ref, s, c, SUB_MOE)[...] = acc_ref[rows, c * LANES:(c + 1) * LANES]

        @pl.when((s >= nsub) & last)
        def _():
            o_ref[pl.ds(s * SUB_MOE * ROW_TILES, SUB_MOE * ROW_TILES), :] = jnp.zeros(
                (SUB_MOE * ROW_TILES, LANES), F32)


def _moe_ffn(x_tiles, row_tok, tile_expert, tile_nsub, w_in, w_out, widx):
    tm, tf = TM_MOE, TF_MOE
    nf = EXPERT_DIM // tf

    def jj(i, j, ns):
        return jnp.where(ns[i] > 0, j, nf - 1)

    grid_spec = pltpu.PrefetchScalarGridSpec(
        num_scalar_prefetch=3,
        grid=(N_MOE_TILES, nf),
        in_specs=[
            pl.BlockSpec(memory_space=pl.ANY),
            pl.BlockSpec((None, None, D_MODEL, tf), lambda i, j, te, ns, tok: (widx, te[i], 0, jj(i, j, ns))),
            pl.BlockSpec((None, None, D_MODEL, tf),
                         lambda i, j, te, ns, tok: (widx, te[i], 0, nf + jj(i, j, ns))),
            pl.BlockSpec((None, None, tf, D_MODEL), lambda i, j, te, ns, tok: (widx, te[i], jj(i, j, ns), 0)),
        ],
        out_specs=pl.BlockSpec((tm * ROW_TILES, LANES), lambda i, j, te, ns, tok: (i, 0)),
        scratch_shapes=[pltpu.VMEM((tm * ROW_TILES, LANES), F32), pltpu.SemaphoreType.DMA(()),
                        pltpu.VMEM((tm, D_MODEL), BF16), pltpu.VMEM((tm, D_MODEL), F32),
                        pltpu.VMEM((D_MODEL, tf), BF16), pltpu.VMEM((D_MODEL, tf), BF16),
                        pltpu.VMEM((tf, D_MODEL), BF16)],
    )
    return pl.pallas_call(
        _moe_ffn_kernel,
        grid_spec=grid_spec,
        out_shape=jax.ShapeDtypeStruct((N_PAD * ROW_TILES, LANES), F32),
        compiler_params=_params(("arbitrary", "arbitrary")),
        name="moe_ffn",
    )(tile_expert, tile_nsub, row_tok, x_tiles, w_in, w_in, w_out)


def _router_kernel(x_ref, g_ref, sh_ref, sc_ref, wr_ref, h_ref, info_ref, w_ref, cnt_ref, base_ref):
    tm = x_ref.shape[0]

    @pl.when(pl.program_id(0) == 0)
    def _():
        base_ref[...] = jnp.zeros_like(base_ref)

    h = _norm_mod(x_ref[...], g_ref[...], sh_ref[...], sc_ref[...])
    for c in range(ROW_TILES):
        h_ref[pl.ds(c, tm, stride=ROW_TILES), :] = h[:, c * LANES:(c + 1) * LANES]
    logits = jnp.dot(h, wr_ref[...], precision=HIGHEST, preferred_element_type=F32)
    lane = lax.broadcasted_iota(jnp.int32, logits.shape, 1)
    logits = jnp.where(lane < N_EXPERTS, logits, -jnp.inf)
    m1 = jnp.max(logits, axis=-1, keepdims=True)
    i1 = jnp.min(jnp.where(logits == m1, lane, LANES), axis=-1, keepdims=True)
    rest = jnp.where(lane == i1, -jnp.inf, logits)
    m2 = jnp.max(rest, axis=-1, keepdims=True)
    i2 = jnp.min(jnp.where(rest == m2, lane, LANES), axis=-1, keepdims=True)
    e2 = jnp.exp(m2 - m1)
    den = 1.0 + e2
    w_ref[...] = jnp.where(lane == 0, 1.0 / den, jnp.where(lane == 1, e2 / den, 0.0))
    chosen = jnp.where((lane == i1) | (lane == i2), 1.0, 0.0)
    rt = lax.broadcasted_iota(jnp.int32, (tm, tm), 0)
    ct = lax.broadcasted_iota(jnp.int32, (tm, tm), 1)
    earlier = jnp.where(ct < rt, 1.0, 0.0).astype(BF16)
    before = jnp.dot(earlier, chosen.astype(BF16), preferred_element_type=F32) + base_ref[0:1, :]
    r1 = jnp.sum(jnp.where(lane == i1, before, 0.0), axis=-1, keepdims=True).astype(jnp.int32)
    r2 = jnp.sum(jnp.where(lane == i2, before, 0.0), axis=-1, keepdims=True).astype(jnp.int32)
    info_ref[...] = jnp.where(lane == 0, i1, jnp.where(lane == 1, i2, jnp.where(
        lane == 2, r1, jnp.where(lane == 3, r2, 0))))
    total = base_ref[0:1, :] + jnp.sum(chosen, axis=0, keepdims=True)
    base_ref[...] = jnp.broadcast_to(total, base_ref.shape)
    cnt_ref[...] = jnp.broadcast_to(total, cnt_ref.shape).astype(jnp.int32)


def _router(x, gain, mod, layer, w_router):
    tm = TM_ROUTER
    wr = jnp.pad(w_router, ((0, 0), (0, LANES - N_EXPERTS)))

    def mspec(which):
        return pl.BlockSpec((None, None, None, 1, D_MODEL),
                            lambda i: (layer, which, _group_of_row(i * tm), 0, 0))

    return pl.pallas_call(
        _router_kernel,
        grid=(N_TOK // tm,),
        in_specs=[
            pl.BlockSpec((tm, D_MODEL), lambda i: (i, 0)),
            pl.BlockSpec((1, D_MODEL), lambda i: (0, 0)),
            mspec(3), mspec(4),
            pl.BlockSpec((D_MODEL, LANES), lambda i: (0, 0)),
        ],
        out_specs=[
            pl.BlockSpec((tm * ROW_TILES, LANES), lambda i: (i, 0)),
            pl.BlockSpec((tm, LANES), lambda i: (i, 0)),
            pl.BlockSpec((tm, LANES), lambda i: (i, 0)),
            pl.BlockSpec((SUBLANES, LANES), lambda i: (0, 0)),
        ],
        out_shape=[
            jax.ShapeDtypeStruct((N_TOK * ROW_TILES, LANES), F32),
            jax.ShapeDtypeStruct((N_TOK, LANES), jnp.int32),
            jax.ShapeDtypeStruct((N_TOK, LANES), F32),
            jax.ShapeDtypeStruct((SUBLANES, LANES), jnp.int32),
        ],
        scratch_shapes=[pltpu.VMEM((SUBLANES, LANES), F32)],
        compiler_params=_params(("arbitrary",)),
        name="router",
    )(x, gain, mod, mod, wr)


def _row_copy(src_hbm, row, dst_ref, r, sem):
    src = src_hbm.at[pl.ds(pl.multiple_of(row * ROW_TILES, ROW_TILES), ROW_TILES), :]
    dst = dst_ref.at[pl.ds(pl.multiple_of(r * ROW_TILES, ROW_TILES), ROW_TILES), :]
    return pltpu.make_async_copy(src, dst, sem)


def _combine_kernel(pos_ref, y_hbm, res_ref, gate_ref, w_ref, o_ref, a_ref, b_ref, sem):
    base = pl.program_id(0) * TG

    def issue(r, carry):
        _row_copy(y_hbm, pos_ref[2 * (base + r)], a_ref, r, sem.at[0]).start()
        _row_copy(y_hbm, pos_ref[2 * (base + r) + 1], b_ref, r, sem.at[1]).start(priority=1)
        return carry

    lax.fori_loop(0, TG, issue, 0, unroll=8)

    def drain(r, carry):
        _row_copy(y_hbm, 0, a_ref, r, sem.at[0]).wait()
        _row_copy(y_hbm, 0, b_ref, r, sem.at[1]).wait()
        return carry

    lax.fori_loop(0, TG, drain, 0, unroll=8)
    w = w_ref[...]
    w0, w1 = w[:, 0:1], w[:, 1:2]
    for c in range(ROW_TILES):
        cols = slice(c * LANES, (c + 1) * LANES)
        rows = pl.ds(c, TG, stride=ROW_TILES)
        mix = w0 * a_ref[rows, :] + w1 * b_ref[rows, :]
        o_ref[:, cols] = res_ref[:, cols] + gate_ref[:, cols] * mix


def _combine(yb, pos, res, mod, layer, top_w):
    grid_spec = pltpu.PrefetchScalarGridSpec(
        num_scalar_prefetch=1,
        grid=(N_TOK // TG,),
        in_specs=[
            pl.BlockSpec(memory_space=pl.ANY),
            pl.BlockSpec((TG, D_MODEL), lambda i, p: (i, 0)),
            pl.BlockSpec((None, None, None, 1, D_MODEL),
                         lambda i, p: (layer, 5, _group_of_row(i * TG), 0, 0)),
            pl.BlockSpec((TG, LANES), lambda i, p: (i, 0)),
        ],
        out_specs=pl.BlockSpec((TG, D_MODEL), lambda i, p: (i, 0)),
        scratch_shapes=[pltpu.VMEM((TG * ROW_TILES, LANES), F32), pltpu.VMEM((TG * ROW_TILES, LANES), F32),
                        pltpu.SemaphoreType.DMA((2,))],
    )
    return pl.pallas_call(
        _combine_kernel,
        grid_spec=grid_spec,
        out_shape=jax.ShapeDtypeStruct((N_TOK, D_MODEL), F32),
        compiler_params=_params(("arbitrary",)),
        name="moe_combine",
    )(pos, yb, res, mod, top_w)


def _moe_layer(x, gain, mod, layer, w_router, w_in, w_out, widx):
    h, info, top_w, cnt = _router(x, gain, mod, layer, w_router)
    experts = jnp.arange(N_EXPERTS, dtype=jnp.int32)
    counts = cnt[0, :N_EXPERTS]
    padded = (counts + TM_MOE - 1) // TM_MOE * TM_MOE
    pad_end = jnp.cumsum(padded)
    pad_start = pad_end - padded
    e_sel = info[:, :TOP_K]
    start_sel = jnp.sum(jnp.where(e_sel[..., None] == experts, pad_start, 0), axis=-1)
    dest = (start_sel + info[:, TOP_K:2 * TOP_K]).reshape(-1).astype(jnp.int32)
    row_tok = jnp.zeros((N_PAD,), jnp.int32).at[dest].set(
        jnp.arange(N_SLOT, dtype=jnp.int32) // TOP_K, unique_indices=True)
    tile_start = jnp.arange(N_MOE_TILES, dtype=jnp.int32) * TM_MOE
    n_before = jnp.sum(pad_end[None, :] <= tile_start[:, None], axis=1)
    used = tile_start < pad_end[-1]
    last_expert = jnp.sum(pad_end < pad_end[-1])
    tile_expert = jnp.where(used, jnp.minimum(n_before, N_EXPERTS - 1), last_expert).astype(jnp.int32)
    seg_end = jnp.sum(jnp.where(tile_expert[:, None] == experts, pad_start + counts, 0), axis=-1)
    rows_used = jnp.clip(seg_end - tile_start, 0, TM_MOE)
    tile_nsub = jnp.where(used, (rows_used + SUB_MOE - 1) // SUB_MOE, 0).astype(jnp.int32)
    yb = _moe_ffn(h, row_tok, tile_expert, tile_nsub, w_in, w_out, widx)
    return _combine(yb, dest, x, mod, layer, top_w)


def _conv_kernel(x_ref, w_ref, b_ref, o_ref):
    x = x_ref[...]
    n = x.shape[0]
    t = lax.broadcasted_iota(jnp.int32, x.shape, 0)
    acc = b_ref[...] + w_ref[SSD_CONV // 2:SSD_CONV // 2 + 1, :] * x
    for k in range(SSD_CONV):
        s = k - SSD_CONV // 2
        if s == 0:
            continue
        xs = pltpu.roll(x, (-s) % n, axis=0)
        ok = (t + s >= 0) & (t + s < n)
        acc = acc + w_ref[k:k + 1, :] * jnp.where(ok, xs, 0.0)
    o_ref[...] = _silu(acc)


def _ssd_conv(zx, conv_w, conv_b, row0, n_seq, seq_len):
    tn = 512
    c0 = SSD_INNER // tn
    r0 = row0 // seq_len
    return pl.pallas_call(
        _conv_kernel,
        grid=(n_seq, SSD_CONV_DIM // tn),
        in_specs=[
            pl.BlockSpec((seq_len, tn), lambda b, j: (r0 + b, c0 + j)),
            pl.BlockSpec((SSD_CONV, tn), lambda b, j: (0, j)),
            pl.BlockSpec((1, tn), lambda b, j: (0, j)),
        ],
        out_specs=pl.BlockSpec((seq_len, tn), lambda b, j: (b, j)),
        out_shape=jax.ShapeDtypeStruct((n_seq * seq_len, SSD_CONV_DIM), F32),
        compiler_params=_params(("arbitrary", "arbitrary")),
        name="ssd_conv",
    )(zx, conv_w, conv_b[None])


def _softplus(x):
    return jnp.maximum(x, 0.0) + jnp.log(1.0 + jnp.exp(-jnp.abs(x)))


def _ssd_scan_kernel(*refs, reverse, has_h0, out_state):
    refs = list(refs)
    xbc_ref, dt_ref, dtb_ref, alog_ref = refs[:4]
    pos = 4
    h0_ref = None
    if has_h0:
        h0_ref = refs[pos]
        pos += 1
    y_ref = refs[pos]
    pos += 1
    sf_ref = None
    if out_state:
        sf_ref = refs[pos]
        pos += 1
    s_ref = refs[pos]
    c = pl.program_id(1)
    q = SSD_CHUNK

    @pl.when(c == 0)
    def _():
        if has_h0:
            s_ref[...] = h0_ref[...]
        else:
            s_ref[...] = jnp.zeros_like(s_ref)

    dt = _softplus(dt_ref[...] + dtb_ref[...])
    da = dt * (-jnp.exp(alog_ref[...]))
    ri = lax.broadcasted_iota(jnp.int32, (q, q), 0)
    ci = lax.broadcasted_iota(jnp.int32, (q, q), 1)
    reach = (ri <= ci) if reverse else (ri >= ci)
    cs = jnp.dot(jnp.where(reach, 1.0, 0.0), da, precision=HIGHEST, preferred_element_type=F32)
    cs_t = cs.T
    cs_end = cs[0:1, :] if reverse else cs[q - 1:q, :]
    dec = jnp.exp(cs_end)
    dt_t = dt.T
    ecs_t = jnp.exp(cs).T
    w_t = (dt * jnp.exp(cs_end - cs)).T
    x_t = xbc_ref[:, 0:SSD_INNER].T
    col0 = SSD_HEADS if reverse else 0
    gn = SSD_GROUPS * SSD_STATE
    ys, states = [], []
    for g in range(SSD_GROUPS):
        bg = xbc_ref[:, SSD_INNER + g * SSD_STATE:SSD_INNER + (g + 1) * SSD_STATE].astype(BF16)
        cg = xbc_ref[:, SSD_INNER + gn + g * SSD_STATE:SSD_INNER + gn + (g + 1) * SSD_STATE].astype(BF16)
        cb = _nt_dot(cg, bg)
        for e in range(HEADS_PER_GROUP):
            h = g * HEADS_PER_GROUP + e
            col = col0 + h
            hs = slice(h * SSD_HEADDIM, (h + 1) * SSD_HEADDIM)
            xh_t = x_t[hs, :]
            seg = cs[:, col:col + 1] - cs_t[col:col + 1, :]
            decay = jnp.exp(jnp.where(reach, seg, -jnp.inf))
            state = s_ref[hs, :]
            y_diag_t = _nt_dot((xh_t * dt_t[col:col + 1, :]).astype(BF16), (cb * decay).astype(BF16))
            y_off_t = _nt_dot(state.astype(BF16), cg) * ecs_t[col:col + 1, :]
            ys.append(y_diag_t + y_off_t)
            xw_t = (xh_t * w_t[col:col + 1, :]).astype(BF16)
            states.append(dec[0:1, col:col + 1] * state + jnp.dot(xw_t, bg, preferred_element_type=F32))
    y_ref[...] = jnp.concatenate(ys, axis=0).T
    new_state = jnp.concatenate(states, axis=0)
    s_ref[...] = new_state

    if out_state:
        @pl.when(c == pl.num_programs(1) - 1)
        def _():
            sf_ref[...] = new_state


def _ssd_scan(xbc, dt_all, dt_bias, a_log, row0, n_seq, seq_len, reverse, h0=None, out_state=False):
    nc = seq_len // SSD_CHUNK
    c0 = row0 // SSD_CHUNK

    def chunk(c):
        return nc - 1 - c if reverse else c

    in_specs = [
        pl.BlockSpec((SSD_CHUNK, SSD_CONV_DIM), lambda b, c: (b * nc + chunk(c), 0)),
        pl.BlockSpec((SSD_CHUNK, LANES), lambda b, c: (c0 + b * nc + chunk(c), 0)),
        pl.BlockSpec((1, LANES), lambda b, c: (0, 0)),
        pl.BlockSpec((1, LANES), lambda b, c: (0, 0)),
    ]
    args = [xbc, dt_all, dt_bias, a_log]
    if h0 is not None:
        in_specs.append(pl.BlockSpec((None, SSD_INNER, SSD_STATE), lambda b, c: (b, 0, 0)))
        args.append(h0)
    out_specs = [pl.BlockSpec((SSD_CHUNK, SSD_INNER), lambda b, c: (b * nc + chunk(c), 0))]
    out_shape = [jax.ShapeDtypeStruct((n_seq * seq_len, SSD_INNER), F32)]
    if out_state:
        out_specs.append(pl.BlockSpec((None, SSD_INNER, SSD_STATE), lambda b, c: (b, 0, 0)))
        out_shape.append(jax.ShapeDtypeStruct((n_seq, SSD_INNER, SSD_STATE), F32))
    return pl.pallas_call(
        functools.partial(_ssd_scan_kernel, reverse=reverse, has_h0=h0 is not None, out_state=out_state),
        grid=(n_seq, nc),
        in_specs=in_specs,
        out_specs=out_specs,
        out_shape=out_shape,
        scratch_shapes=[pltpu.VMEM((SSD_INNER, SSD_STATE), F32)],
        compiler_params=_params(("arbitrary", "arbitrary")),
        name="ssd_scan",
    )(*args)


def _ssd_out_kernel(yf_ref, yb_ref, x_ref, z_ref, d_ref, nw_ref, w_ref, r_ref, g_ref, o_ref):
    y = yf_ref[...] + yb_ref[...] + d_ref[...] * x_ref[...]
    y = y * _silu(z_ref[...])
    gw = SSD_INNER // SSD_GROUPS
    parts = []
    for g in range(SSD_GROUPS):
        yg = y[:, g * gw:(g + 1) * gw]
        yg = yg * lax.rsqrt(jnp.mean(yg * yg, axis=-1, keepdims=True) + NORM_EPS)
        parts.append((yg * nw_ref[:, g * gw:(g + 1) * gw]).astype(BF16))
    yn = jnp.concatenate(parts, axis=1)
    acc = jnp.dot(yn, w_ref[...].astype(BF16), preferred_element_type=F32)
    o_ref[...] = r_ref[...] + g_ref[...] * acc


def _ssd_out(yf, yb, xbc, zx, d_exp, norm_w, w_out, widx, res, mod, layer, row0, n_rows):
    tm = TM_SSD_OUT
    r0 = row0 // tm
    return pl.pallas_call(
        _ssd_out_kernel,
        grid=(n_rows // tm,),
        in_specs=[
            pl.BlockSpec((tm, SSD_INNER), lambda i: (i, 0)),
            pl.BlockSpec((tm, SSD_INNER), lambda i: (i, 0)),
            pl.BlockSpec((tm, SSD_INNER), lambda i: (i, 0)),
            pl.BlockSpec((tm, SSD_INNER), lambda i: (r0 + i, 0)),
            pl.BlockSpec((1, SSD_INNER), lambda i: (0, 0)),
            pl.BlockSpec((1, SSD_INNER), lambda i: (0, 0)),
            pl.BlockSpec((None, SSD_INNER, D_MODEL), lambda i: (widx, 0, 0)),
            pl.BlockSpec((tm, D_MODEL), lambda i: (r0 + i, 0)),
            pl.BlockSpec((None, None, None, 1, D_MODEL),
                         lambda i: (layer, 2, _group_of_row(row0 + i * tm), 0, 0)),
        ],
        out_specs=pl.BlockSpec((tm, D_MODEL), lambda i: (i, 0)),
        out_shape=jax.ShapeDtypeStruct((n_rows, D_MODEL), F32),
        compiler_params=_params(("arbitrary",)),
        name="ssd_out",
    )(yf, yb, xbc, zx, d_exp, norm_w, w_out, res, mod)


def _ssd_layer(x, gain, mod, layer, j, state_f, state_b, w_in, conv_w, conv_b, dt_bias, a_log, d_skip,
               norm_w, w_out):
    n_zx = SSD_INNER + SSD_CONV_DIM
    zx = _nm_matmul(x, gain, mod, layer, w_in, j, n_zx, name="ssd_in_proj")
    pad = LANES - 2 * SSD_HEADS
    w_dt = jnp.pad(w_in[j][:, n_zx:], ((0, 0), (0, pad)))[None]
    dt_all = _nm_matmul(x, gain, mod, layer, w_dt, 0, LANES, tn=LANES, name="ssd_dt_proj")
    dtb = jnp.pad(dt_bias.reshape(1, -1), ((0, 0), (0, pad)))
    alog = jnp.pad(a_log.reshape(1, -1), ((0, 0), (0, pad)))
    d_exp = jnp.repeat(d_skip, SSD_HEADDIM)[None]
    outs, states = [], []
    for row0, n_seq, seq_len, h0s in ((0, BATCH, SEQ, None), (N_PROMPT, DEC_BATCH, DEC_SEQ, (state_f, state_b))):
        xbc = _ssd_conv(zx, conv_w, conv_b, row0, n_seq, seq_len)
        ys = []
        for reverse in (False, True):
            h0 = None if h0s is None else h0s[int(reverse)].reshape(n_seq, SSD_INNER, SSD_STATE)
            res = _ssd_scan(xbc, dt_all, dtb, alog, row0, n_seq, seq_len, reverse, h0=h0,
                            out_state=h0s is None)
            ys.append(res[0])
            if h0s is None:
                states.append(res[1])
        outs.append(_ssd_out(ys[0], ys[1], xbc, zx, d_exp, norm_w[None], w_out, j, x, mod, layer,
                             row0, n_seq * seq_len))
    shape = (BATCH, SSD_HEADS, SSD_HEADDIM, SSD_STATE)
    return tuple(outs), states[0].reshape(shape), states[1].reshape(shape)


def _na_layer(x, gain, mod, layer, j, cache_k, cache_v, w_qkv, w_o, q_norm, k_norm, rpb):
    w = NA_HEADS * HEAD_DIM
    qp, kp, vp = _qkv_proj(x, gain, mod, layer, w_qkv, j, q_norm, k_norm, w, 0, N_PROMPT, F32)
    qs, ks, vs = _qkv_proj(x, gain, mod, layer, w_qkv, j, q_norm, k_norm, w, N_PROMPT, N_SAMPLE, BF16)
    op = _ctx_attn(qp, kp, vp)
    ck = cache_k.reshape(DEC_BATCH, PAST_LEN, w).astype(BF16)
    cv = cache_v.reshape(DEC_BATCH, PAST_LEN, w).astype(BF16)
    os_ = _na_latent(qs, ks, vs, ck, cv, _na_bias_table(rpb))
    shape = (BATCH, SEQ, NA_HEADS, HEAD_DIM)
    return _linear_res((op, os_), w_o, j, x, mod, layer, 2), kp.reshape(shape), vp.reshape(shape)


def _swa_layer(x, gain, mod, layer, j, cache_k, cache_v, w_qkv, w_o, q_norm, k_norm, sink):
    kw = SWA_KV_HEADS * HEAD_DIM
    qp, kp, vp = _qkv_proj(x, gain, mod, layer, w_qkv, j, q_norm, k_norm, kw, 0, N_PROMPT, F32)
    qs, ks, vs = _qkv_proj(x, gain, mod, layer, w_qkv, j, q_norm, k_norm, kw, N_PROMPT, N_SAMPLE, BF16,
                           rope_tabs=_rope_tables())
    sink = sink.astype(F32)
    op = _ctx_attn(qp, kp, vp, sink)
    ck = cache_k.reshape(DEC_BATCH, PAST_LEN, kw).astype(BF16)
    cv = cache_v.reshape(DEC_BATCH, PAST_LEN, kw).astype(BF16)
    os_ = _swa_latent(qs, ks, vs, ck, cv, sink)
    shape = (BATCH, SEQ, SWA_KV_HEADS, HEAD_DIM)
    return _linear_res((op, os_), w_o, j, x, mod, layer, 2), kp.reshape(shape), vp.reshape(shape)


def kernel(x_prompt, x_sample, cache_na_k, cache_na_v, cache_swa_k, cache_swa_v, state_ssd_fwd, state_ssd_bwd, c, c_ctx, ada_w, ada_b, norm_mix, norm_ffn, na_w_qkv, na_w_o, na_q_norm, na_k_norm, na_rpb, swa_w_qkv, swa_w_o, swa_q_norm, swa_k_norm, swa_sink, ssd_w_in, ssd_conv_w, ssd_conv_b, ssd_dt_bias, ssd_a_log, ssd_d, ssd_norm, ssd_w_out, ffn_w_in, ffn_w_out, moe_router, moe_w_in, moe_w_out):
    x = (x_prompt.reshape(N_PROMPT, D_MODEL), x_sample.reshape(N_SAMPLE, D_MODEL))
    cond = jnp.concatenate([c_ctx[None], c, jnp.zeros((N_GROUPS - 1 - DEC_BATCH, D_MODEL), F32)], axis=0)
    mod = _adaln(cond, ada_w, ada_b)
    ffn_in_bf, ffn_out_bf = ffn_w_in.astype(BF16), ffn_w_out.astype(BF16)
    na_w_qkv, na_w_o = na_w_qkv.astype(BF16), na_w_o.astype(BF16)
    swa_w_qkv, swa_w_o = swa_w_qkv.astype(BF16), swa_w_o.astype(BF16)
    ssd_w_in = ssd_w_in.astype(BF16)
    na_k, na_v, swa_k, swa_v, ssd_f, ssd_b = [], [], [], [], [], []
    for i in range(DEPTH):
        kind, j = i % N_MIXERS, i // N_MIXERS
        g_mix, g_ffn = norm_mix[i][None], norm_ffn[i][None]
        if kind == 0:
            x, kc, vc = _na_layer(x, g_mix, mod, i, j, cache_na_k[:, j], cache_na_v[:, j], na_w_qkv,
                                  na_w_o, na_q_norm[j], na_k_norm[j], na_rpb[j])
            na_k.append(kc)
            na_v.append(vc)
        elif kind == 1:
            x, kc, vc = _swa_layer(x, g_mix, mod, i, j, cache_swa_k[:, j], cache_swa_v[:, j], swa_w_qkv,
                                   swa_w_o, swa_q_norm[j], swa_k_norm[j], swa_sink[j])
            swa_k.append(kc)
            swa_v.append(vc)
        else:
            x, sf, sb = _ssd_layer(x, g_mix, mod, i, j, state_ssd_fwd[:, j], state_ssd_bwd[:, j], ssd_w_in,
                                   ssd_conv_w[j], ssd_conv_b[j], ssd_dt_bias[j], ssd_a_log[j], ssd_d[j],
                                   ssd_norm[j], ssd_w_out)
            ssd_f.append(sf)
            ssd_b.append(sb)
        if i % 2 == 0:
            x = _dense_ffn(x, g_ffn, mod, i, ffn_in_bf, ffn_out_bf, i // 2)
        else:
            x = _moe_layer(x, g_ffn, mod, i, moe_router[i // 2], moe_w_in, moe_w_out, i // 2)
    yp = x[:N_PROMPT].reshape(BATCH, SEQ, D_MODEL)
    ys = x[N_PROMPT:].reshape(DEC_BATCH, DEC_SEQ, D_MODEL)
    return (yp, ys, jnp.stack(na_k, axis=1), jnp.stack(na_v, axis=1), jnp.stack(swa_k, axis=1),
            jnp.stack(swa_v, axis=1), jnp.stack(ssd_f, axis=1), jnp.stack(ssd_b, axis=1))
```

```python
import functools

import jax
import jax.numpy as jnp
from jax import lax
from jax.experimental import pallas as pl
from jax.experimental.pallas import tpu as pltpu

F32 = jnp.float32
BF16 = jnp.bfloat16
HIGHEST = lax.Precision.HIGHEST

D_MODEL = 1024
BATCH = 32
SEQ = 256
DEPTH = 4
DEC_BATCH = 4
DEC_SEQ = 1024
PAST_LEN = 256
GRID_W = 64
N_MIXERS = 3
HEAD_DIM = 64
NORM_EPS = 1e-6
ROPE_BASE = 10000.0
NA_HEADS = 16
NA_ROWS = 8
NA_COLS = 16
SWA_Q_HEADS = 16
SWA_KV_HEADS = 4
SWA_GROUP = SWA_Q_HEADS // SWA_KV_HEADS
SWA_WINDOW = 128
SWA_BLOCK = 128
SSD_INNER = 2 * D_MODEL
SSD_HEADDIM = 64
SSD_HEADS = SSD_INNER // SSD_HEADDIM
SSD_GROUPS = 4
SSD_STATE = 128
SSD_CONV = 5
SSD_CHUNK = 128
SSD_CONV_DIM = SSD_INNER + 2 * SSD_GROUPS * SSD_STATE
FFN_DIM = 2816
N_EXPERTS = 8
TOP_K = 2
EXPERT_DIM = 3584

N_PROMPT = BATCH * SEQ
N_SAMPLE = DEC_BATCH * DEC_SEQ
N_TOK = N_PROMPT + N_SAMPLE
N_GROUPS = 8
LANES = 128
SUBLANES = 8
MXU_DIM = 256
GRID_ROWS = DEC_SEQ // GRID_W
HEADS_PER_GROUP = SSD_HEADS // SSD_GROUPS
ROW_TILES = D_MODEL // LANES

VMEM_LIMIT = 56 * 1024 * 1024
TM_LIN = 1024
TN_LIN = 512
TM_FFN = 512
TF_DENSE = 1408
TF_MOE = 512
TM_MOE = 2048
SUB_MOE = 512
N_SUB = TM_MOE // SUB_MOE
N_SLOT = N_TOK * TOP_K
N_PAD = N_SLOT + N_EXPERTS * TM_MOE
N_MOE_TILES = N_PAD // TM_MOE
TG = 512
TM_ROUTER = 512
TM_SSD_OUT = 256
assert ROW_TILES == SUBLANES


def _params(sem):
    return pltpu.CompilerParams(dimension_semantics=sem, vmem_limit_bytes=VMEM_LIMIT)


def _group_of_row(start):
    return jnp.where(start < N_PROMPT, 0, 1 + (start - N_PROMPT) // DEC_SEQ)


def _silu(x):
    return x / (1.0 + jnp.exp(-x))


def _norm_mod(x, gain, shift, scale):
    y = x * lax.rsqrt(jnp.mean(x * x, axis=-1, keepdims=True) + NORM_EPS)
    return (y * gain) * (1.0 + scale) + shift


def _adaln_kernel(c_ref, w_ref, b_ref, o_ref):
    s = _silu(c_ref[...])
    o_ref[...] = jnp.dot(s, w_ref[...], precision=HIGHEST, preferred_element_type=F32) + b_ref[...]


def _adaln(cond, ada_w, ada_b):
    tn = 1024
    out = pl.pallas_call(
        _adaln_kernel,
        grid=(DEPTH, 6 * D_MODEL // tn),
        in_specs=[
            pl.BlockSpec((N_GROUPS, D_MODEL), lambda l, j: (0, 0)),
            pl.BlockSpec((None, D_MODEL, tn), lambda l, j: (l, 0, j)),
            pl.BlockSpec((None, 1, tn), lambda l, j: (l, 0, j)),
        ],
        out_specs=pl.BlockSpec((None, N_GROUPS, tn), lambda l, j: (l, 0, j)),
        out_shape=jax.ShapeDtypeStruct((DEPTH, N_GROUPS, 6 * D_MODEL), F32),
        compiler_params=_params(("arbitrary", "arbitrary")),
        name="adaln",
    )(cond, ada_w, ada_b.reshape(DEPTH, 1, 6 * D_MODEL))
    out = out.reshape(DEPTH, N_GROUPS, 6, D_MODEL)
    return jnp.transpose(out, (0, 2, 1, 3)).reshape(DEPTH, 6, N_GROUPS, 1, D_MODEL)


def _nm_matmul_kernel(x_ref, g_ref, sh_ref, sc_ref, w_ref, o_ref, h_ref):
    @pl.when(pl.program_id(1) == 0)
    def _():
        h_ref[...] = _norm_mod(x_ref[...], g_ref[...], sh_ref[...], sc_ref[...]).astype(BF16)

    o_ref[...] = jnp.dot(h_ref[...], w_ref[...].astype(BF16), preferred_element_type=F32)


def _nm_matmul(x, gain, mod, layer, w, widx, n_out, tn=TN_LIN, name="nm_matmul"):
    tm = TM_LIN
    return pl.pallas_call(
        _nm_matmul_kernel,
        grid=(N_TOK // tm, n_out // tn),
        in_specs=[
            pl.BlockSpec((tm, D_MODEL), lambda i, j: (i, 0)),
            pl.BlockSpec((1, D_MODEL), lambda i, j: (0, 0)),
            pl.BlockSpec((None, None, None, 1, D_MODEL),
                         lambda i, j: (layer, 0, _group_of_row(i * tm), 0, 0)),
            pl.BlockSpec((None, None, None, 1, D_MODEL),
                         lambda i, j: (layer, 1, _group_of_row(i * tm), 0, 0)),
            pl.BlockSpec((None, D_MODEL, tn), lambda i, j: (widx, 0, j)),
        ],
        out_specs=pl.BlockSpec((tm, tn), lambda i, j: (i, j)),
        out_shape=jax.ShapeDtypeStruct((N_TOK, n_out), F32),
        scratch_shapes=[pltpu.VMEM((tm, D_MODEL), BF16)],
        compiler_params=_params(("arbitrary", "arbitrary")),
        name=name,
    )(x, gain, mod, mod, w)


def _split_stream(arr, tm, n_col):
    a, b = arr if isinstance(arr, tuple) else (arr, arr)
    n_p = N_PROMPT // tm
    off = 0 if isinstance(arr, tuple) else n_p
    blk = (tm, a.shape[1] // n_col)

    def col(own, j, parked):
        return jnp.where(own, j, parked) if n_col > 1 else 0

    spec_a = pl.BlockSpec(blk, lambda i, j: (jnp.minimum(i, n_p - 1), col(i < n_p, j, n_col - 1)))
    spec_b = pl.BlockSpec(blk, lambda i, j: (off + jnp.maximum(i - n_p, 0), col(i >= n_p, j, 0)))
    return (a, b), (spec_a, spec_b)


def _pick_half(a_ref, b_ref, tm):
    return jnp.where(pl.program_id(0) < N_PROMPT // tm, a_ref[...], b_ref[...])


def _linear_res_kernel(xp_ref, xs_ref, w_ref, rp_ref, rs_ref, g_ref, o_ref):
    tm = o_ref.shape[0]
    acc = jnp.dot(_pick_half(xp_ref, xs_ref, tm), w_ref[...].astype(BF16), preferred_element_type=F32)
    o_ref[...] = _pick_half(rp_ref, rs_ref, tm) + g_ref[...] * acc


def _linear_res(x, w, widx, res, mod, layer, which):
    tm, tn = TM_LIN, TN_LIN
    n_col = D_MODEL // tn
    x_arrs, x_specs = _split_stream(x, tm, 1)
    r_arrs, r_specs = _split_stream(res, tm, n_col)
    k = x_arrs[0].shape[1]
    return pl.pallas_call(
        _linear_res_kernel,
        grid=(N_TOK // tm, n_col),
        in_specs=[
            *x_specs,
            pl.BlockSpec((None, k, tn), lambda i, j: (widx, 0, j)),
            *r_specs,
            pl.BlockSpec((None, None, None, 1, tn),
                         lambda i, j: (layer, which, _group_of_row(i * tm), 0, j)),
        ],
        out_specs=pl.BlockSpec((tm, tn), lambda i, j: (i, j)),
        out_shape=jax.ShapeDtypeStruct((N_TOK, D_MODEL), F32),
        compiler_params=_params(("arbitrary", "arbitrary")),
        name="linear_res",
    )(*x_arrs, w, *r_arrs, mod)


def _group_sumsq(x):
    r = lax.broadcasted_iota(jnp.int32, (MXU_DIM, MXU_DIM), 0) // HEAD_DIM
    c = lax.broadcasted_iota(jnp.int32, (MXU_DIM, MXU_DIM), 1) // HEAD_DIM
    ones = jnp.where(r == c, 1.0, 0.0).astype(BF16)
    outs = []
    for t in range(x.shape[1] // MXU_DIM):
        x2 = x[:, t * MXU_DIM:(t + 1) * MXU_DIM]
        x2 = x2 * x2
        hi = x2.astype(BF16)
        lo = (x2 - hi.astype(F32)).astype(BF16)
        outs.append(jnp.dot(hi, ones, preferred_element_type=F32)
                    + jnp.dot(lo, ones, preferred_element_type=F32))
    return jnp.concatenate(outs, axis=1) if len(outs) > 1 else outs[0]


def _head_norm(x, gain):
    return x * lax.rsqrt(_group_sumsq(x) * (1.0 / HEAD_DIM) + NORM_EPS) * gain


def _rope(x, cos, sin):
    w = x.shape[1]
    lane = lax.broadcasted_iota(jnp.int32, x.shape, 1)
    partner = jnp.where((lane % 32) < 16, pltpu.roll(x, w - 16, axis=1), pltpu.roll(x, 16, axis=1))
    return x * cos + partner * sin


def _qkv_kernel(*refs, nq, rope):
    if rope:
        (x_ref, g_ref, sh_ref, sc_ref, w_ref, qg_ref, kg_ref, cos_ref, sin_ref,
         q_ref, k_ref, v_ref, h_ref) = refs
    else:
        x_ref, g_ref, sh_ref, sc_ref, w_ref, qg_ref, kg_ref, q_ref, k_ref, v_ref, h_ref = refs
    j = pl.program_id(1)

    @pl.when(j == 0)
    def _():
        h_ref[...] = _norm_mod(x_ref[...], g_ref[...], sh_ref[...], sc_ref[...]).astype(BF16)

    acc = jnp.dot(h_ref[...], w_ref[...].astype(BF16), preferred_element_type=F32)

    def normed(gain_ref):
        y = _head_norm(acc, gain_ref[...])
        return _rope(y, cos_ref[...], sin_ref[...]) if rope else y

    @pl.when(j < nq)
    def _():
        q_ref[...] = (normed(qg_ref) * (HEAD_DIM ** -0.5)).astype(q_ref.dtype)

    @pl.when(j == nq)
    def _():
        k_ref[...] = normed(kg_ref).astype(k_ref.dtype)

    @pl.when(j == nq + 1)
    def _():
        v_ref[...] = acc.astype(v_ref.dtype)


def _qkv_proj(x, gain, mod, layer, w, widx, q_gain, k_gain, kw, row0, n_rows, kv_dtype, rope_tabs=None):
    tm = TM_LIN
    qw = NA_HEADS * HEAD_DIM
    nq = qw // kw
    r0 = row0 // tm
    if isinstance(x, tuple):
        x, r0 = (x[0], 0) if row0 == 0 else (x[1], 0)
    in_specs = [
        pl.BlockSpec((tm, D_MODEL), lambda i, j: (r0 + i, 0)),
        pl.BlockSpec((1, D_MODEL), lambda i, j: (0, 0)),
        pl.BlockSpec((None, None, None, 1, D_MODEL),
                     lambda i, j: (layer, 0, _group_of_row(row0 + i * tm), 0, 0)),
        pl.BlockSpec((None, None, None, 1, D_MODEL),
                     lambda i, j: (layer, 1, _group_of_row(row0 + i * tm), 0, 0)),
        pl.BlockSpec((None, D_MODEL, kw), lambda i, j: (widx, 0, j)),
        pl.BlockSpec((1, kw), lambda i, j: (0, 0)),
        pl.BlockSpec((1, kw), lambda i, j: (0, 0)),
    ]
    args = [x, gain, mod, mod, w, jnp.tile(q_gain, kw // HEAD_DIM)[None], jnp.tile(k_gain, kw // HEAD_DIM)[None]]
    if rope_tabs is not None:
        in_specs += [pl.BlockSpec((tm, kw), lambda i, j: (0, 0))] * 2
        args += [t[:, :kw] for t in rope_tabs]
    return pl.pallas_call(
        functools.partial(_qkv_kernel, nq=nq, rope=rope_tabs is not None),
        grid=(n_rows // tm, nq + 2),
        in_specs=in_specs,
        out_specs=[
            pl.BlockSpec((tm, kw), lambda i, j: (i, jnp.minimum(j, nq - 1))),
            pl.BlockSpec((tm, kw), lambda i, j: (i, 0)),
            pl.BlockSpec((tm, kw), lambda i, j: (i, 0)),
        ],
        out_shape=[
            jax.ShapeDtypeStruct((n_rows, qw), BF16),
            jax.ShapeDtypeStruct((n_rows, kw), kv_dtype),
            jax.ShapeDtypeStruct((n_rows, kw), kv_dtype),
        ],
        scratch_shapes=[pltpu.VMEM((tm, D_MODEL), BF16)],
        compiler_params=_params(("arbitrary", "arbitrary")),
        name="qkv_proj",
    )(*args)


def _rope_tables():
    quarter = HEAD_DIM // 4
    t = jnp.arange(DEC_SEQ)
    pos = jnp.stack([t // GRID_W, t % GRID_W], axis=-1).astype(F32)
    inv = ROPE_BASE ** (-jnp.arange(quarter, dtype=F32) / quarter)
    ang = pos[:, :, None] * inv
    cos, sin = jnp.cos(ang), jnp.sin(ang)
    cos64 = jnp.concatenate([cos[:, 0], cos[:, 0], cos[:, 1], cos[:, 1]], axis=1)
    sin64 = jnp.concatenate([-sin[:, 0], sin[:, 0], -sin[:, 1], sin[:, 1]], axis=1)
    return jnp.tile(cos64, (1, SWA_KV_HEADS)), jnp.tile(sin64, (1, SWA_KV_HEADS))


def _nt_dot(a, b):
    return lax.dot_general(a, b, (((1,), (1,)), ((), ())), preferred_element_type=F32)


def _tn_dot(a, b):
    return lax.dot_general(a, b, (((0,), (0,)), ((), ())), preferred_element_type=F32)


def _pair_queries(q):
    n = q.shape[0]
    ri = lax.broadcasted_iota(jnp.int32, (2 * n, LANES), 0)
    ci = lax.broadcasted_iota(jnp.int32, (2 * n, LANES), 1)
    return jnp.where((ri // n) == (ci // HEAD_DIM), jnp.concatenate([q, q], axis=0), jnp.zeros((), BF16))


def _pair_outputs(o2, l):
    n = o2.shape[0] // 2
    o2 = o2 / jnp.broadcast_to(l, (LANES, 2 * n)).T
    first = lax.broadcasted_iota(jnp.int32, (n, LANES), 1) < HEAD_DIM
    return jnp.where(first, o2[:n], o2[n:])


def _ctx_attn_pairs(q_ref, k_ref, v_ref, o_ref):
    outs = []
    for p in range(q_ref.shape[1] // LANES):
        sl = slice(p * LANES, (p + 1) * LANES)
        s = _nt_dot(k_ref[:, sl].astype(BF16), _pair_queries(q_ref[:, sl]))
        e = jnp.exp(s - jnp.max(s, axis=0, keepdims=True))
        l = jnp.sum(e, axis=0, keepdims=True)
        outs.append(_pair_outputs(_tn_dot(e.astype(BF16), v_ref[:, sl].astype(BF16)), l))
    o_ref[...] = jnp.concatenate(outs, axis=1).astype(o_ref.dtype)


def _stack_heads(q_ref, heads):
    parts = [q_ref[:, h * HEAD_DIM:(h + 1) * HEAD_DIM] for h in heads]
    return jnp.concatenate(parts, axis=0) if len(parts) > 1 else parts[0]


def _per_head_column(scalars, rows):
    gid = lax.broadcasted_iota(jnp.int32, (len(scalars) * rows, 1), 0) // rows
    col = jnp.full(gid.shape, scalars[-1], F32)
    for g in range(len(scalars) - 1):
        col = jnp.where(gid == g, scalars[g], col)
    return col


def _ctx_attn_kernel(*refs, group, use_sink):
    if use_sink:
        q_ref, k_ref, v_ref, sink_ref, o_ref = refs
    else:
        q_ref, k_ref, v_ref, o_ref = refs
    rows = q_ref.shape[0]
    if group == 1:
        _ctx_attn_pairs(q_ref, k_ref, v_ref, o_ref)
        return
    outs = []
    for kh in range(k_ref.shape[1] // HEAD_DIM):
        heads = range(kh * group, (kh + 1) * group)
        q = _stack_heads(q_ref, heads)
        k = k_ref[:, kh * HEAD_DIM:(kh + 1) * HEAD_DIM].astype(BF16)
        v = v_ref[:, kh * HEAD_DIM:(kh + 1) * HEAD_DIM].astype(BF16)
        s = _nt_dot(q, k)
        m = jnp.max(s, axis=-1, keepdims=True)
        if use_sink:
            sink = _per_head_column([sink_ref[h] for h in heads], rows)
            m = jnp.maximum(m, sink)
        e = jnp.exp(s - m)
        l = jnp.sum(e, axis=-1, keepdims=True)
        if use_sink:
            l = l + jnp.exp(sink - m)
        o = jnp.dot(e.astype(BF16), v, preferred_element_type=F32) / l
        outs += [o[g * rows:(g + 1) * rows] for g in range(group)]
    o_ref[...] = jnp.concatenate(outs, axis=1).astype(o_ref.dtype)


def _ctx_attn(q, k, v, sink=None):
    kw = k.shape[1]
    group = q.shape[1] // kw
    in_specs = [
        pl.BlockSpec((SEQ, q.shape[1]), lambda b: (b, 0)),
        pl.BlockSpec((SEQ, kw), lambda b: (b, 0)),
        pl.BlockSpec((SEQ, kw), lambda b: (b, 0)),
    ]
    args = [q, k, v]
    if sink is not None:
        in_specs.append(pl.BlockSpec(memory_space=pltpu.SMEM))
        args.append(sink)
    return pl.pallas_call(
        functools.partial(_ctx_attn_kernel, group=group, use_sink=sink is not None),
        grid=(BATCH,),
        in_specs=in_specs,
        out_specs=pl.BlockSpec((SEQ, q.shape[1]), lambda b: (b, 0)),
        out_shape=jax.ShapeDtypeStruct((N_PROMPT, q.shape[1]), BF16),
        compiler_params=_params(("arbitrary",)),
        name="ctx_attn",
    )(*args)


def _na_latent_kernel(q_ref, k_ref, v_ref, ck_ref, cv_ref, bias_ref, o_ref):
    r = pl.program_id(1)
    kr = NA_ROWS
    start = jnp.clip(r - kr // 2, 0, GRID_ROWS - kr)
    row0 = pl.multiple_of(start * GRID_W, GRID_W)
    win = pl.ds(row0, kr * GRID_W)
    outs = []
    for p in range(NA_HEADS // 2):
        sl = slice(p * LANES, (p + 1) * LANES)
        qd = _pair_queries(q_ref[:, sl])
        bias = bias_ref[p, pl.ds(start - r + kr - 1, kr)].reshape(kr * GRID_W, LANES)
        s_nb = _nt_dot(k_ref[win, sl], qd) + bias
        s_cx = _nt_dot(ck_ref[:, sl], qd)
        m = jnp.maximum(jnp.max(s_nb, axis=0, keepdims=True), jnp.max(s_cx, axis=0, keepdims=True))
        e_nb = jnp.exp(s_nb - m)
        e_cx = jnp.exp(s_cx - m)
        l = jnp.sum(e_nb, axis=0, keepdims=True) + jnp.sum(e_cx, axis=0, keepdims=True)
        o2 = _tn_dot(e_nb.astype(BF16), v_ref[win, sl]) + _tn_dot(e_cx.astype(BF16), cv_ref[:, sl])
        outs.append(_pair_outputs(o2, l))
    o_ref[...] = jnp.concatenate(outs, axis=1).astype(o_ref.dtype)


def _na_bias_table(rpb):
    col = jnp.arange(GRID_W)
    col_start = jnp.clip(col - NA_COLS // 2, 0, GRID_W - NA_COLS)
    col_ok = (col[None, :] >= col_start[:, None]) & (col[None, :] < col_start[:, None] + NA_COLS)
    dc = jnp.clip(col[None, :] - col[:, None], 1 - NA_COLS, NA_COLS - 1) + NA_COLS - 1
    pick = (dc[None] == jnp.arange(2 * NA_COLS - 1)[:, None, None]).astype(F32)
    t = jnp.einsum('hrc,cqk->hrqk', rpb.astype(F32), pick, precision=HIGHEST)
    t = jnp.where(col_ok[None, None], t, -jnp.inf)
    t = t.reshape(NA_HEADS // 2, 2, 2 * NA_ROWS - 1, GRID_W, GRID_W)
    t = jnp.transpose(t, (0, 2, 4, 1, 3))
    return t.reshape(NA_HEADS // 2, 2 * NA_ROWS - 1, GRID_W, 2 * GRID_W)


def _na_latent(q, k, v, ck, cv, bias):
    w = NA_HEADS * HEAD_DIM

    return pl.pallas_call(
        _na_latent_kernel,
        grid=(DEC_BATCH, GRID_ROWS),
        in_specs=[
            pl.BlockSpec((GRID_W, w), lambda b, r: (b * GRID_ROWS + r, 0)),
            pl.BlockSpec((DEC_SEQ, w), lambda b, r: (b, 0)),
            pl.BlockSpec((DEC_SEQ, w), lambda b, r: (b, 0)),
            pl.BlockSpec((None, PAST_LEN, w), lambda b, r: (b, 0, 0)),
            pl.BlockSpec((None, PAST_LEN, w), lambda b, r: (b, 0, 0)),
            pl.BlockSpec((NA_HEADS // 2, 2 * NA_ROWS - 1, GRID_W, 2 * GRID_W), lambda b, r: (0, 0, 0, 0)),
        ],
        out_specs=pl.BlockSpec((GRID_W, w), lambda b, r: (b * GRID_ROWS + r, 0)),
        out_shape=jax.ShapeDtypeStruct((N_SAMPLE, w), BF16),
        compiler_params=_params(("arbitrary", "arbitrary")),
        name="na_latent",
    )(q, k, v, ck, cv, bias)


SWA_SPAN = SWA_BLOCK + 2 * SWA_WINDOW


def _swa_latent_kernel(q_ref, k_ref, v_ref, ck_ref, cv_ref, sink_ref, o_ref):
    n = pl.program_id(1)
    k0 = pl.multiple_of(jnp.clip(n - 1, 0, DEC_SEQ // SWA_BLOCK - SWA_SPAN // SWA_BLOCK) * SWA_BLOCK,
                        SWA_BLOCK)
    rows = SWA_GROUP * SWA_BLOCK
    qpos = n * SWA_BLOCK + lax.broadcasted_iota(jnp.int32, (rows, SWA_SPAN), 0) % SWA_BLOCK
    kpos = k0 + lax.broadcasted_iota(jnp.int32, (rows, SWA_SPAN), 1)
    ok = jnp.abs(qpos - kpos) <= SWA_WINDOW
    outs = []
    for kh in range(SWA_KV_HEADS):
        ksl = slice(kh * HEAD_DIM, (kh + 1) * HEAD_DIM)
        heads = range(kh * SWA_GROUP, (kh + 1) * SWA_GROUP)
        q = _stack_heads(q_ref, heads)
        sink = _per_head_column([sink_ref[h] for h in heads], SWA_BLOCK)
        s_b = jnp.where(ok, _nt_dot(q, k_ref[pl.ds(k0, SWA_SPAN), ksl]), -jnp.inf)
        s_c = _nt_dot(q, ck_ref[:, ksl])
        m = jnp.maximum(jnp.maximum(jnp.max(s_b, axis=-1, keepdims=True),
                                    jnp.max(s_c, axis=-1, keepdims=True)), sink)
        e_b = jnp.exp(s_b - m)
        e_c = jnp.exp(s_c - m)
        l = (jnp.sum(e_b, axis=-1, keepdims=True) + jnp.sum(e_c, axis=-1, keepdims=True)
             + jnp.exp(sink - m))
        o = (jnp.dot(e_b.astype(BF16), v_ref[pl.ds(k0, SWA_SPAN), ksl], preferred_element_type=F32)
             + jnp.dot(e_c.astype(BF16), cv_ref[:, ksl], preferred_element_type=F32)) / l
        outs += [o[g * SWA_BLOCK:(g + 1) * SWA_BLOCK] for g in range(SWA_GROUP)]
    o_ref[...] = jnp.concatenate(outs, axis=1).astype(o_ref.dtype)


def _swa_latent(q, k, v, ck, cv, sink):
    qw = SWA_Q_HEADS * HEAD_DIM
    kw = SWA_KV_HEADS * HEAD_DIM
    nb = DEC_SEQ // SWA_BLOCK
    return pl.pallas_call(
        _swa_latent_kernel,
        grid=(DEC_BATCH, nb),
        in_specs=[
            pl.BlockSpec((SWA_BLOCK, qw), lambda b, n: (b * nb + n, 0)),
            pl.BlockSpec((DEC_SEQ, kw), lambda b, n: (b, 0)),
            pl.BlockSpec((DEC_SEQ, kw), lambda b, n: (b, 0)),
            pl.BlockSpec((None, PAST_LEN, kw), lambda b, n: (b, 0, 0)),
            pl.BlockSpec((None, PAST_LEN, kw), lambda b, n: (b, 0, 0)),
            pl.BlockSpec(memory_space=pltpu.SMEM),
        ],
        out_specs=pl.BlockSpec((SWA_BLOCK, qw), lambda b, n: (b * nb + n, 0)),
        out_shape=jax.ShapeDtypeStruct((N_SAMPLE, qw), BF16),
        compiler_params=_params(("arbitrary", "arbitrary")),
        name="swa_latent",
    )(q, k, v, ck, cv, sink)


def _swiglu_part(h, wg, wu, wo):
    g = jnp.dot(h, wg, preferred_element_type=F32)
    u = jnp.dot(h, wu, preferred_element_type=F32)
    a = (_silu(g) * u).astype(BF16)
    return jnp.dot(a, wo, preferred_element_type=F32)


def _dense_ffn_kernel(xp_ref, xs_ref, g_ref, sh_ref, sc_ref, wg_ref, wu_ref, wo_ref, gate_ref, o_ref,
                      h_ref, acc_ref):
    j = pl.program_id(1)
    tm = o_ref.shape[0]

    @pl.when(j == 0)
    def _():
        x = _pick_half(xp_ref, xs_ref, tm)
        h_ref[...] = _norm_mod(x, g_ref[...], sh_ref[...], sc_ref[...]).astype(BF16)

    part = _swiglu_part(h_ref[...], wg_ref[...], wu_ref[...], wo_ref[...])

    @pl.when(j == 0)
    def _():
        acc_ref[...] = part

    @pl.when(j > 0)
    def _():
        acc_ref[...] += part

    @pl.when(j == pl.num_programs(1) - 1)
    def _():
        o_ref[...] = _pick_half(xp_ref, xs_ref, tm) + gate_ref[...] * acc_ref[...]


def _dense_ffn(x, gain, mod, layer, w_in, w_out, widx):
    tm, tf = TM_FFN, TF_DENSE
    nf = FFN_DIM // tf
    x_arrs, x_specs = _split_stream(x, tm, 1)

    def mspec(which):
        return pl.BlockSpec((None, None, None, 1, D_MODEL),
                            lambda i, j: (layer, which, _group_of_row(i * tm), 0, 0))

    return pl.pallas_call(
        _dense_ffn_kernel,
        grid=(N_TOK // tm, nf),
        in_specs=[
            *x_specs,
            pl.BlockSpec((1, D_MODEL), lambda i, j: (0, 0)),
            mspec(3), mspec(4),
            pl.BlockSpec((None, D_MODEL, tf), lambda i, j: (widx, 0, j)),
            pl.BlockSpec((None, D_MODEL, tf), lambda i, j: (widx, 0, nf + j)),
            pl.BlockSpec((None, tf, D_MODEL), lambda i, j: (widx, j, 0)),
            mspec(5),
        ],
        out_specs=pl.BlockSpec((tm, D_MODEL), lambda i, j: (i, 0)),
        out_shape=jax.ShapeDtypeStruct((N_TOK, D_MODEL), F32),
        scratch_shapes=[pltpu.VMEM((tm, D_MODEL), BF16), pltpu.VMEM((tm, D_MODEL), F32)],
        compiler_params=_params(("arbitrary", "arbitrary")),
        name="dense_ffn",
    )(*x_arrs, gain, mod, mod, w_in, w_in, w_out, mod)


def _tile_rows(ref, sub, c, n):
    return ref.at[pl.ds(sub * n * ROW_TILES + c, n, stride=ROW_TILES), :]


def _moe_ffn_kernel(te_ref, ns_ref, tok_ref, x_hbm, wg_ref, wu_ref, wo_ref, o_ref,
                    xbuf_ref, sem, h_ref, acc_ref, wgb_ref, wub_ref, wob_ref):
    i, j = pl.program_id(0), pl.program_id(1)
    nsub = ns_ref[i]
    last = j == pl.num_programs(1) - 1

    def gather(tile, wait):
        for s in range(N_SUB):
            @pl.when(s < ns_ref[tile])
            def _():
                def body(r, carry):
                    src = 0 if wait else tok_ref[tile * TM_MOE + s * SUB_MOE + r]
                    cp = _row_copy(x_hbm, src, xbuf_ref, s * SUB_MOE + r, sem)
                    cp.wait() if wait else cp.start(priority=s % 2)
                    return carry

                lax.fori_loop(0, SUB_MOE, body, 0, unroll=8)

    @pl.when((i == 0) & (j == 0))
    def _():
        gather(0, wait=False)

    @pl.when(j == 0)
    def _():
        gather(i, wait=True)
        for s in range(N_SUB):
            @pl.when(s < nsub)
            def _():
                for c in range(ROW_TILES):
                    h_ref[pl.ds(s * SUB_MOE, SUB_MOE), c * LANES:(c + 1) * LANES] = _tile_rows(
                        xbuf_ref, s, c, SUB_MOE)[...].astype(BF16)

        @pl.when(i + 1 < pl.num_programs(0))
        def _():
            gather(i + 1, wait=False)

    @pl.when(nsub > 0)
    def _():
        wgb_ref[...] = wg_ref[...].astype(BF16)
        wub_ref[...] = wu_ref[...].astype(BF16)
        wob_ref[...] = wo_ref[...].astype(BF16)

    for s in range(N_SUB):
        rows = pl.ds(s * SUB_MOE, SUB_MOE)

        @pl.when(s < nsub)
        def _():
            part = _swiglu_part(h_ref[rows, :], wgb_ref[...], wub_ref[...], wob_ref[...])

            @pl.when(j == 0)
            def _():
                acc_ref[rows, :] = part

            @pl.when(j > 0)
            def _():
                acc_ref[rows, :] += part

            @pl.when(last)
            def _():
                for c in range(ROW_TILES):
                    _tile_rows(o_ref, s, c, SUB_MOE)[...] = acc_ref[rows, c * LANES:(c + 1) * LANES]

        @pl.when((s >= nsub) & last)
        def _():
            o_ref[pl.ds(s * SUB_MOE * ROW_TILES, SUB_MOE * ROW_TILES), :] = jnp.zeros(
                (SUB_MOE * ROW_TILES, LANES), F32)


def _moe_ffn(x_tiles, row_tok, tile_expert, tile_nsub, w_in, w_out, widx):
    tm, tf = TM_MOE, TF_MOE
    nf = EXPERT_DIM // tf

    def jj(i, j, ns):
        return jnp.where(ns[i] > 0, j, nf - 1)

    grid_spec = pltpu.PrefetchScalarGridSpec(
        num_scalar_prefetch=3,
        grid=(N_MOE_TILES, nf),
        in_specs=[
            pl.BlockSpec(memory_space=pl.ANY),
            pl.BlockSpec((None, None, D_MODEL, tf), lambda i, j, te, ns, tok: (widx, te[i], 0, jj(i, j, ns))),
            pl.BlockSpec((None, None, D_MODEL, tf),
                         lambda i, j, te, ns, tok: (widx, te[i], 0, nf + jj(i, j, ns))),
            pl.BlockSpec((None, None, tf, D_MODEL), lambda i, j, te, ns, tok: (widx, te[i], jj(i, j, ns), 0)),
        ],
        out_specs=pl.BlockSpec((tm * ROW_TILES, LANES), lambda i, j, te, ns, tok: (i, 0)),
        scratch_shapes=[pltpu.VMEM((tm * ROW_TILES, LANES), F32), pltpu.SemaphoreType.DMA(()),
                        pltpu.VMEM((tm, D_MODEL), BF16), pltpu.VMEM((tm, D_MODEL), F32),
                        pltpu.VMEM((D_MODEL, tf), BF16), pltpu.VMEM((D_MODEL, tf), BF16),
                        pltpu.VMEM((tf, D_MODEL), BF16)],
    )
    return pl.pallas_call(
        _moe_ffn_kernel,
        grid_spec=grid_spec,
        out_shape=jax.ShapeDtypeStruct((N_PAD * ROW_TILES, LANES), F32),
        compiler_params=_params(("arbitrary", "arbitrary")),
        name="moe_ffn",
    )(tile_expert, tile_nsub, row_tok, x_tiles, w_in, w_in, w_out)


def _router_kernel(x_ref, g_ref, sh_ref, sc_ref, wr_ref, h_ref, info_ref, w_ref, cnt_ref, base_ref):
    tm = x_ref.shape[0]

    @pl.when(pl.program_id(0) == 0)
    def _():
        base_ref[...] = jnp.zeros_like(base_ref)

    h = _norm_mod(x_ref[...], g_ref[...], sh_ref[...], sc_ref[...])
    for c in range(ROW_TILES):
        h_ref[pl.ds(c, tm, stride=ROW_TILES), :] = h[:, c * LANES:(c + 1) * LANES]
    w = wr_ref[...]
    h_hi, w_hi = h.astype(BF16), w.astype(BF16)
    h_lo, w_lo = (h - h_hi.astype(F32)).astype(BF16), (w - w_hi.astype(F32)).astype(BF16)
    logits = (jnp.dot(h_hi, w_hi, preferred_element_type=F32)
              + (jnp.dot(h_hi, w_lo, preferred_element_type=F32) + jnp.dot(h_lo, w_hi, preferred_element_type=F32)))
    lane = lax.broadcasted_iota(jnp.int32, logits.shape, 1)
    logits = jnp.where(lane < N_EXPERTS, logits, -jnp.inf)
    m1 = jnp.max(logits, axis=-1, keepdims=True)
    i1 = jnp.min(jnp.where(logits == m1, lane, LANES), axis=-1, keepdims=True)
    rest = jnp.where(lane == i1, -jnp.inf, logits)
    m2 = jnp.max(rest, axis=-1, keepdims=True)
    i2 = jnp.min(jnp.where(rest == m2, lane, LANES), axis=-1, keepdims=True)
    e2 = jnp.exp(m2 - m1)
    den = 1.0 + e2
    w_ref[...] = jnp.where(lane == 0, 1.0 / den, jnp.where(lane == 1, e2 / den, 0.0))
    chosen = jnp.where((lane == i1) | (lane == i2), 1.0, 0.0)
    rt = lax.broadcasted_iota(jnp.int32, (tm, tm), 0)
    ct = lax.broadcasted_iota(jnp.int32, (tm, tm), 1)
    earlier = jnp.where(ct < rt, 1.0, 0.0).astype(BF16)
    before = jnp.dot(earlier, chosen.astype(BF16), preferred_element_type=F32) + base_ref[0:1, :]
    r1 = jnp.sum(jnp.where(lane == i1, before, 0.0), axis=-1, keepdims=True).astype(jnp.int32)
    r2 = jnp.sum(jnp.where(lane == i2, before, 0.0), axis=-1, keepdims=True).astype(jnp.int32)
    info_ref[...] = jnp.where(lane == 0, i1, jnp.where(lane == 1, i2, jnp.where(
        lane == 2, r1, jnp.where(lane == 3, r2, 0))))
    total = base_ref[0:1, :] + jnp.sum(chosen, axis=0, keepdims=True)
    base_ref[...] = jnp.broadcast_to(total, base_ref.shape)
    cnt_ref[...] = jnp.broadcast_to(total, cnt_ref.shape).astype(jnp.int32)


def _router(x, gain, mod, layer, w_router):
    tm = TM_ROUTER
    wr = jnp.pad(w_router, ((0, 0), (0, LANES - N_EXPERTS)))

    def mspec(which):
        return pl.BlockSpec((None, None, None, 1, D_MODEL),
                            lambda i: (layer, which, _group_of_row(i * tm), 0, 0))

    return pl.pallas_call(
        _router_kernel,
        grid=(N_TOK // tm,),
        in_specs=[
            pl.BlockSpec((tm, D_MODEL), lambda i: (i, 0)),
            pl.BlockSpec((1, D_MODEL), lambda i: (0, 0)),
            mspec(3), mspec(4),
            pl.BlockSpec((D_MODEL, LANES), lambda i: (0, 0)),
        ],
        out_specs=[
            pl.BlockSpec((tm * ROW_TILES, LANES), lambda i: (i, 0)),
            pl.BlockSpec((tm, LANES), lambda i: (i, 0)),
            pl.BlockSpec((tm, LANES), lambda i: (i, 0)),
            pl.BlockSpec((SUBLANES, LANES), lambda i: (0, 0)),
        ],
        out_shape=[
            jax.ShapeDtypeStruct((N_TOK * ROW_TILES, LANES), F32),
            jax.ShapeDtypeStruct((N_TOK, LANES), jnp.int32),
            jax.ShapeDtypeStruct((N_TOK, LANES), F32),
            jax.ShapeDtypeStruct((SUBLANES, LANES), jnp.int32),
        ],
        scratch_shapes=[pltpu.VMEM((SUBLANES, LANES), F32)],
        compiler_params=_params(("arbitrary",)),
        name="router",
    )(x, gain, mod, mod, wr)


def _row_copy(src_hbm, row, dst_ref, r, sem):
    src = src_hbm.at[pl.ds(pl.multiple_of(row * ROW_TILES, ROW_TILES), ROW_TILES), :]
    dst = dst_ref.at[pl.ds(pl.multiple_of(r * ROW_TILES, ROW_TILES), ROW_TILES), :]
    return pltpu.make_async_copy(src, dst, sem)


def _combine_kernel(pos_ref, y_hbm, res_ref, gate_ref, w_ref, o_ref, a_ref, b_ref, sem):
    base = pl.program_id(0) * TG

    def issue(r, carry):
        _row_copy(y_hbm, pos_ref[2 * (base + r)], a_ref, r, sem.at[0]).start()
        _row_copy(y_hbm, pos_ref[2 * (base + r) + 1], b_ref, r, sem.at[1]).start(priority=1)
        return carry

    lax.fori_loop(0, TG, issue, 0, unroll=8)

    def drain(r, carry):
        _row_copy(y_hbm, 0, a_ref, r, sem.at[0]).wait()
        _row_copy(y_hbm, 0, b_ref, r, sem.at[1]).wait()
        return carry

    lax.fori_loop(0, TG, drain, 0, unroll=8)
    w = w_ref[...]
    w0, w1 = w[:, 0:1], w[:, 1:2]
    for c in range(ROW_TILES):
        cols = slice(c * LANES, (c + 1) * LANES)
        rows = pl.ds(c, TG, stride=ROW_TILES)
        mix = w0 * a_ref[rows, :] + w1 * b_ref[rows, :]
        o_ref[:, cols] = res_ref[:, cols] + gate_ref[:, cols] * mix


def _combine(yb, pos, res, mod, layer, top_w):
    grid_spec = pltpu.PrefetchScalarGridSpec(
        num_scalar_prefetch=1,
        grid=(N_TOK // TG,),
        in_specs=[
            pl.BlockSpec(memory_space=pl.ANY),
            pl.BlockSpec((TG, D_MODEL), lambda i, p: (i, 0)),
            pl.BlockSpec((None, None, None, 1, D_MODEL),
                         lambda i, p: (layer, 5, _group_of_row(i * TG), 0, 0)),
            pl.BlockSpec((TG, LANES), lambda i, p: (i, 0)),
        ],
        out_specs=pl.BlockSpec((TG, D_MODEL), lambda i, p: (i, 0)),
        scratch_shapes=[pltpu.VMEM((TG * ROW_TILES, LANES), F32), pltpu.VMEM((TG * ROW_TILES, LANES), F32),
                        pltpu.SemaphoreType.DMA((2,))],
    )
    return pl.pallas_call(
        _combine_kernel,
        grid_spec=grid_spec,
        out_shape=jax.ShapeDtypeStruct((N_TOK, D_MODEL), F32),
        compiler_params=_params(("arbitrary",)),
        name="moe_combine",
    )(pos, yb, res, mod, top_w)


def _moe_layer(x, gain, mod, layer, w_router, w_in, w_out, widx):
    h, info, top_w, cnt = _router(x, gain, mod, layer, w_router)
    experts = jnp.arange(N_EXPERTS, dtype=jnp.int32)
    counts = cnt[0, :N_EXPERTS]
    padded = (counts + TM_MOE - 1) // TM_MOE * TM_MOE
    pad_end = jnp.cumsum(padded)
    pad_start = pad_end - padded
    e_sel = info[:, :TOP_K]
    start_sel = jnp.sum(jnp.where(e_sel[..., None] == experts, pad_start, 0), axis=-1)
    dest = (start_sel + info[:, TOP_K:2 * TOP_K]).reshape(-1).astype(jnp.int32)
    row_tok = jnp.zeros((N_PAD,), jnp.int32).at[dest].set(
        jnp.arange(N_SLOT, dtype=jnp.int32) // TOP_K, unique_indices=True)
    tile_start = jnp.arange(N_MOE_TILES, dtype=jnp.int32) * TM_MOE
    n_before = jnp.sum(pad_end[None, :] <= tile_start[:, None], axis=1)
    used = tile_start < pad_end[-1]
    last_expert = jnp.sum(pad_end < pad_end[-1])
    tile_expert = jnp.where(used, jnp.minimum(n_before, N_EXPERTS - 1), last_expert).astype(jnp.int32)
    seg_end = jnp.sum(jnp.where(tile_expert[:, None] == experts, pad_start + counts, 0), axis=-1)
    rows_used = jnp.clip(seg_end - tile_start, 0, TM_MOE)
    tile_nsub = jnp.where(used, (rows_used + SUB_MOE - 1) // SUB_MOE, 0).astype(jnp.int32)
    yb = _moe_ffn(h, row_tok, tile_expert, tile_nsub, w_in, w_out, widx)
    return _combine(yb, dest, x, mod, layer, top_w)


def _conv_kernel(x_ref, w_ref, b_ref, o_ref):
    x = x_ref[...]
    n = x.shape[0]
    zeros = jnp.zeros((SUBLANES, x.shape[1]), F32)
    xp = jnp.concatenate([zeros, x, zeros], axis=0)
    acc = b_ref[...] + w_ref[SSD_CONV // 2:SSD_CONV // 2 + 1, :] * x
    for k in range(SSD_CONV):
        s = k - SSD_CONV // 2
        if s == 0:
            continue
        xs = pltpu.roll(xp, (-s) % (n + 2 * SUBLANES), axis=0)[SUBLANES:SUBLANES + n]
        acc = acc + w_ref[k:k + 1, :] * xs
    o_ref[...] = _silu(acc)


def _ssd_conv(zx, conv_w, conv_b, row0, n_seq, seq_len):
    tn = 512
    c0 = SSD_INNER // tn
    r0 = row0 // seq_len
    return pl.pallas_call(
        _conv_kernel,
        grid=(n_seq, SSD_CONV_DIM // tn),
        in_specs=[
            pl.BlockSpec((seq_len, tn), lambda b, j: (r0 + b, c0 + j)),
            pl.BlockSpec((SSD_CONV, tn), lambda b, j: (0, j)),
            pl.BlockSpec((1, tn), lambda b, j: (0, j)),
        ],
        out_specs=pl.BlockSpec((seq_len, tn), lambda b, j: (b, j)),
        out_shape=jax.ShapeDtypeStruct((n_seq * seq_len, SSD_CONV_DIM), F32),
        compiler_params=_params(("arbitrary", "arbitrary")),
        name="ssd_conv",
    )(zx, conv_w, conv_b[None])


def _softplus(x):
    return jnp.maximum(x, 0.0) + jnp.log(1.0 + jnp.exp(-jnp.abs(x)))


def _ssd_scan_kernel(*refs, reverse, has_h0, out_state):
    refs = list(refs)
    xbc_ref, dt_ref, dtb_ref, alog_ref = refs[:4]
    pos = 4
    h0_ref = None
    if has_h0:
        h0_ref = refs[pos]
        pos += 1
    y_ref = refs[pos]
    pos += 1
    sf_ref = None
    if out_state:
        sf_ref = refs[pos]
        pos += 1
    s_ref = refs[pos]
    c = pl.program_id(1)
    q = SSD_CHUNK

    @pl.when(c == 0)
    def _():
        if has_h0:
            s_ref[...] = h0_ref[...]
        else:
            s_ref[...] = jnp.zeros_like(s_ref)

    dt = _softplus(dt_ref[...] + dtb_ref[...])
    da = dt * (-jnp.exp(alog_ref[...]))
    ri = lax.broadcasted_iota(jnp.int32, (q, q), 0)
    ci = lax.broadcasted_iota(jnp.int32, (q, q), 1)
    reach = (ri <= ci) if reverse else (ri >= ci)
    cs = jnp.dot(jnp.where(reach, 1.0, 0.0), da, precision=HIGHEST, preferred_element_type=F32)
    cs_t = cs.T
    cs_end = cs[0:1, :] if reverse else cs[q - 1:q, :]
    dec = jnp.exp(cs_end)
    dt_t = dt.T
    ecs_t = jnp.exp(cs).T
    w_t = (dt * jnp.exp(cs_end - cs)).T
    x_t = xbc_ref[:, 0:SSD_INNER].T
    col0 = SSD_HEADS if reverse else 0
    gn = SSD_GROUPS * SSD_STATE
    ys, states = [], []
    for g in range(SSD_GROUPS):
        bg = xbc_ref[:, SSD_INNER + g * SSD_STATE:SSD_INNER + (g + 1) * SSD_STATE].astype(BF16)
        cg = xbc_ref[:, SSD_INNER + gn + g * SSD_STATE:SSD_INNER + gn + (g + 1) * SSD_STATE].astype(BF16)
        cb = _nt_dot(cg, bg)
        for e in range(HEADS_PER_GROUP):
            h = g * HEADS_PER_GROUP + e
            col = col0 + h
            hs = slice(h * SSD_HEADDIM, (h + 1) * SSD_HEADDIM)
            xh_t = x_t[hs, :]
            seg = cs[:, col:col + 1] - cs_t[col:col + 1, :]
            decay = jnp.exp(jnp.where(reach, seg, -jnp.inf))
            state = s_ref[hs, :]
            y_diag_t = _nt_dot((xh_t * dt_t[col:col + 1, :]).astype(BF16), (cb * decay).astype(BF16))
            y_off_t = _nt_dot(state.astype(BF16), cg) * ecs_t[col:col + 1, :]
            ys.append(y_diag_t + y_off_t)
            xw_t = (xh_t * w_t[col:col + 1, :]).astype(BF16)
            states.append(dec[0:1, col:col + 1] * state + jnp.dot(xw_t, bg, preferred_element_type=F32))
    y_ref[...] = jnp.concatenate(ys, axis=0).T
    new_state = jnp.concatenate(states, axis=0)
    s_ref[...] = new_state

    if out_state:
        @pl.when(c == pl.num_programs(1) - 1)
        def _():
            sf_ref[...] = new_state


def _ssd_scan(xbc, dt_all, dt_bias, a_log, row0, n_seq, seq_len, reverse, h0=None, out_state=False):
    nc = seq_len // SSD_CHUNK
    c0 = row0 // SSD_CHUNK

    def chunk(c):
        return nc - 1 - c if reverse else c

    in_specs = [
        pl.BlockSpec((SSD_CHUNK, SSD_CONV_DIM), lambda b, c: (b * nc + chunk(c), 0)),
        pl.BlockSpec((SSD_CHUNK, LANES), lambda b, c: (c0 + b * nc + chunk(c), 0)),
        pl.BlockSpec((1, LANES), lambda b, c: (0, 0)),
        pl.BlockSpec((1, LANES), lambda b, c: (0, 0)),
    ]
    args = [xbc, dt_all, dt_bias, a_log]
    if h0 is not None:
        in_specs.append(pl.BlockSpec((None, SSD_INNER, SSD_STATE), lambda b, c: (b, 0, 0)))
        args.append(h0)
    out_specs = [pl.BlockSpec((SSD_CHUNK, SSD_INNER), lambda b, c: (b * nc + chunk(c), 0))]
    out_shape = [jax.ShapeDtypeStruct((n_seq * seq_len, SSD_INNER), F32)]
    if out_state:
        out_specs.append(pl.BlockSpec((None, SSD_INNER, SSD_STATE), lambda b, c: (b, 0, 0)))
        out_shape.append(jax.ShapeDtypeStruct((n_seq, SSD_INNER, SSD_STATE), F32))
    return pl.pallas_call(
        functools.partial(_ssd_scan_kernel, reverse=reverse, has_h0=h0 is not None, out_state=out_state),
        grid=(n_seq, nc),
        in_specs=in_specs,
        out_specs=out_specs,
        out_shape=out_shape,
        scratch_shapes=[pltpu.VMEM((SSD_INNER, SSD_STATE), F32)],
        compiler_params=_params(("arbitrary", "arbitrary")),
        name="ssd_scan",
    )(*args)


def _ssd_out_kernel(yf_ref, yb_ref, x_ref, z_ref, d_ref, nw_ref, w_ref, r_ref, g_ref, o_ref):
    y = yf_ref[...] + yb_ref[...] + d_ref[...] * x_ref[...]
    y = y * _silu(z_ref[...])
    gw = SSD_INNER // SSD_GROUPS
    parts = []
    for g in range(SSD_GROUPS):
        yg = y[:, g * gw:(g + 1) * gw]
        yg = yg * lax.rsqrt(jnp.mean(yg * yg, axis=-1, keepdims=True) + NORM_EPS)
        parts.append((yg * nw_ref[:, g * gw:(g + 1) * gw]).astype(BF16))
    yn = jnp.concatenate(parts, axis=1)
    acc = jnp.dot(yn, w_ref[...].astype(BF16), preferred_element_type=F32)
    o_ref[...] = r_ref[...] + g_ref[...] * acc


def _ssd_out(yf, yb, xbc, zx, d_exp, norm_w, w_out, widx, res, mod, layer, row0, n_rows):
    tm = TM_SSD_OUT
    r0 = row0 // tm
    return pl.pallas_call(
        _ssd_out_kernel,
        grid=(n_rows // tm,),
        in_specs=[
            pl.BlockSpec((tm, SSD_INNER), lambda i: (i, 0)),
            pl.BlockSpec((tm, SSD_INNER), lambda i: (i, 0)),
            pl.BlockSpec((tm, SSD_INNER), lambda i: (i, 0)),
            pl.BlockSpec((tm, SSD_INNER), lambda i: (r0 + i, 0)),
            pl.BlockSpec((1, SSD_INNER), lambda i: (0, 0)),
            pl.BlockSpec((1, SSD_INNER), lambda i: (0, 0)),
            pl.BlockSpec((None, SSD_INNER, D_MODEL), lambda i: (widx, 0, 0)),
            pl.BlockSpec((tm, D_MODEL), lambda i: (r0 + i, 0)),
            pl.BlockSpec((None, None, None, 1, D_MODEL),
                         lambda i: (layer, 2, _group_of_row(row0 + i * tm), 0, 0)),
        ],
        out_specs=pl.BlockSpec((tm, D_MODEL), lambda i: (i, 0)),
        out_shape=jax.ShapeDtypeStruct((n_rows, D_MODEL), F32),
        compiler_params=_params(("arbitrary",)),
        name="ssd_out",
    )(yf, yb, xbc, zx, d_exp, norm_w, w_out, res, mod)


def _ssd_layer(x, gain, mod, layer, j, state_f, state_b, w_in, conv_w, conv_b, dt_bias, a_log, d_skip,
               norm_w, w_out):
    n_zx = SSD_INNER + SSD_CONV_DIM
    zx = _nm_matmul(x, gain, mod, layer, w_in, j, n_zx, name="ssd_in_proj")
    pad = LANES - 2 * SSD_HEADS
    w_dt = jnp.pad(w_in[j][:, n_zx:], ((0, 0), (0, pad)))[None]
    dt_all = _nm_matmul(x, gain, mod, layer, w_dt, 0, LANES, tn=LANES, name="ssd_dt_proj")
    dtb = jnp.pad(dt_bias.reshape(1, -1), ((0, 0), (0, pad)))
    alog = jnp.pad(a_log.reshape(1, -1), ((0, 0), (0, pad)))
    d_exp = jnp.repeat(d_skip, SSD_HEADDIM)[None]
    outs, states = [], []
    for row0, n_seq, seq_len, h0s in ((0, BATCH, SEQ, None), (N_PROMPT, DEC_BATCH, DEC_SEQ, (state_f, state_b))):
        xbc = _ssd_conv(zx, conv_w, conv_b, row0, n_seq, seq_len)
        ys = []
        for reverse in (False, True):
            h0 = None if h0s is None else h0s[int(reverse)].reshape(n_seq, SSD_INNER, SSD_STATE)
            res = _ssd_scan(xbc, dt_all, dtb, alog, row0, n_seq, seq_len, reverse, h0=h0,
                            out_state=h0s is None)
            ys.append(res[0])
            if h0s is None:
                states.append(res[1])
        outs.append(_ssd_out(ys[0], ys[1], xbc, zx, d_exp, norm_w[None], w_out, j, x, mod, layer,
                             row0, n_seq * seq_len))
    shape = (BATCH, SSD_HEADS, SSD_HEADDIM, SSD_STATE)
    return tuple(outs), states[0].reshape(shape), states[1].reshape(shape)


def _na_layer(x, gain, mod, layer, j, cache_k, cache_v, w_qkv, w_o, q_norm, k_norm, rpb):
    w = NA_HEADS * HEAD_DIM
    qp, kp, vp = _qkv_proj(x, gain, mod, layer, w_qkv, j, q_norm, k_norm, w, 0, N_PROMPT, F32)
    qs, ks, vs = _qkv_proj(x, gain, mod, layer, w_qkv, j, q_norm, k_norm, w, N_PROMPT, N_SAMPLE, BF16)
    op = _ctx_attn(qp, kp, vp)
    ck = cache_k.reshape(DEC_BATCH, PAST_LEN, w).astype(BF16)
    cv = cache_v.reshape(DEC_BATCH, PAST_LEN, w).astype(BF16)
    os_ = _na_latent(qs, ks, vs, ck, cv, _na_bias_table(rpb))
    shape = (BATCH, SEQ, NA_HEADS, HEAD_DIM)
    return _linear_res((op, os_), w_o, j, x, mod, layer, 2), kp.reshape(shape), vp.reshape(shape)


def _swa_layer(x, gain, mod, layer, j, cache_k, cache_v, w_qkv, w_o, q_norm, k_norm, sink):
    kw = SWA_KV_HEADS * HEAD_DIM
    qp, kp, vp = _qkv_proj(x, gain, mod, layer, w_qkv, j, q_norm, k_norm, kw, 0, N_PROMPT, F32)
    qs, ks, vs = _qkv_proj(x, gain, mod, layer, w_qkv, j, q_norm, k_norm, kw, N_PROMPT, N_SAMPLE, BF16,
                           rope_tabs=_rope_tables())
    sink = sink.astype(F32)
    op = _ctx_attn(qp, kp, vp, sink)
    ck = cache_k.reshape(DEC_BATCH, PAST_LEN, kw).astype(BF16)
    cv = cache_v.reshape(DEC_BATCH, PAST_LEN, kw).astype(BF16)
    os_ = _swa_latent(qs, ks, vs, ck, cv, sink)
    shape = (BATCH, SEQ, SWA_KV_HEADS, HEAD_DIM)
    return _linear_res((op, os_), w_o, j, x, mod, layer, 2), kp.reshape(shape), vp.reshape(shape)


def kernel(x_prompt, x_sample, cache_na_k, cache_na_v, cache_swa_k, cache_swa_v, state_ssd_fwd, state_ssd_bwd, c, c_ctx, ada_w, ada_b, norm_mix, norm_ffn, na_w_qkv, na_w_o, na_q_norm, na_k_norm, na_rpb, swa_w_qkv, swa_w_o, swa_q_norm, swa_k_norm, swa_sink, ssd_w_in, ssd_conv_w, ssd_conv_b, ssd_dt_bias, ssd_a_log, ssd_d, ssd_norm, ssd_w_out, ffn_w_in, ffn_w_out, moe_router, moe_w_in, moe_w_out):
    x = (x_prompt.reshape(N_PROMPT, D_MODEL), x_sample.reshape(N_SAMPLE, D_MODEL))
    cond = jnp.concatenate([c_ctx[None], c, jnp.zeros((N_GROUPS - 1 - DEC_BATCH, D_MODEL), F32)], axis=0)
    mod = _adaln(cond, ada_w, ada_b)
    ffn_in_bf, ffn_out_bf = ffn_w_in.astype(BF16), ffn_w_out.astype(BF16)
    na_w_qkv, na_w_o = na_w_qkv.astype(BF16), na_w_o.astype(BF16)
    swa_w_qkv, swa_w_o = swa_w_qkv.astype(BF16), swa_w_o.astype(BF16)
    ssd_w_in = ssd_w_in.astype(BF16)
    na_k, na_v, swa_k, swa_v, ssd_f, ssd_b = [], [], [], [], [], []
    for i in range(DEPTH):
        kind, j = i % N_MIXERS, i // N_MIXERS
        g_mix, g_ffn = norm_mix[i][None], norm_ffn[i][None]
        if kind == 0:
            x, kc, vc = _na_layer(x, g_mix, mod, i, j, cache_na_k[:, j], cache_na_v[:, j], na_w_qkv,
                                  na_w_o, na_q_norm[j], na_k_norm[j], na_rpb[j])
            na_k.append(kc)
            na_v.append(vc)
        elif kind == 1:
            x, kc, vc = _swa_layer(x, g_mix, mod, i, j, cache_swa_k[:, j], cache_swa_v[:, j], swa_w_qkv,
                                   swa_w_o, swa_q_norm[j], swa_k_norm[j], swa_sink[j])
            swa_k.append(kc)
            swa_v.append(vc)
        else:
            x, sf, sb = _ssd_layer(x, g_mix, mod, i, j, state_ssd_fwd[:, j], state_ssd_bwd[:, j], ssd_w_in,
                                   ssd_conv_w[j], ssd_conv_b[j], ssd_dt_bias[j], ssd_a_log[j], ssd_d[j],
                                   ssd_norm[j], ssd_w_out)
            ssd_f.append(sf)
            ssd_b.append(sb)
        if i % 2 == 0:
            x = _dense_ffn(x, g_ffn, mod, i, ffn_in_bf, ffn_out_bf, i // 2)
        else:
            x = _moe_layer(x, g_ffn, mod, i, moe_router[i // 2], moe_w_in, moe_w_out, i // 2)
    yp = x[:N_PROMPT].reshape(BATCH, SEQ, D_MODEL)
    ys = x[N_PROMPT:].reshape(DEC_BATCH, DEC_SEQ, D_MODEL)
    return (yp, ys, jnp.stack(na_k, axis=1), jnp.stack(na_v, axis=1), jnp.stack(swa_k, axis=1),
            jnp.stack(swa_v, axis=1), jnp.stack(ssd_f, axis=1), jnp.stack(ssd_b, axis=1))
```

```python
import functools

import jax
import jax.numpy as jnp
from jax import lax
from jax.experimental import pallas as pl
from jax.experimental.pallas import tpu as pltpu

F32 = jnp.float32
BF16 = jnp.bfloat16
HIGHEST = lax.Precision.HIGHEST

D_MODEL = 1024
BATCH = 32
SEQ = 256
DEPTH = 4
DEC_BATCH = 4
DEC_SEQ = 1024
PAST_LEN = 256
GRID_W = 64
N_MIXERS = 3
HEAD_DIM = 64
NORM_EPS = 1e-6
ROPE_BASE = 10000.0
NA_HEADS = 16
NA_ROWS = 8
NA_COLS = 16
SWA_Q_HEADS = 16
SWA_KV_HEADS = 4
SWA_GROUP = SWA_Q_HEADS // SWA_KV_HEADS
SWA_WINDOW = 128
SWA_BLOCK = 128
SSD_INNER = 2 * D_MODEL
SSD_HEADDIM = 64
SSD_HEADS = SSD_INNER // SSD_HEADDIM
SSD_GROUPS = 4
SSD_STATE = 128
SSD_CONV = 5
SSD_CHUNK = 128
SSD_CONV_DIM = SSD_INNER + 2 * SSD_GROUPS * SSD_STATE
FFN_DIM = 2816
N_EXPERTS = 8
TOP_K = 2
EXPERT_DIM = 3584

N_PROMPT = BATCH * SEQ
N_SAMPLE = DEC_BATCH * DEC_SEQ
N_TOK = N_PROMPT + N_SAMPLE
N_GROUPS = 8
LANES = 128
SUBLANES = 8
MXU_DIM = 256
GRID_ROWS = DEC_SEQ // GRID_W
HEADS_PER_GROUP = SSD_HEADS // SSD_GROUPS
ROW_TILES = D_MODEL // LANES

VMEM_LIMIT = 56 * 1024 * 1024
TM_LIN = 1024
TN_LIN = 512
TM_FFN = 512
TF_DENSE = 1408
TF_MOE = 512
TM_MOE = 2048
SUB_MOE = 512
N_SUB = TM_MOE // SUB_MOE
N_SLOT = N_TOK * TOP_K
N_PAD = N_SLOT + N_EXPERTS * TM_MOE
N_MOE_TILES = N_PAD // TM_MOE
TG = 512
TM_ROUTER = 512
TM_SSD_OUT = 256
assert ROW_TILES == SUBLANES


def _params(sem):
    return pltpu.CompilerParams(dimension_semantics=sem, vmem_limit_bytes=VMEM_LIMIT)


def _group_of_row(start):
    return jnp.where(start < N_PROMPT, 0, 1 + (start - N_PROMPT) // DEC_SEQ)


def _silu(x):
    return x / (1.0 + jnp.exp(-x))


def _norm_mod(x, gain, shift, scale):
    y = x * lax.rsqrt(jnp.mean(x * x, axis=-1, keepdims=True) + NORM_EPS)
    return (y * gain) * (1.0 + scale) + shift


def _adaln_kernel(c_ref, w_ref, b_ref, o_ref):
    s = _silu(c_ref[...])
    o_ref[...] = jnp.dot(s, w_ref[...], precision=HIGHEST, preferred_element_type=F32) + b_ref[...]


def _adaln(cond, ada_w, ada_b):
    tn = 1024
    out = pl.pallas_call(
        _adaln_kernel,
        grid=(DEPTH, 6 * D_MODEL // tn),
        in_specs=[
            pl.BlockSpec((N_GROUPS, D_MODEL), lambda l, j: (0, 0)),
            pl.BlockSpec((None, D_MODEL, tn), lambda l, j: (l, 0, j)),
            pl.BlockSpec((None, 1, tn), lambda l, j: (l, 0, j)),
        ],
        out_specs=pl.BlockSpec((None, N_GROUPS, tn), lambda l, j: (l, 0, j)),
        out_shape=jax.ShapeDtypeStruct((DEPTH, N_GROUPS, 6 * D_MODEL), F32),
        compiler_params=_params(("arbitrary", "arbitrary")),
        name="adaln",
    )(cond, ada_w, ada_b.reshape(DEPTH, 1, 6 * D_MODEL))
    out = out.reshape(DEPTH, N_GROUPS, 6, D_MODEL)
    return jnp.transpose(out, (0, 2, 1, 3)).reshape(DEPTH, 6, N_GROUPS, 1, D_MODEL)


def _nm_matmul_kernel(x_ref, g_ref, sh_ref, sc_ref, w_ref, o_ref, h_ref):
    @pl.when(pl.program_id(1) == 0)
    def _():
        h_ref[...] = _norm_mod(x_ref[...], g_ref[...], sh_ref[...], sc_ref[...]).astype(BF16)

    o_ref[...] = jnp.dot(h_ref[...], w_ref[...].astype(BF16), preferred_element_type=F32)


def _nm_matmul(x, gain, mod, layer, w, widx, n_out, tn=TN_LIN, name="nm_matmul"):
    tm = TM_LIN
    return pl.pallas_call(
        _nm_matmul_kernel,
        grid=(N_TOK // tm, n_out // tn),
        in_specs=[
            pl.BlockSpec((tm, D_MODEL), lambda i, j: (i, 0)),
            pl.BlockSpec((1, D_MODEL), lambda i, j: (0, 0)),
            pl.BlockSpec((None, None, None, 1, D_MODEL),
                         lambda i, j: (layer, 0, _group_of_row(i * tm), 0, 0)),
            pl.BlockSpec((None, None, None, 1, D_MODEL),
                         lambda i, j: (layer, 1, _group_of_row(i * tm), 0, 0)),
            pl.BlockSpec((None, D_MODEL, tn), lambda i, j: (widx, 0, j)),
        ],
        out_specs=pl.BlockSpec((tm, tn), lambda i, j: (i, j)),
        out_shape=jax.ShapeDtypeStruct((N_TOK, n_out), F32),
        scratch_shapes=[pltpu.VMEM((tm, D_MODEL), BF16)],
        compiler_params=_params(("arbitrary", "arbitrary")),
        name=name,
    )(x, gain, mod, mod, w)


def _split_stream(arr, tm, n_col):
    a, b = arr if isinstance(arr, tuple) else (arr, arr)
    n_p = N_PROMPT // tm
    off = 0 if isinstance(arr, tuple) else n_p
    blk = (tm, a.shape[1] // n_col)

    def col(own, j, parked):
        return jnp.where(own, j, parked) if n_col > 1 else 0

    spec_a = pl.BlockSpec(blk, lambda i, j: (jnp.minimum(i, n_p - 1), col(i < n_p, j, n_col - 1)))
    spec_b = pl.BlockSpec(blk, lambda i, j: (off + jnp.maximum(i - n_p, 0), col(i >= n_p, j, 0)))
    return (a, b), (spec_a, spec_b)


def _pick_half(a_ref, b_ref, tm):
    return jnp.where(pl.program_id(0) < N_PROMPT // tm, a_ref[...], b_ref[...])


def _linear_res_kernel(xp_ref, xs_ref, w_ref, rp_ref, rs_ref, g_ref, o_ref):
    tm = o_ref.shape[0]
    acc = jnp.dot(_pick_half(xp_ref, xs_ref, tm), w_ref[...].astype(BF16), preferred_element_type=F32)
    o_ref[...] = _pick_half(rp_ref, rs_ref, tm) + g_ref[...] * acc


def _linear_res(x, w, widx, res, mod, layer, which):
    tm, tn = TM_LIN, TN_LIN
    n_col = D_MODEL // tn
    x_arrs, x_specs = _split_stream(x, tm, 1)
    r_arrs, r_specs = _split_stream(res, tm, n_col)
    k = x_arrs[0].shape[1]
    return pl.pallas_call(
        _linear_res_kernel,
        grid=(N_TOK // tm, n_col),
        in_specs=[
            *x_specs,
            pl.BlockSpec((None, k, tn), lambda i, j: (widx, 0, j)),
            *r_specs,
            pl.BlockSpec((None, None, None, 1, tn),
                         lambda i, j: (layer, which, _group_of_row(i * tm), 0, j)),
        ],
        out_specs=pl.BlockSpec((tm, tn), lambda i, j: (i, j)),
        out_shape=jax.ShapeDtypeStruct((N_TOK, D_MODEL), F32),
        compiler_params=_params(("arbitrary", "arbitrary")),
        name="linear_res",
    )(*x_arrs, w, *r_arrs, mod)


def _group_sumsq(x):
    r = lax.broadcasted_iota(jnp.int32, (MXU_DIM, MXU_DIM), 0) // HEAD_DIM
    c = lax.broadcasted_iota(jnp.int32, (MXU_DIM, MXU_DIM), 1) // HEAD_DIM
    ones = jnp.where(r == c, 1.0, 0.0).astype(BF16)
    outs = []
    for t in range(x.shape[1] // MXU_DIM):
        x2 = x[:, t * MXU_DIM:(t + 1) * MXU_DIM]
        x2 = x2 * x2
        hi = x2.astype(BF16)
        lo = (x2 - hi.astype(F32)).astype(BF16)
        outs.append(jnp.dot(hi, ones, preferred_element_type=F32)
                    + jnp.dot(lo, ones, preferred_element_type=F32))
    return jnp.concatenate(outs, axis=1) if len(outs) > 1 else outs[0]


def _head_norm(x, gain):
    return x * lax.rsqrt(_group_sumsq(x) * (1.0 / HEAD_DIM) + NORM_EPS) * gain


def _rope(x, cos, sin):
    w = x.shape[1]
    lane = lax.broadcasted_iota(jnp.int32, x.shape, 1)
    partner = jnp.where((lane % 32) < 16, pltpu.roll(x, w - 16, axis=1), pltpu.roll(x, 16, axis=1))
    return x * cos + partner * sin


def _qkv_kernel(*refs, nq, rope):
    if rope:
        (x_ref, g_ref, sh_ref, sc_ref, w_ref, qg_ref, kg_ref, cos_ref, sin_ref,
         q_ref, k_ref, v_ref, h_ref) = refs
    else:
        x_ref, g_ref, sh_ref, sc_ref, w_ref, qg_ref, kg_ref, q_ref, k_ref, v_ref, h_ref = refs
    j = pl.program_id(1)

    @pl.when(j == 0)
    def _():
        h_ref[...] = _norm_mod(x_ref[...], g_ref[...], sh_ref[...], sc_ref[...]).astype(BF16)

    acc = jnp.dot(h_ref[...], w_ref[...].astype(BF16), preferred_element_type=F32)

    def normed(gain_ref):
        y = _head_norm(acc, gain_ref[...])
        return _rope(y, cos_ref[...], sin_ref[...]) if rope else y

    @pl.when(j < nq)
    def _():
        q_ref[...] = (normed(qg_ref) * (HEAD_DIM ** -0.5)).astype(q_ref.dtype)

    @pl.when(j == nq)
    def _():
        k_ref[...] = normed(kg_ref).astype(k_ref.dtype)

    @pl.when(j == nq + 1)
    def _():
        v_ref[...] = acc.astype(v_ref.dtype)


def _qkv_proj(x, gain, mod, layer, w, widx, q_gain, k_gain, kw, row0, n_rows, kv_dtype, rope_tabs=None):
    tm = TM_LIN
    qw = NA_HEADS * HEAD_DIM
    nq = qw // kw
    r0 = row0 // tm
    if isinstance(x, tuple):
        x, r0 = (x[0], 0) if row0 == 0 else (x[1], 0)
    in_specs = [
        pl.BlockSpec((tm, D_MODEL), lambda i, j: (r0 + i, 0)),
        pl.BlockSpec((1, D_MODEL), lambda i, j: (0, 0)),
        pl.BlockSpec((None, None, None, 1, D_MODEL),
                     lambda i, j: (layer, 0, _group_of_row(row0 + i * tm), 0, 0)),
        pl.BlockSpec((None, None, None, 1, D_MODEL),
                     lambda i, j: (layer, 1, _group_of_row(row0 + i * tm), 0, 0)),
        pl.BlockSpec((None, D_MODEL, kw), lambda i, j: (widx, 0, j)),
        pl.BlockSpec((1, kw), lambda i, j: (0, 0)),
        pl.BlockSpec((1, kw), lambda i, j: (0, 0)),
    ]
    args = [x, gain, mod, mod, w, jnp.tile(q_gain, kw // HEAD_DIM)[None], jnp.tile(k_gain, kw // HEAD_DIM)[None]]
    if rope_tabs is not None:
        in_specs += [pl.BlockSpec((tm, kw), lambda i, j: (0, 0))] * 2
        args += [t[:, :kw] for t in rope_tabs]
    return pl.pallas_call(
        functools.partial(_qkv_kernel, nq=nq, rope=rope_tabs is not None),
        grid=(n_rows // tm, nq + 2),
        in_specs=in_specs,
        out_specs=[
            pl.BlockSpec((tm, kw), lambda i, j: (i, jnp.minimum(j, nq - 1))),
            pl.BlockSpec((tm, kw), lambda i, j: (i, 0)),
            pl.BlockSpec((tm, kw), lambda i, j: (i, 0)),
        ],
        out_shape=[
            jax.ShapeDtypeStruct((n_rows, qw), BF16),
            jax.ShapeDtypeStruct((n_rows, kw), kv_dtype),
            jax.ShapeDtypeStruct((n_rows, kw), kv_dtype),
        ],
        scratch_shapes=[pltpu.VMEM((tm, D_MODEL), BF16)],
        compiler_params=_params(("arbitrary", "arbitrary")),
        name="qkv_proj",
    )(*args)


def _rope_tables():
    quarter = HEAD_DIM // 4
    t = jnp.arange(DEC_SEQ)
    pos = jnp.stack([t // GRID_W, t % GRID_W], axis=-1).astype(F32)
    inv = ROPE_BASE ** (-jnp.arange(quarter, dtype=F32) / quarter)
    ang = pos[:, :, None] * inv
    cos, sin = jnp.cos(ang), jnp.sin(ang)
    cos64 = jnp.concatenate([cos[:, 0], cos[:, 0], cos[:, 1], cos[:, 1]], axis=1)
    sin64 = jnp.concatenate([-sin[:, 0], sin[:, 0], -sin[:, 1], sin[:, 1]], axis=1)
    return jnp.tile(cos64, (1, SWA_KV_HEADS)), jnp.tile(sin64, (1, SWA_KV_HEADS))


def _nt_dot(a, b):
    return lax.dot_general(a, b, (((1,), (1,)), ((), ())), preferred_element_type=F32)


def _tn_dot(a, b):
    return lax.dot_general(a, b, (((0,), (0,)), ((), ())), preferred_element_type=F32)


def _pair_queries(q):
    n = q.shape[0]
    ri = lax.broadcasted_iota(jnp.int32, (2 * n, LANES), 0)
    ci = lax.broadcasted_iota(jnp.int32, (2 * n, LANES), 1)
    return jnp.where((ri // n) == (ci // HEAD_DIM), jnp.concatenate([q, q], axis=0), jnp.zeros((), BF16))


def _pair_outputs(o2, l):
    n = o2.shape[0] // 2
    o2 = o2 / jnp.broadcast_to(l, (LANES, 2 * n)).T
    first = lax.broadcasted_iota(jnp.int32, (n, LANES), 1) < HEAD_DIM
    return jnp.where(first, o2[:n], o2[n:])


def _ctx_attn_pairs(q_ref, k_ref, v_ref, o_ref):
    outs = []
    for p in range(q_ref.shape[1] // LANES):
        sl = slice(p * LANES, (p + 1) * LANES)
        s = _nt_dot(k_ref[:, sl].astype(BF16), _pair_queries(q_ref[:, sl]))
        e = jnp.exp(s - jnp.max(s, axis=0, keepdims=True))
        l = jnp.sum(e, axis=0, keepdims=True)
        outs.append(_pair_outputs(_tn_dot(e.astype(BF16), v_ref[:, sl].astype(BF16)), l))
    o_ref[...] = jnp.concatenate(outs, axis=1).astype(o_ref.dtype)


def _stack_heads(q_ref, heads):
    parts = [q_ref[:, h * HEAD_DIM:(h + 1) * HEAD_DIM] for h in heads]
    return jnp.concatenate(parts, axis=0) if len(parts) > 1 else parts[0]


def _per_head_column(scalars, rows):
    gid = lax.broadcasted_iota(jnp.int32, (len(scalars) * rows, 1), 0) // rows
    col = jnp.full(gid.shape, scalars[-1], F32)
    for g in range(len(scalars) - 1):
        col = jnp.where(gid == g, scalars[g], col)
    return col


def _ctx_attn_kernel(*refs, group, use_sink):
    if use_sink:
        q_ref, k_ref, v_ref, sink_ref, o_ref = refs
    else:
        q_ref, k_ref, v_ref, o_ref = refs
    rows = q_ref.shape[0]
    if group == 1:
        _ctx_attn_pairs(q_ref, k_ref, v_ref, o_ref)
        return
    outs = []
    for kh in range(k_ref.shape[1] // HEAD_DIM):
        heads = range(kh * group, (kh + 1) * group)
        q = _stack_heads(q_ref, heads)
        k = k_ref[:, kh * HEAD_DIM:(kh + 1) * HEAD_DIM].astype(BF16)
        v = v_ref[:, kh * HEAD_DIM:(kh + 1) * HEAD_DIM].astype(BF16)
        s = _nt_dot(q, k)
        m = jnp.max(s, axis=-1, keepdims=True)
        if use_sink:
            sink = _per_head_column([sink_ref[h] for h in heads], rows)
            m = jnp.maximum(m, sink)
        e = jnp.exp(s - m)
        l = jnp.sum(e, axis=-1, keepdims=True)
        if use_sink:
            l = l + jnp.exp(sink - m)
        o = jnp.dot(e.astype(BF16), v, preferred_element_type=F32) / l
        outs += [o[g * rows:(g + 1) * rows] for g in range(group)]
    o_ref[...] = jnp.concatenate(outs, axis=1).astype(o_ref.dtype)


def _ctx_attn(q, k, v, sink=None):
    kw = k.shape[1]
    group = q.shape[1] // kw
    in_specs = [
        pl.BlockSpec((SEQ, q.shape[1]), lambda b: (b, 0)),
        pl.BlockSpec((SEQ, kw), lambda b: (b, 0)),
        pl.BlockSpec((SEQ, kw), lambda b: (b, 0)),
    ]
    args = [q, k, v]
    if sink is not None:
        in_specs.append(pl.BlockSpec(memory_space=pltpu.SMEM))
        args.append(sink)
    return pl.pallas_call(
        functools.partial(_ctx_attn_kernel, group=group, use_sink=sink is not None),
        grid=(BATCH,),
        in_specs=in_specs,
        out_specs=pl.BlockSpec((SEQ, q.shape[1]), lambda b: (b, 0)),
        out_shape=jax.ShapeDtypeStruct((N_PROMPT, q.shape[1]), BF16),
        compiler_params=_params(("arbitrary",)),
        name="ctx_attn",
    )(*args)


def _na_latent_kernel(q_ref, k_ref, v_ref, ck_ref, cv_ref, bias_ref, o_ref):
    r = pl.program_id(1)
    kr = NA_ROWS
    start = jnp.clip(r - kr // 2, 0, GRID_ROWS - kr)
    row0 = pl.multiple_of(start * GRID_W, GRID_W)
    win = pl.ds(row0, kr * GRID_W)
    outs = []
    for p in range(NA_HEADS // 2):
        sl = slice(p * LANES, (p + 1) * LANES)
        qd = _pair_queries(q_ref[:, sl])
        bias = bias_ref[p, pl.ds(start - r + kr - 1, kr)].reshape(kr * GRID_W, LANES)
        s_nb = _nt_dot(k_ref[win, sl], qd) + bias
        s_cx = _nt_dot(ck_ref[:, sl], qd)
        m = jnp.maximum(jnp.max(s_nb, axis=0, keepdims=True), jnp.max(s_cx, axis=0, keepdims=True))
        e_nb = jnp.exp(s_nb - m)
        e_cx = jnp.exp(s_cx - m)
        l = jnp.sum(e_nb, axis=0, keepdims=True) + jnp.sum(e_cx, axis=0, keepdims=True)
        o2 = _tn_dot(e_nb.astype(BF16), v_ref[win, sl]) + _tn_dot(e_cx.astype(BF16), cv_ref[:, sl])
        outs.append(_pair_outputs(o2, l))
    o_ref[...] = jnp.concatenate(outs, axis=1).astype(o_ref.dtype)


def _na_bias_table(rpb):
    col = jnp.arange(GRID_W)
    col_start = jnp.clip(col - NA_COLS // 2, 0, GRID_W - NA_COLS)
    col_ok = (col[None, :] >= col_start[:, None]) & (col[None, :] < col_start[:, None] + NA_COLS)
    dc = jnp.clip(col[None, :] - col[:, None], 1 - NA_COLS, NA_COLS - 1) + NA_COLS - 1
    pick = (dc[None] == jnp.arange(2 * NA_COLS - 1)[:, None, None]).astype(F32)
    t = jnp.einsum('hrc,cqk->hrqk', rpb.astype(F32), pick, precision=HIGHEST)
    t = jnp.where(col_ok[None, None], t, -jnp.inf)
    t = t.reshape(NA_HEADS // 2, 2, 2 * NA_ROWS - 1, GRID_W, GRID_W)
    t = jnp.transpose(t, (0, 2, 4, 1, 3))
    return t.reshape(NA_HEADS // 2, 2 * NA_ROWS - 1, GRID_W, 2 * GRID_W)


def _na_latent(q, k, v, ck, cv, bias):
    w = NA_HEADS * HEAD_DIM

    return pl.pallas_call(
        _na_latent_kernel,
        grid=(DEC_BATCH, GRID_ROWS),
        in_specs=[
            pl.BlockSpec((GRID_W, w), lambda b, r: (b * GRID_ROWS + r, 0)),
            pl.BlockSpec((DEC_SEQ, w), lambda b, r: (b, 0)),
            pl.BlockSpec((DEC_SEQ, w), lambda b, r: (b, 0)),
            pl.BlockSpec((None, PAST_LEN, w), lambda b, r: (b, 0, 0)),
            pl.BlockSpec((None, PAST_LEN, w), lambda b, r: (b, 0, 0)),
            pl.BlockSpec((NA_HEADS // 2, 2 * NA_ROWS - 1, GRID_W, 2 * GRID_W), lambda b, r: (0, 0, 0, 0)),
        ],
        out_specs=pl.BlockSpec((GRID_W, w), lambda b, r: (b * GRID_ROWS + r, 0)),
        out_shape=jax.ShapeDtypeStruct((N_SAMPLE, w), BF16),
        compiler_params=_params(("arbitrary", "arbitrary")),
        name="na_latent",
    )(q, k, v, ck, cv, bias)


SWA_SPAN = SWA_BLOCK + 2 * SWA_WINDOW


def _swa_latent_kernel(q_ref, k_ref, v_ref, ck_ref, cv_ref, sink_ref, o_ref):
    n = pl.program_id(1)
    k0 = pl.multiple_of(jnp.clip(n - 1, 0, DEC_SEQ // SWA_BLOCK - SWA_SPAN // SWA_BLOCK) * SWA_BLOCK,
                        SWA_BLOCK)
    rows = SWA_GROUP * SWA_BLOCK
    qpos = n * SWA_BLOCK + lax.broadcasted_iota(jnp.int32, (rows, SWA_SPAN), 0) % SWA_BLOCK
    kpos = k0 + lax.broadcasted_iota(jnp.int32, (rows, SWA_SPAN), 1)
    ok = jnp.abs(qpos - kpos) <= SWA_WINDOW
    outs = []
    for kh in range(SWA_KV_HEADS):
        ksl = slice(kh * HEAD_DIM, (kh + 1) * HEAD_DIM)
        heads = range(kh * SWA_GROUP, (kh + 1) * SWA_GROUP)
        q = _stack_heads(q_ref, heads)
        sink = _per_head_column([sink_ref[h] for h in heads], SWA_BLOCK)
        s_b = jnp.where(ok, _nt_dot(q, k_ref[pl.ds(k0, SWA_SPAN), ksl]), -jnp.inf)
        s_c = _nt_dot(q, ck_ref[:, ksl])
        m = jnp.maximum(jnp.maximum(jnp.max(s_b, axis=-1, keepdims=True),
                                    jnp.max(s_c, axis=-1, keepdims=True)), sink)
        e_b = jnp.exp(s_b - m)
        e_c = jnp.exp(s_c - m)
        l = (jnp.sum(e_b, axis=-1, keepdims=True) + jnp.sum(e_c, axis=-1, keepdims=True)
             + jnp.exp(sink - m))
        o = (jnp.dot(e_b.astype(BF16), v_ref[pl.ds(k0, SWA_SPAN), ksl], preferred_element_type=F32)
             + jnp.dot(e_c.astype(BF16), cv_ref[:, ksl], preferred_element_type=F32)) / l
        outs += [o[g * SWA_BLOCK:(g + 1) * SWA_BLOCK] for g in range(SWA_GROUP)]
    o_ref[...] = jnp.concatenate(outs, axis=1).astype(o_ref.dtype)


def _swa_latent(q, k, v, ck, cv, sink):
    qw = SWA_Q_HEADS * HEAD_DIM
    kw = SWA_KV_HEADS * HEAD_DIM
    nb = DEC_SEQ // SWA_BLOCK
    return pl.pallas_call(
        _swa_latent_kernel,
        grid=(DEC_BATCH, nb),
        in_specs=[
            pl.BlockSpec((SWA_BLOCK, qw), lambda b, n: (b * nb + n, 0)),
            pl.BlockSpec((DEC_SEQ, kw), lambda b, n: (b, 0)),
            pl.BlockSpec((DEC_SEQ, kw), lambda b, n: (b, 0)),
            pl.BlockSpec((None, PAST_LEN, kw), lambda b, n: (b, 0, 0)),
            pl.BlockSpec((None, PAST_LEN, kw), lambda b, n: (b, 0, 0)),
            pl.BlockSpec(memory_space=pltpu.SMEM),
        ],
        out_specs=pl.BlockSpec((SWA_BLOCK, qw), lambda b, n: (b * nb + n, 0)),
        out_shape=jax.ShapeDtypeStruct((N_SAMPLE, qw), BF16),
        compiler_params=_params(("arbitrary", "arbitrary")),
        name="swa_latent",
    )(q, k, v, ck, cv, sink)


def _swiglu_part(h, wg, wu, wo):
    g = jnp.dot(h, wg, preferred_element_type=F32)
    u = jnp.dot(h, wu, preferred_element_type=F32)
    a = (_silu(g) * u).astype(BF16)
    return jnp.dot(a, wo, preferred_element_type=F32)


def _dense_ffn_kernel(xp_ref, xs_ref, g_ref, sh_ref, sc_ref, wg_ref, wu_ref, wo_ref, gate_ref, o_ref,
                      h_ref, acc_ref):
    j = pl.program_id(1)
    tm = o_ref.shape[0]

    @pl.when(j == 0)
    def _():
        x = _pick_half(xp_ref, xs_ref, tm)
        h_ref[...] = _norm_mod(x, g_ref[...], sh_ref[...], sc_ref[...]).astype(BF16)

    part = _swiglu_part(h_ref[...], wg_ref[...], wu_ref[...], wo_ref[...])

    @pl.when(j == 0)
    def _():
        acc_ref[...] = part

    @pl.when(j > 0)
    def _():
        acc_ref[...] += part

    @pl.when(j == pl.num_programs(1) - 1)
    def _():
        o_ref[...] = _pick_half(xp_ref, xs_ref, tm) + gate_ref[...] * acc_ref[...]


def _dense_ffn(x, gain, mod, layer, w_in, w_out, widx):
    tm, tf = TM_FFN, TF_DENSE
    nf = FFN_DIM // tf
    x_arrs, x_specs = _split_stream(x, tm, 1)

    def mspec(which):
        return pl.BlockSpec((None, None, None, 1, D_MODEL),
                            lambda i, j: (layer, which, _group_of_row(i * tm), 0, 0))

    return pl.pallas_call(
        _dense_ffn_kernel,
        grid=(N_TOK // tm, nf),
        in_specs=[
            *x_specs,
            pl.BlockSpec((1, D_MODEL), lambda i, j: (0, 0)),
            mspec(3), mspec(4),
            pl.BlockSpec((None, D_MODEL, tf), lambda i, j: (widx, 0, j)),
            pl.BlockSpec((None, D_MODEL, tf), lambda i, j: (widx, 0, nf + j)),
            pl.BlockSpec((None, tf, D_MODEL), lambda i, j: (widx, j, 0)),
            mspec(5),
        ],
        out_specs=pl.BlockSpec((tm, D_MODEL), lambda i, j: (i, 0)),
        out_shape=jax.ShapeDtypeStruct((N_TOK, D_MODEL), F32),
        scratch_shapes=[pltpu.VMEM((tm, D_MODEL), BF16), pltpu.VMEM((tm, D_MODEL), F32)],
        compiler_params=_params(("arbitrary", "arbitrary")),
        name="dense_ffn",
    )(*x_arrs, gain, mod, mod, w_in, w_in, w_out, mod)


def _tile_rows(ref, sub, c, n):
    return ref.at[pl.ds(sub * n * ROW_TILES + c, n, stride=ROW_TILES), :]


def _moe_ffn_kernel(te_ref, ns_ref, tok_ref, x_hbm, wg_ref, wu_ref, wo_ref, o_ref,
                    xbuf_ref, sem, h_ref, acc_ref, wgb_ref, wub_ref, wob_ref):
    i, j = pl.program_id(0), pl.program_id(1)
    nsub = ns_ref[i]
    last = j == pl.num_programs(1) - 1

    def gather(tile, wait):
        for s in range(N_SUB):
            @pl.when(s < ns_ref[tile])
            def _():
                def body(r, carry):
                    src = 0 if wait else tok_ref[tile * TM_MOE + s * SUB_MOE + r]
                    cp = _row_copy(x_hbm, src, xbuf_ref, s * SUB_MOE + r, sem)
                    cp.wait() if wait else cp.start(priority=s % 2)
                    return carry

                lax.fori_loop(0, SUB_MOE, body, 0, unroll=8)

    @pl.when((i == 0) & (j == 0))
    def _():
        gather(0, wait=False)

    @pl.when(j == 0)
    def _():
        gather(i, wait=True)
        for s in range(N_SUB):
            @pl.when(s < nsub)
            def _():
                for c in range(ROW_TILES):
                    h_ref[pl.ds(s * SUB_MOE, SUB_MOE), c * LANES:(c + 1) * LANES] = _tile_rows(
                        xbuf_ref, s, c, SUB_MOE)[...].astype(BF16)

        @pl.when(i + 1 < pl.num_programs(0))
        def _():
            gather(i + 1, wait=False)

    @pl.when(nsub > 0)
    def _():
        wgb_ref[...] = wg_ref[...].astype(BF16)
        wub_ref[...] = wu_ref[...].astype(BF16)
        wob_ref[...] = wo_ref[...].astype(BF16)

    for s in range(N_SUB):
        rows = pl.ds(s * SUB_MOE, SUB_MOE)

        @pl.when(s < nsub)
        def _():
            part = _swiglu_part(h_ref[rows, :], wgb_ref[...], wub_ref[...], wob_ref[...])

            @pl.when(j == 0)
            def _():
                acc_ref[rows, :] = part

            @pl.when(j > 0)
            def _():
                acc_ref[rows, :] += part

            @pl.when(last)
            def _():
                for c in range(ROW_TILES):
                    _tile_rows(o_ref, s, c, SUB_MOE)[...] = acc_ref[rows, c * LANES:(c + 1) * LANES]

        @pl.when((s >= nsub) & last)
        def _():
            o_ref[pl.ds(s * SUB_MOE * ROW_TILES, SUB_MOE * ROW_TILES), :] = jnp.zeros(
                (SUB_MOE * ROW_TILES, LANES), F32)


def _moe_ffn(x_tiles, row_tok, tile_expert, tile_nsub, w_in, w_out, widx):
    tm, tf = TM_MOE, TF_MOE
    nf = EXPERT_DIM // tf

    def jj(i, j, ns):
        return jnp.where(ns[i] > 0, j, nf - 1)

    grid_spec = pltpu.PrefetchScalarGridSpec(
        num_scalar_prefetch=3,
        grid=(N_MOE_TILES, nf),
        in_specs=[
            pl.BlockSpec(memory_space=pl.ANY),
            pl.BlockSpec((None, None, D_MODEL, tf), lambda i, j, te, ns, tok: (widx, te[i], 0, jj(i, j, ns))),
            pl.BlockSpec((None, None, D_MODEL, tf),
                         lambda i, j, te, ns, tok: (widx, te[i], 0, nf + jj(i, j, ns))),
            pl.BlockSpec((None, None, tf, D_MODEL), lambda i, j, te, ns, tok: (widx, te[i], jj(i, j, ns), 0)),
        ],
        out_specs=pl.BlockSpec((tm * ROW_TILES, LANES), lambda i, j, te, ns, tok: (i, 0)),
        scratch_shapes=[pltpu.VMEM((tm * ROW_TILES, LANES), F32), pltpu.SemaphoreType.DMA(()),
                        pltpu.VMEM((tm, D_MODEL), BF16), pltpu.VMEM((tm, D_MODEL), F32),
                        pltpu.VMEM((D_MODEL, tf), BF16), pltpu.VMEM((D_MODEL, tf), BF16),
                        pltpu.VMEM((tf, D_MODEL), BF16)],
    )
    return pl.pallas_call(
        _moe_ffn_kernel,
        grid_spec=grid_spec,
        out_shape=jax.ShapeDtypeStruct((N_PAD * ROW_TILES, LANES), F32),
        compiler_params=_params(("arbitrary", "arbitrary")),
        name="moe_ffn",
    )(tile_expert, tile_nsub, row_tok, x_tiles, w_in, w_in, w_out)


def _router_kernel(x_ref, g_ref, sh_ref, sc_ref, wr_ref, h_ref, info_ref, w_ref, cnt_ref, base_ref):
    tm = x_ref.shape[0]

    @pl.when(pl.program_id(0) == 0)
    def _():
        base_ref[...] = jnp.zeros_like(base_ref)

    h = _norm_mod(x_ref[...], g_ref[...], sh_ref[...], sc_ref[...])
    for c in range(ROW_TILES):
        h_ref[pl.ds(c, tm, stride=ROW_TILES), :] = h[:, c * LANES:(c + 1) * LANES]
    w = wr_ref[...]
    h_hi, w_hi = h.astype(BF16), w.astype(BF16)
    h_lo, w_lo = (h - h_hi.astype(F32)).astype(BF16), (w - w_hi.astype(F32)).astype(BF16)
    logits = (jnp.dot(h_hi, w_hi, preferred_element_type=F32)
              + (jnp.dot(h_hi, w_lo, preferred_element_type=F32) + jnp.dot(h_lo, w_hi, preferred_element_type=F32)))
    lane = lax.broadcasted_iota(jnp.int32, logits.shape, 1)
    logits = jnp.where(lane < N_EXPERTS, logits, -jnp.inf)
    m1 = jnp.max(logits, axis=-1, keepdims=True)
    i1 = jnp.min(jnp.where(logits == m1, lane, LANES), axis=-1, keepdims=True)
    rest = jnp.where(lane == i1, -jnp.inf, logits)
    m2 = jnp.max(rest, axis=-1, keepdims=True)
    i2 = jnp.min(jnp.where(rest == m2, lane, LANES), axis=-1, keepdims=True)
    e2 = jnp.exp(m2 - m1)
    den = 1.0 + e2
    w_ref[...] = jnp.where(lane == 0, 1.0 / den, jnp.where(lane == 1, e2 / den, 0.0))
    chosen = jnp.where((lane == i1) | (lane == i2), 1.0, 0.0)
    rt = lax.broadcasted_iota(jnp.int32, (tm, tm), 0)
    ct = lax.broadcasted_iota(jnp.int32, (tm, tm), 1)
    earlier = jnp.where(ct < rt, 1.0, 0.0).astype(BF16)
    before = jnp.dot(earlier, chosen.astype(BF16), preferred_element_type=F32) + base_ref[0:1, :]
    r1 = jnp.sum(jnp.where(lane == i1, before, 0.0), axis=-1, keepdims=True).astype(jnp.int32)
    r2 = jnp.sum(jnp.where(lane == i2, before, 0.0), axis=-1, keepdims=True).astype(jnp.int32)
    info_ref[...] = jnp.where(lane == 0, i1, jnp.where(lane == 1, i2, jnp.where(
        lane == 2, r1, jnp.where(lane == 3, r2, 0))))
    total = base_ref[0:1, :] + jnp.sum(chosen, axis=0, keepdims=True)
    base_ref[...] = jnp.broadcast_to(total, base_ref.shape)
    cnt_ref[...] = jnp.broadcast_to(total, cnt_ref.shape).astype(jnp.int32)


def _router(x, gain, mod, layer, w_router):
    tm = TM_ROUTER
    wr = jnp.pad(w_router, ((0, 0), (0, LANES - N_EXPERTS)))

    def mspec(which):
        return pl.BlockSpec((None, None, None, 1, D_MODEL),
                            lambda i: (layer, which, _group_of_row(i * tm), 0, 0))

    return pl.pallas_call(
        _router_kernel,
        grid=(N_TOK // tm,),
        in_specs=[
            pl.BlockSpec((tm, D_MODEL), lambda i: (i, 0)),
            pl.BlockSpec((1, D_MODEL), lambda i: (0, 0)),
            mspec(3), mspec(4),
            pl.BlockSpec((D_MODEL, LANES), lambda i: (0, 0)),
        ],
        out_specs=[
            pl.BlockSpec((tm * ROW_TILES, LANES), lambda i: (i, 0)),
            pl.BlockSpec((tm, LANES), lambda i: (i, 0)),
            pl.BlockSpec((tm, LANES), lambda i: (i, 0)),
            pl.BlockSpec((SUBLANES, LANES), lambda i: (0, 0)),
        ],
        out_shape=[
            jax.ShapeDtypeStruct((N_TOK * ROW_TILES, LANES), F32),
            jax.ShapeDtypeStruct((N_TOK, LANES), jnp.int32),
            jax.ShapeDtypeStruct((N_TOK, LANES), F32),
            jax.ShapeDtypeStruct((SUBLANES, LANES), jnp.int32),
        ],
        scratch_shapes=[pltpu.VMEM((SUBLANES, LANES), F32)],
        compiler_params=_params(("arbitrary",)),
        name="router",
    )(x, gain, mod, mod, wr)


def _row_copy(src_hbm, row, dst_ref, r, sem):
    src = src_hbm.at[pl.ds(pl.multiple_of(row * ROW_TILES, ROW_TILES), ROW_TILES), :]
    dst = dst_ref.at[pl.ds(pl.multiple_of(r * ROW_TILES, ROW_TILES), ROW_TILES), :]
    return pltpu.make_async_copy(src, dst, sem)


def _combine_kernel(pos_ref, y_hbm, res_ref, gate_ref, w_ref, o_ref, a_ref, b_ref, sem):
    i = pl.program_id(0)
    buf = i % 2

    def copies(step, slot, wait):
        def body(r, carry):
            t = 2 * (step * TG + r)
            ca = _row_copy(y_hbm, 0 if wait else pos_ref[t], a_ref.at[slot], r, sem.at[slot])
            cb = _row_copy(y_hbm, 0 if wait else pos_ref[t + 1], b_ref.at[slot], r, sem.at[2 + slot])
            if wait:
                ca.wait()
                cb.wait()
            else:
                ca.start()
                cb.start(priority=1)
            return carry

        lax.fori_loop(0, TG, body, 0, unroll=8)

    @pl.when(i == 0)
    def _():
        copies(0, 0, wait=False)

    @pl.when(i + 1 < pl.num_programs(0))
    def _():
        copies(i + 1, 1 - buf, wait=False)

    copies(i, buf, wait=True)
    w = w_ref[...]
    w0, w1 = w[:, 0:1], w[:, 1:2]
    for c in range(ROW_TILES):
        cols = slice(c * LANES, (c + 1) * LANES)
        rows = pl.ds(c, TG, stride=ROW_TILES)
        mix = w0 * a_ref[buf, rows, :] + w1 * b_ref[buf, rows, :]
        o_ref[:, cols] = res_ref[:, cols] + gate_ref[:, cols] * mix


def _combine(yb, pos, res, mod, layer, top_w):
    grid_spec = pltpu.PrefetchScalarGridSpec(
        num_scalar_prefetch=1,
        grid=(N_TOK // TG,),
        in_specs=[
            pl.BlockSpec(memory_space=pl.ANY),
            pl.BlockSpec((TG, D_MODEL), lambda i, p: (i, 0)),
            pl.BlockSpec((None, None, None, 1, D_MODEL),
                         lambda i, p: (layer, 5, _group_of_row(i * TG), 0, 0)),
            pl.BlockSpec((TG, LANES), lambda i, p: (i, 0)),
        ],
        out_specs=pl.BlockSpec((TG, D_MODEL), lambda i, p: (i, 0)),
        scratch_shapes=[pltpu.VMEM((2, TG * ROW_TILES, LANES), F32), pltpu.VMEM((2, TG * ROW_TILES, LANES), F32),
                        pltpu.SemaphoreType.DMA((4,))],
    )
    return pl.pallas_call(
        _combine_kernel,
        grid_spec=grid_spec,
        out_shape=jax.ShapeDtypeStruct((N_TOK, D_MODEL), F32),
        compiler_params=_params(("arbitrary",)),
        name="moe_combine",
    )(pos, yb, res, mod, top_w)


def _moe_layer(x, gain, mod, layer, w_router, w_in, w_out, widx):
    h, info, top_w, cnt = _router(x, gain, mod, layer, w_router)
    experts = jnp.arange(N_EXPERTS, dtype=jnp.int32)
    counts = cnt[0, :N_EXPERTS]
    padded = (counts + TM_MOE - 1) // TM_MOE * TM_MOE
    pad_end = jnp.cumsum(padded)
    pad_start = pad_end - padded
    e_sel = info[:, :TOP_K]
    start_sel = jnp.sum(jnp.where(e_sel[..., None] == experts, pad_start, 0), axis=-1)
    dest = (start_sel + info[:, TOP_K:2 * TOP_K]).reshape(-1).astype(jnp.int32)
    row_tok = jnp.zeros((N_PAD,), jnp.int32).at[dest].set(
        jnp.arange(N_SLOT, dtype=jnp.int32) // TOP_K, unique_indices=True)
    tile_start = jnp.arange(N_MOE_TILES, dtype=jnp.int32) * TM_MOE
    n_before = jnp.sum(pad_end[None, :] <= tile_start[:, None], axis=1)
    used = tile_start < pad_end[-1]
    last_expert = jnp.sum(pad_end < pad_end[-1])
    tile_expert = jnp.where(used, jnp.minimum(n_before, N_EXPERTS - 1), last_expert).astype(jnp.int32)
    seg_end = jnp.sum(jnp.where(tile_expert[:, None] == experts, pad_start + counts, 0), axis=-1)
    rows_used = jnp.clip(seg_end - tile_start, 0, TM_MOE)
    tile_nsub = jnp.where(used, (rows_used + SUB_MOE - 1) // SUB_MOE, 0).astype(jnp.int32)
    yb = _moe_ffn(h, row_tok, tile_expert, tile_nsub, w_in, w_out, widx)
    return _combine(yb, dest, x, mod, layer, top_w)


def _conv_kernel(x_ref, w_ref, b_ref, o_ref):
    x = x_ref[...]
    n = x.shape[0]
    zeros = jnp.zeros((SUBLANES, x.shape[1]), F32)
    xp = jnp.concatenate([zeros, x, zeros], axis=0)
    acc = b_ref[...] + w_ref[SSD_CONV // 2:SSD_CONV // 2 + 1, :] * x
    for k in range(SSD_CONV):
        s = k - SSD_CONV // 2
        if s == 0:
            continue
        xs = pltpu.roll(xp, (-s) % (n + 2 * SUBLANES), axis=0)[SUBLANES:SUBLANES + n]
        acc = acc + w_ref[k:k + 1, :] * xs
    o_ref[...] = _silu(acc)


def _ssd_conv(zx, conv_w, conv_b, row0, n_seq, seq_len):
    tn = 512
    c0 = SSD_INNER // tn
    r0 = row0 // seq_len
    return pl.pallas_call(
        _conv_kernel,
        grid=(n_seq, SSD_CONV_DIM // tn),
        in_specs=[
            pl.BlockSpec((seq_len, tn), lambda b, j: (r0 + b, c0 + j)),
            pl.BlockSpec((SSD_CONV, tn), lambda b, j: (0, j)),
            pl.BlockSpec((1, tn), lambda b, j: (0, j)),
        ],
        out_specs=pl.BlockSpec((seq_len, tn), lambda b, j: (b, j)),
        out_shape=jax.ShapeDtypeStruct((n_seq * seq_len, SSD_CONV_DIM), F32),
        compiler_params=_params(("arbitrary", "arbitrary")),
        name="ssd_conv",
    )(zx, conv_w, conv_b[None])


def _softplus(x):
    return jnp.maximum(x, 0.0) + jnp.log(1.0 + jnp.exp(-jnp.abs(x)))


def _ssd_scan_kernel(*refs, reverse, has_h0, out_state):
    refs = list(refs)
    xbc_ref, dt_ref, dtb_ref, alog_ref = refs[:4]
    pos = 4
    h0_ref = None
    if has_h0:
        h0_ref = refs[pos]
        pos += 1
    y_ref = refs[pos]
    pos += 1
    sf_ref = None
    if out_state:
        sf_ref = refs[pos]
        pos += 1
    s_ref = refs[pos]
    c = pl.program_id(1)
    q = SSD_CHUNK

    @pl.when(c == 0)
    def _():
        if has_h0:
            s_ref[...] = h0_ref[...]
        else:
            s_ref[...] = jnp.zeros_like(s_ref)

    dt = _softplus(dt_ref[...] + dtb_ref[...])
    da = dt * (-jnp.exp(alog_ref[...]))
    ri = lax.broadcasted_iota(jnp.int32, (q, q), 0)
    ci = lax.broadcasted_iota(jnp.int32, (q, q), 1)
    reach = (ri <= ci) if reverse else (ri >= ci)
    cs = jnp.dot(jnp.where(reach, 1.0, 0.0), da, precision=HIGHEST, preferred_element_type=F32)
    cs_t = cs.T
    cs_end = cs[0:1, :] if reverse else cs[q - 1:q, :]
    dec = jnp.exp(cs_end)
    dt_t = dt.T
    ecs_t = jnp.exp(cs).T
    w_t = (dt * jnp.exp(cs_end - cs)).T
    x_t = xbc_ref[:, 0:SSD_INNER].T
    col0 = SSD_HEADS if reverse else 0
    gn = SSD_GROUPS * SSD_STATE
    ys, states = [], []
    for g in range(SSD_GROUPS):
        bg = xbc_ref[:, SSD_INNER + g * SSD_STATE:SSD_INNER + (g + 1) * SSD_STATE].astype(BF16)
        cg = xbc_ref[:, SSD_INNER + gn + g * SSD_STATE:SSD_INNER + gn + (g + 1) * SSD_STATE].astype(BF16)
        cb = _nt_dot(cg, bg)
        for e in range(HEADS_PER_GROUP):
            h = g * HEADS_PER_GROUP + e
            col = col0 + h
            hs = slice(h * SSD_HEADDIM, (h + 1) * SSD_HEADDIM)
            xh_t = x_t[hs, :]
            seg = cs[:, col:col + 1] - cs_t[col:col + 1, :]
            decay = jnp.exp(jnp.where(reach, seg, -jnp.inf))
            state = s_ref[hs, :]
            y_diag_t = _nt_dot((xh_t * dt_t[col:col + 1, :]).astype(BF16), (cb * decay).astype(BF16))
            y_off_t = _nt_dot(state.astype(BF16), cg) * ecs_t[col:col + 1, :]
            ys.append(y_diag_t + y_off_t)
            xw_t = (xh_t * w_t[col:col + 1, :]).astype(BF16)
            states.append(dec[0:1, col:col + 1] * state + jnp.dot(xw_t, bg, preferred_element_type=F32))
    y_ref[...] = jnp.concatenate(ys, axis=0).T
    new_state = jnp.concatenate(states, axis=0)
    s_ref[...] = new_state

    if out_state:
        @pl.when(c == pl.num_programs(1) - 1)
        def _():
            sf_ref[...] = new_state


def _ssd_scan(xbc, dt_all, dt_bias, a_log, row0, n_seq, seq_len, reverse, h0=None, out_state=False):
    nc = seq_len // SSD_CHUNK
    c0 = row0 // SSD_CHUNK

    def chunk(c):
        return nc - 1 - c if reverse else c

    in_specs = [
        pl.BlockSpec((SSD_CHUNK, SSD_CONV_DIM), lambda b, c: (b * nc + chunk(c), 0)),
        pl.BlockSpec((SSD_CHUNK, LANES), lambda b, c: (c0 + b * nc + chunk(c), 0)),
        pl.BlockSpec((1, LANES), lambda b, c: (0, 0)),
        pl.BlockSpec((1, LANES), lambda b, c: (0, 0)),
    ]
    args = [xbc, dt_all, dt_bias, a_log]
    if h0 is not None:
        in_specs.append(pl.BlockSpec((None, SSD_INNER, SSD_STATE), lambda b, c: (b, 0, 0)))
        args.append(h0)
    out_specs = [pl.BlockSpec((SSD_CHUNK, SSD_INNER), lambda b, c: (b * nc + chunk(c), 0))]
    out_shape = [jax.ShapeDtypeStruct((n_seq * seq_len, SSD_INNER), F32)]
    if out_state:
        out_specs.append(pl.BlockSpec((None, SSD_INNER, SSD_STATE), lambda b, c: (b, 0, 0)))
        out_shape.append(jax.ShapeDtypeStruct((n_seq, SSD_INNER, SSD_STATE), F32))
    return pl.pallas_call(
        functools.partial(_ssd_scan_kernel, reverse=reverse, has_h0=h0 is not None, out_state=out_state),
        grid=(n_seq, nc),
        in_specs=in_specs,
        out_specs=out_specs,
        out_shape=out_shape,
        scratch_shapes=[pltpu.VMEM((SSD_INNER, SSD_STATE), F32)],
        compiler_params=_params(("arbitrary", "arbitrary")),
        name="ssd_scan",
    )(*args)


def _ssd_out_kernel(yf_ref, yb_ref, x_ref, z_ref, d_ref, nw_ref, w_ref, r_ref, g_ref, o_ref):
    y = yf_ref[...] + yb_ref[...] + d_ref[...] * x_ref[...]
    y = y * _silu(z_ref[...])
    gw = SSD_INNER // SSD_GROUPS
    parts = []
    for g in range(SSD_GROUPS):
        yg = y[:, g * gw:(g + 1) * gw]
        yg = yg * lax.rsqrt(jnp.mean(yg * yg, axis=-1, keepdims=True) + NORM_EPS)
        parts.append((yg * nw_ref[:, g * gw:(g + 1) * gw]).astype(BF16))
    yn = jnp.concatenate(parts, axis=1)
    acc = jnp.dot(yn, w_ref[...].astype(BF16), preferred_element_type=F32)
    o_ref[...] = r_ref[...] + g_ref[...] * acc


def _ssd_out(yf, yb, xbc, zx, d_exp, norm_w, w_out, widx, res, mod, layer, row0, n_rows):
    tm = TM_SSD_OUT
    r0 = row0 // tm
    return pl.pallas_call(
        _ssd_out_kernel,
        grid=(n_rows // tm,),
        in_specs=[
            pl.BlockSpec((tm, SSD_INNER), lambda i: (i, 0)),
            pl.BlockSpec((tm, SSD_INNER), lambda i: (i, 0)),
            pl.BlockSpec((tm, SSD_INNER), lambda i: (i, 0)),
            pl.BlockSpec((tm, SSD_INNER), lambda i: (r0 + i, 0)),
            pl.BlockSpec((1, SSD_INNER), lambda i: (0, 0)),
            pl.BlockSpec((1, SSD_INNER), lambda i: (0, 0)),
            pl.BlockSpec((None, SSD_INNER, D_MODEL), lambda i: (widx, 0, 0)),
            pl.BlockSpec((tm, D_MODEL), lambda i: (r0 + i, 0)),
            pl.BlockSpec((None, None, None, 1, D_MODEL),
                         lambda i: (layer, 2, _group_of_row(row0 + i * tm), 0, 0)),
        ],
        out_specs=pl.BlockSpec((tm, D_MODEL), lambda i: (i, 0)),
        out_shape=jax.ShapeDtypeStruct((n_rows, D_MODEL), F32),
        compiler_params=_params(("arbitrary",)),
        name="ssd_out",
    )(yf, yb, xbc, zx, d_exp, norm_w, w_out, res, mod)


def _ssd_layer(x, gain, mod, layer, j, state_f, state_b, w_in, conv_w, conv_b, dt_bias, a_log, d_skip,
               norm_w, w_out):
    n_zx = SSD_INNER + SSD_CONV_DIM
    zx = _nm_matmul(x, gain, mod, layer, w_in, j, n_zx, name="ssd_in_proj")
    pad = LANES - 2 * SSD_HEADS
    w_dt = jnp.pad(w_in[j][:, n_zx:], ((0, 0), (0, pad)))[None]
    dt_all = _nm_matmul(x, gain, mod, layer, w_dt, 0, LANES, tn=LANES, name="ssd_dt_proj")
    dtb = jnp.pad(dt_bias.reshape(1, -1), ((0, 0), (0, pad)))
    alog = jnp.pad(a_log.reshape(1, -1), ((0, 0), (0, pad)))
    d_exp = jnp.repeat(d_skip, SSD_HEADDIM)[None]
    outs, states = [], []
    for row0, n_seq, seq_len, h0s in ((0, BATCH, SEQ, None), (N_PROMPT, DEC_BATCH, DEC_SEQ, (state_f, state_b))):
        xbc = _ssd_conv(zx, conv_w, conv_b, row0, n_seq, seq_len)
        ys = []
        for reverse in (False, True):
            h0 = None if h0s is None else h0s[int(reverse)].reshape(n_seq, SSD_INNER, SSD_STATE)
            res = _ssd_scan(xbc, dt_all, dtb, alog, row0, n_seq, seq_len, reverse, h0=h0,
                            out_state=h0s is None)
            ys.append(res[0])
            if h0s is None:
                states.append(res[1])
        outs.append(_ssd_out(ys[0], ys[1], xbc, zx, d_exp, norm_w[None], w_out, j, x, mod, layer,
                             row0, n_seq * seq_len))
    shape = (BATCH, SSD_HEADS, SSD_HEADDIM, SSD_STATE)
    return tuple(outs), states[0].reshape(shape), states[1].reshape(shape)


def _na_layer(x, gain, mod, layer, j, cache_k, cache_v, w_qkv, w_o, q_norm, k_norm, rpb):
    w = NA_HEADS * HEAD_DIM
    qp, kp, vp = _qkv_proj(x, gain, mod, layer, w_qkv, j, q_norm, k_norm, w, 0, N_PROMPT, F32)
    qs, ks, vs = _qkv_proj(x, gain, mod, layer, w_qkv, j, q_norm, k_norm, w, N_PROMPT, N_SAMPLE, BF16)
    op = _ctx_attn(qp, kp, vp)
    ck = cache_k.reshape(DEC_BATCH, PAST_LEN, w).astype(BF16)
    cv = cache_v.reshape(DEC_BATCH, PAST_LEN, w).astype(BF16)
    os_ = _na_latent(qs, ks, vs, ck, cv, _na_bias_table(rpb))
    shape = (BATCH, SEQ, NA_HEADS, HEAD_DIM)
    return _linear_res((op, os_), w_o, j, x, mod, layer, 2), kp.reshape(shape), vp.reshape(shape)


def _swa_layer(x, gain, mod, layer, j, cache_k, cache_v, w_qkv, w_o, q_norm, k_norm, sink):
    kw = SWA_KV_HEADS * HEAD_DIM
    qp, kp, vp = _qkv_proj(x, gain, mod, layer, w_qkv, j, q_norm, k_norm, kw, 0, N_PROMPT, F32)
    qs, ks, vs = _qkv_proj(x, gain, mod, layer, w_qkv, j, q_norm, k_norm, kw, N_PROMPT, N_SAMPLE, BF16,
                           rope_tabs=_rope_tables())
    sink = sink.astype(F32)
    op = _ctx_attn(qp, kp, vp, sink)
    ck = cache_k.reshape(DEC_BATCH, PAST_LEN, kw).astype(BF16)
    cv = cache_v.reshape(DEC_BATCH, PAST_LEN, kw).astype(BF16)
    os_ = _swa_latent(qs, ks, vs, ck, cv, sink)
    shape = (BATCH, SEQ, SWA_KV_HEADS, HEAD_DIM)
    return _linear_res((op, os_), w_o, j, x, mod, layer, 2), kp.reshape(shape), vp.reshape(shape)


def kernel(x_prompt, x_sample, cache_na_k, cache_na_v, cache_swa_k, cache_swa_v, state_ssd_fwd, state_ssd_bwd, c, c_ctx, ada_w, ada_b, norm_mix, norm_ffn, na_w_qkv, na_w_o, na_q_norm, na_k_norm, na_rpb, swa_w_qkv, swa_w_o, swa_q_norm, swa_k_norm, swa_sink, ssd_w_in, ssd_conv_w, ssd_conv_b, ssd_dt_bias, ssd_a_log, ssd_d, ssd_norm, ssd_w_out, ffn_w_in, ffn_w_out, moe_router, moe_w_in, moe_w_out):
    x = (x_prompt.reshape(N_PROMPT, D_MODEL), x_sample.reshape(N_SAMPLE, D_MODEL))
    cond = jnp.concatenate([c_ctx[None], c, jnp.zeros((N_GROUPS - 1 - DEC_BATCH, D_MODEL), F32)], axis=0)
    mod = _adaln(cond, ada_w, ada_b)
    ffn_in_bf, ffn_out_bf = ffn_w_in.astype(BF16), ffn_w_out.astype(BF16)
    na_w_qkv, na_w_o = na_w_qkv.astype(BF16), na_w_o.astype(BF16)
    swa_w_qkv, swa_w_o = swa_w_qkv.astype(BF16), swa_w_o.astype(BF16)
    ssd_w_in = ssd_w_in.astype(BF16)
    na_k, na_v, swa_k, swa_v, ssd_f, ssd_b = [], [], [], [], [], []
    for i in range(DEPTH):
        kind, j = i % N_MIXERS, i // N_MIXERS
        g_mix, g_ffn = norm_mix[i][None], norm_ffn[i][None]
        if kind == 0:
            x, kc, vc = _na_layer(x, g_mix, mod, i, j, cache_na_k[:, j], cache_na_v[:, j], na_w_qkv,
                                  na_w_o, na_q_norm[j], na_k_norm[j], na_rpb[j])
            na_k.append(kc)
            na_v.append(vc)
        elif kind == 1:
            x, kc, vc = _swa_layer(x, g_mix, mod, i, j, cache_swa_k[:, j], cache_swa_v[:, j], swa_w_qkv,
                                   swa_w_o, swa_q_norm[j], swa_k_norm[j], swa_sink[j])
            swa_k.append(kc)
            swa_v.append(vc)
        else:
            x, sf, sb = _ssd_layer(x, g_mix, mod, i, j, state_ssd_fwd[:, j], state_ssd_bwd[:, j], ssd_w_in,
                                   ssd_conv_w[j], ssd_conv_b[j], ssd_dt_bias[j], ssd_a_log[j], ssd_d[j],
                                   ssd_norm[j], ssd_w_out)
            ssd_f.append(sf)
            ssd_b.append(sb)
        if i % 2 == 0:
            x = _dense_ffn(x, g_ffn, mod, i, ffn_in_bf, ffn_out_bf, i // 2)
        else:
            x = _moe_layer(x, g_ffn, mod, i, moe_router[i // 2], moe_w_in, moe_w_out, i // 2)
    yp = x[:N_PROMPT].reshape(BATCH, SEQ, D_MODEL)
    ys = x[N_PROMPT:].reshape(DEC_BATCH, DEC_SEQ, D_MODEL)
    return (yp, ys, jnp.stack(na_k, axis=1), jnp.stack(na_v, axis=1), jnp.stack(swa_k, axis=1),
            jnp.stack(swa_v, axis=1), jnp.stack(ssd_f, axis=1), jnp.stack(ssd_b, axis=1))
```
